```python
import jax, jax.numpy as jnp
from jax import lax
import numpy as np

D_MODEL = 1024
BATCH = 8
SEQ = 4096
DEPTH = 4

N_MEM = 256
MLA_HEADS = 4
MLA_Q_RANK = 256
MLA_KV_RANK = 128
MLA_NOPE_DIM = 64
MLA_ROPE_DIM = 32
MLA_V_DIM = 64
FOX_HEADS = 6
FOX_HEAD_DIM = 64
FOX_GATE_BIAS_MEAN = 3.0
DIL_HEADS = 6
DIL_HEAD_DIM = 64
DILATED_PAIRS = ((128, 1), (512, 4), (2048, 16))
D_MIX = MLA_HEADS * MLA_V_DIM + FOX_HEADS * FOX_HEAD_DIM + DIL_HEADS * DIL_HEAD_DIM
IN_SPLITS = (MLA_Q_RANK, MLA_KV_RANK, MLA_ROPE_DIM,
             3 * FOX_HEADS * FOX_HEAD_DIM, FOX_HEADS,
             3 * DIL_HEADS * DIL_HEAD_DIM)
IN_COLS = sum(IN_SPLITS)
XATTN_HEADS = 4
N_EXPERTS = 16
N_GROUPS = 4
EXPERTS_PER_GROUP = N_EXPERTS // N_GROUPS
TOP_K = 2
D_EXPERT = 256
ROPE_THETA = 10000.0
Q_BLOCK = 128
DIL_BLOCK = 64
NORM_EPS = 1e-5
NEG_INF = -1e30
DEEPNORM_ALPHA = (2 * DEPTH) ** 0.25
DEEPNORM_BETA = (8 * DEPTH) ** -0.25

kernel_name = "hybrid_mla_fox_dilated_moe_deepnorm"


def layer_norm(x, g, b):
    xf = x.astype(jnp.float32)
    mu = jnp.mean(xf, axis=-1, keepdims=True)
    var = jnp.mean(jnp.square(xf - mu), axis=-1, keepdims=True)
    return ((xf - mu) * lax.rsqrt(var + NORM_EPS) * g.astype(jnp.float32) + b.astype(jnp.float32)).astype(x.dtype)


def rms_norm(x, g):
    xf = x.astype(jnp.float32)
    ms = jnp.mean(jnp.square(xf), axis=-1, keepdims=True)
    return (xf * lax.rsqrt(ms + NORM_EPS) * g.astype(jnp.float32)).astype(x.dtype)


def rope(x, positions):
    half = x.shape[-1] // 2
    inv_freq = ROPE_THETA ** (-jnp.arange(half, dtype=jnp.float32) / half)
    ang = positions.astype(jnp.float32)[:, None, :, None] * inv_freq
    cos, sin = jnp.cos(ang), jnp.sin(ang)
    xf = x.astype(jnp.float32)
    x1, x2 = xf[..., :half], xf[..., half:]
    return jnp.concatenate([x1 * cos - x2 * sin, x1 * sin + x2 * cos], axis=-1).astype(x.dtype)


def to_heads(t, n_heads):
    b, s, _ = t.shape
    return t.reshape(b, s, n_heads, -1).transpose(0, 2, 1, 3)


def from_heads(t):
    b, h, s, d = t.shape
    return t.transpose(0, 2, 1, 3).reshape(b, s, h * d)


def causal_block_attention(q, k, v, log_decay=None):
    B, H, S, Dk = q.shape
    Dv = v.shape[-1]
    nb = S // Q_BLOCK
    scale = Dk ** -0.5
    kf = k.astype(jnp.float32)
    vf = v.astype(jnp.float32)
    kpos = jnp.arange(S)
    qb = q.astype(jnp.float32).reshape(B, H, nb, Q_BLOCK, Dk).transpose(2, 0, 1, 3, 4)
    xs = (jnp.arange(nb), qb)
    if log_decay is not None:
        xs = xs + (log_decay.reshape(B, H, nb, Q_BLOCK).transpose(2, 0, 1, 3),)

    def one_block(args):
        i, q_i = args[0], args[1]
        s = jnp.einsum('bhqd,bhkd->bhqk', q_i, kf) * scale
        if log_decay is not None:
            s = s + args[2][..., :, None] - log_decay[:, :, None, :]
        qpos = i * Q_BLOCK + jnp.arange(Q_BLOCK)
        s = jnp.where(kpos[None, :] <= qpos[:, None], s, NEG_INF)
        p = jax.nn.softmax(s, axis=-1)
        return jnp.einsum('bhqk,bhkd->bhqd', p, vf)

    out = lax.map(one_block, xs)
    return out.transpose(1, 2, 0, 3, 4).reshape(B, H, S, Dv).astype(v.dtype)


def dilated_mixture_attention(q, k, v):
    B, H, S, D = q.shape
    nb = S // DIL_BLOCK
    scale = D ** -0.5
    kf = k.astype(jnp.float32)
    vf = v.astype(jnp.float32)
    qb = q.astype(jnp.float32).reshape(B, H, nb, DIL_BLOCK, D).transpose(2, 0, 1, 3, 4)

    def one_block(args):
        i, q_i = args
        qpos = i * DIL_BLOCK + jnp.arange(DIL_BLOCK)
        maxes, dens, nums = [], [], []
        for window, dilation in DILATED_PAIRS:
            offs = dilation * jnp.arange(window // dilation + 1)
            idx = qpos[:, None] - offs[None, :]
            valid = idx >= 0
            idx = jnp.maximum(idx, 0)
            k_g = jnp.take(kf, idx, axis=2)
            v_g = jnp.take(vf, idx, axis=2)
            s = jnp.einsum('bhqd,bhqnd->bhqn', q_i, k_g) * scale
            s = jnp.where(valid, s, NEG_INF)
            m = jnp.max(s, axis=-1, keepdims=True)
            p = jnp.exp(s - m)
            maxes.append(m)
            dens.append(jnp.sum(p, axis=-1, keepdims=True))
            nums.append(jnp.einsum('bhqn,bhqnd->bhqd', p, v_g))
        m_all = jnp.stack(maxes)
        w = jnp.exp(m_all - jnp.max(m_all, axis=0, keepdims=True))
        return jnp.sum(w * jnp.stack(nums), axis=0) / jnp.sum(w * jnp.stack(dens), axis=0)

    out = lax.map(one_block, (jnp.arange(nb), qb))
    return out.transpose(1, 2, 0, 3, 4).reshape(B, H, S, D).astype(v.dtype)


def hybrid_mixer(x, positions, w_in, b_forget, q_gain, kv_gain, w_uq, w_ukv, w_out):
    B, S, _ = x.shape
    h = x @ w_in
    cuts = np.cumsum(IN_SPLITS)[:-1].tolist()
    c_q, c_kv, k_rope, fox_qkv, fox_f, dil_qkv = jnp.split(h, cuts, axis=-1)

    q_a = to_heads(rms_norm(c_q, q_gain) @ w_uq, MLA_HEADS)
    kv_a = to_heads(rms_norm(c_kv, kv_gain) @ w_ukv, MLA_HEADS)
    q_a = jnp.concatenate([q_a[..., :MLA_NOPE_DIM], rope(q_a[..., MLA_NOPE_DIM:], positions)], axis=-1)
    k_r = jnp.broadcast_to(rope(k_rope[:, None], positions), (B, MLA_HEADS, S, MLA_ROPE_DIM))
    k_a = jnp.concatenate([kv_a[..., :MLA_NOPE_DIM], k_r], axis=-1)
    o_a = causal_block_attention(q_a, k_a, kv_a[..., MLA_NOPE_DIM:])

    q_b, k_b, v_b = [to_heads(t, FOX_HEADS) for t in jnp.split(fox_qkv, 3, axis=-1)]
    log_f = jax.nn.log_sigmoid(fox_f.astype(jnp.float32) + b_forget.astype(jnp.float32))
    cum_log_f = jnp.cumsum(log_f, axis=1).transpose(0, 2, 1)
    o_b = causal_block_attention(q_b, k_b, v_b, cum_log_f)

    q_c, k_c, v_c = [to_heads(t, DIL_HEADS) for t in jnp.split(dil_qkv, 3, axis=-1)]
    o_c = dilated_mixture_attention(rope(q_c, positions), rope(k_c, positions), v_c)

    mixed = jnp.concatenate([from_heads(o_a), from_heads(o_b), from_heads(o_c)], axis=-1)
    return mixed @ w_out


def memory_cross_attention(x, mem, w_q, w_kv, w_o):
    q = to_heads(x @ w_q, XATTN_HEADS).astype(jnp.float32)
    k, v = [to_heads(t, XATTN_HEADS).astype(jnp.float32) for t in jnp.split(mem @ w_kv, 2, axis=-1)]
    s = jnp.einsum('bhqd,bhkd->bhqk', q, k) * (q.shape[-1] ** -0.5)
    p = jax.nn.softmax(s, axis=-1)
    o = jnp.einsum('bhqk,bhkd->bhqd', p, v).astype(x.dtype)
    return from_heads(o) @ w_o


def shared_group_router(x, router_w, router_bias):
    B, S, _ = x.shape
    scores = jax.nn.sigmoid((x @ router_w).astype(jnp.float32))
    biased = scores + router_bias.astype(jnp.float32)
    grp = biased.reshape(B, S, N_GROUPS, EXPERTS_PER_GROUP)
    grp_score = jnp.sum(lax.top_k(grp, TOP_K)[0], axis=-1)
    g_sel = jnp.argmax(grp_score, axis=-1)
    in_group = g_sel[..., None] == jnp.arange(N_GROUPS)
    masked = jnp.where(in_group[..., None], grp, NEG_INF).reshape(B, S, N_EXPERTS)
    _, expert_idx = lax.top_k(masked, TOP_K)
    w = jnp.take_along_axis(scores, expert_idx, axis=-1)
    w = w / jnp.sum(w, axis=-1, keepdims=True)
    return jnp.sum(jax.nn.one_hot(expert_idx, N_EXPERTS, dtype=jnp.float32) * w[..., None], axis=-2)


def moe_ffn(x, gates, w_gate, w_up, w_down):
    def per_sequence(args):
        xs, gs = args
        hg = jnp.einsum('sd,edf->sef', xs, w_gate)
        hu = jnp.einsum('sd,edf->sef', xs, w_up)
        a = jax.nn.silu(hg) * hu * gs[..., None].astype(xs.dtype)
        return jnp.einsum('sef,efd->sd', a, w_down)
    return lax.map(per_sequence, (x, gates))


def setup_inputs(seed: int = 0) -> dict:
    key = jax.random.key(seed)
    ks = jax.random.split(key, 24)
    f32 = jnp.float32
    L = DEPTH

    def normal(k, shape, scale):
        return jax.random.normal(k, shape, f32) * scale

    def gain(k, shape):
        return 1.0 + 0.02 * jax.random.normal(k, shape, f32)

    def bias(k, shape):
        return 0.02 * jax.random.normal(k, shape, f32)

    beta = DEEPNORM_BETA
    positions = (jax.random.randint(ks[2], (BATCH, 1), 0, 1024, dtype=jnp.int32)
                 + jnp.arange(SEQ, dtype=jnp.int32)[None, :])
    return {
        "x": normal(ks[0], (BATCH, SEQ, D_MODEL), 1.0),
        "mem": normal(ks[1], (BATCH, N_MEM, D_MODEL), 1.0),
        "positions": positions,
        "w_in": normal(ks[3], (L, D_MODEL, IN_COLS), D_MODEL ** -0.5),
        "b_forget": FOX_GATE_BIAS_MEAN + 0.5 * jax.random.normal(ks[4], (L, FOX_HEADS), f32),
        "mla_q_gain": gain(ks[5], (L, MLA_Q_RANK)),
        "mla_kv_gain": gain(ks[6], (L, MLA_KV_RANK)),
        "mla_w_uq": normal(ks[7], (L, MLA_Q_RANK, MLA_HEADS * (MLA_NOPE_DIM + MLA_ROPE_DIM)), MLA_Q_RANK ** -0.5),
        "mla_w_ukv": normal(ks[8], (L, MLA_KV_RANK, MLA_HEADS * (MLA_NOPE_DIM + MLA_V_DIM)), MLA_KV_RANK ** -0.5),
        "w_mix_out": normal(ks[9], (L, D_MIX, D_MODEL), beta * D_MIX ** -0.5),
        "ln_mix_g": gain(ks[10], (L, D_MODEL)),
        "ln_mix_b": bias(ks[11], (L, D_MODEL)),
        "xattn_w_q": normal(ks[12], (L, D_MODEL, D_MODEL), D_MODEL ** -0.5),
        "xattn_w_kv": normal(ks[13], (L, D_MODEL, 2 * D_MODEL), D_MODEL ** -0.5),
        "xattn_w_o": normal(ks[14], (L, D_MODEL, D_MODEL), beta * D_MODEL ** -0.5),
        "ln_mem_g": gain(ks[15], (L, D_MODEL)),
        "ln_mem_b": bias(ks[16], (L, D_MODEL)),
        "router_w": normal(ks[17], (D_MODEL, N_EXPERTS), D_MODEL ** -0.5),
        "router_bias": normal(ks[18], (N_EXPERTS,), 0.01),
        "expert_w_gate": normal(ks[19], (L, N_EXPERTS, D_MODEL, D_EXPERT), D_MODEL ** -0.5),
        "expert_w_up": normal(ks[20], (L, N_EXPERTS, D_MODEL, D_EXPERT), D_MODEL ** -0.5),
        "expert_w_down": normal(ks[21], (L, N_EXPERTS, D_EXPERT, D_MODEL), beta * D_EXPERT ** -0.5),
        "ln_ffn_g": gain(ks[22], (L, D_MODEL)),
        "ln_ffn_b": bias(ks[23], (L, D_MODEL)),
    }


def reference(x, mem, positions, w_in, b_forget, mla_q_gain, mla_kv_gain, mla_w_uq, mla_w_ukv,
              w_mix_out, ln_mix_g, ln_mix_b, xattn_w_q, xattn_w_kv, xattn_w_o, ln_mem_g, ln_mem_b,
              router_w, router_bias, expert_w_gate, expert_w_up, expert_w_down, ln_ffn_g, ln_ffn_b):
    for l in range(DEPTH):
        mix = hybrid_mixer(x, positions, w_in[l], b_forget[l], mla_q_gain[l], mla_kv_gain[l],
                           mla_w_uq[l], mla_w_ukv[l], w_mix_out[l])
        x = layer_norm(DEEPNORM_ALPHA * x + mix, ln_mix_g[l], ln_mix_b[l])
        xa = memory_cross_attention(x, mem, xattn_w_q[l], xattn_w_kv[l], xattn_w_o[l])
        x = layer_norm(DEEPNORM_ALPHA * x + xa, ln_mem_g[l], ln_mem_b[l])
        gates = shared_group_router(x, router_w, router_bias)
        ff = moe_ffn(x, gates, expert_w_gate[l], expert_w_up[l], expert_w_down[l])
        x = layer_norm(DEEPNORM_ALPHA * x + ff, ln_ffn_g[l], ln_ffn_b[l])
    return x
```

```python
import functools
import math

import numpy as np
import jax
import jax.numpy as jnp
from jax import lax
from jax.experimental import pallas as pl
from jax.experimental.pallas import tpu as pltpu

F32 = jnp.float32
BF16 = jnp.bfloat16
HIGHEST = lax.Precision.HIGHEST

D_MODEL = 1024
DEPTH = 4
MLA_HEADS = 4
MLA_Q_RANK = 256
MLA_KV_RANK = 128
MLA_NOPE_DIM = 64
MLA_ROPE_DIM = 32
MLA_V_DIM = 64
FOX_HEADS = 6
FOX_HEAD_DIM = 64
DIL_HEADS = 6
DIL_HEAD_DIM = 64
DILATED_PAIRS = ((128, 1), (512, 4), (2048, 16))
XATTN_HEADS = 4
N_EXPERTS = 16
N_GROUPS = 4
EXPERTS_PER_GROUP = N_EXPERTS // N_GROUPS
D_EXPERT = 256
ROPE_THETA = 10000.0
NORM_EPS = 1e-5
NEG_INF = -1e30
DEEPNORM_ALPHA = (2 * DEPTH) ** 0.25

LANES = 128
HEAD_LANES = 64
VMEM_LIMIT_BYTES = 52 * 1024 * 1024

A_CQ = 0
A_CKV = A_CQ + MLA_Q_RANK
A_KR = A_CKV + MLA_KV_RANK
A_KRR = A_KR + LANES
A_FF = A_KRR + LANES
A_COLS = A_FF + LANES


def _cparams(sem):
    return pltpu.CompilerParams(dimension_semantics=sem, vmem_limit_bytes=VMEM_LIMIT_BYTES)


def _nt_dot(a, b, precision=None):
    return lax.dot_general(a, b, (((1,), (1,)), ((), ())), precision=precision,
                           preferred_element_type=F32)


def _dot(a, b, precision=None):
    return jnp.dot(a, b, precision=precision, preferred_element_type=F32)


def _layer_norm(y, g, b):
    mu = jnp.mean(y, axis=-1, keepdims=True)
    yc = y - mu
    var = jnp.mean(yc * yc, axis=-1, keepdims=True)
    return yc * lax.rsqrt(var + NORM_EPS) * g + b


def _rms_norm(y, g):
    ms = jnp.mean(y * y, axis=-1, keepdims=True)
    return y * lax.rsqrt(ms + NORM_EPS) * g


def _rope_table_kernel(pos_ref, invf_ref, sgn_ref, cos_ref, sin_ref):
    ang = pos_ref[0].astype(F32) * invf_ref[...]
    cos_ref[0] = jnp.cos(ang)
    sin_ref[0] = jnp.sin(ang) * sgn_ref[...]


def _rope_tables(positions, invf, sgn, ts):
    B, S = positions.shape
    pos3 = positions.reshape(B, S, 1)
    spec = pl.BlockSpec((1, ts, LANES), lambda b, i: (b, i, 0))
    vec = pl.BlockSpec((1, LANES), lambda b, i: (0, 0))
    return pl.pallas_call(
        _rope_table_kernel,
        out_shape=(jax.ShapeDtypeStruct((B, S, LANES), F32),) * 2,
        grid=(B, S // ts),
        in_specs=[pl.BlockSpec((1, ts, 1), lambda b, i: (b, i, 0)), vec, vec],
        out_specs=(spec, spec),
        compiler_params=_cparams(("parallel", "parallel")),
        name="rope_tables",
    )(pos3, invf, sgn)


def _proj_kernel(x_ref, wa_ref, wfox_ref, wdil_ref, qg_ref, kvg_ref, wuq_ref, wuk_ref, wuv_ref,
                 bf_ref, cosa_ref, sina_ref, cosd_ref, sind_ref,
                 qa_ref, ka_ref, va_ref, qb_ref, kb_ref, vb_ref, fcol_ref, frow_ref,
                 qc_ref, kc_ref, vc_ref, carry_c, carry_r, *, tm):
    i = pl.program_id(1)
    xb = x_ref[0].astype(BF16)

    ha = _dot(xb, wa_ref[...])
    cqn = _rms_norm(ha[:, A_CQ:A_CQ + MLA_Q_RANK], qg_ref[...]).astype(BF16)
    ckvn = _rms_norm(ha[:, A_CKV:A_CKV + MLA_KV_RANK], kvg_ref[...]).astype(BF16)
    cos_a = cosa_ref[0]
    sin_a = sina_ref[0]
    q2 = _dot(cqn, wuq_ref[...])
    k_nope = _dot(ckvn, wuk_ref[...])
    k_rot = ha[:, A_KR:A_KR + LANES] * cos_a + ha[:, A_KRR:A_KRR + LANES] * sin_a
    q_scale = (MLA_NOPE_DIM + MLA_ROPE_DIM) ** -0.5
    n_half = MLA_HEADS * LANES
    for h in range(MLA_HEADS):
        sl = slice(h * LANES, (h + 1) * LANES)
        qh = q2[:, sl] * cos_a + q2[:, n_half + h * LANES:n_half + (h + 1) * LANES] * sin_a
        qa_ref[0, :, sl] = (qh * q_scale).astype(BF16)
        ka_ref[0, :, sl] = (k_nope[:, sl] + k_rot).astype(BF16)
    va_ref[0] = _dot(ckvn, wuv_ref[...]).astype(BF16)

    hf = _dot(xb, wfox_ref[...])
    nf = FOX_HEADS * FOX_HEAD_DIM
    qb_ref[0] = (hf[:, :nf] * (FOX_HEAD_DIM ** -0.5)).astype(BF16)
    kb_ref[0] = hf[:, nf:2 * nf].astype(BF16)
    vb_ref[0] = hf[:, 2 * nf:].astype(BF16)

    z = ha[:, A_FF:A_FF + LANES] + bf_ref[...]
    logf = jnp.minimum(z, 0.0) - jnp.log(1.0 + jnp.exp(-jnp.abs(z)))
    lane = lax.broadcasted_iota(jnp.int32, (1, LANES), 1)
    logf = jnp.where(lane < FOX_HEADS, logf, 0.0)

    @pl.when(i == 0)
    def _():
        carry_c[...] = jnp.zeros_like(carry_c)
        carry_r[...] = jnp.zeros_like(carry_r)

    r = lax.broadcasted_iota(jnp.int32, (tm, tm), 0)
    c = lax.broadcasted_iota(jnp.int32, (tm, tm), 1)
    lower = (c <= r).astype(F32)
    upper = (r <= c).astype(F32)
    fcol = _dot(lower, logf, HIGHEST) + carry_c[...]
    sel = (lax.broadcasted_iota(jnp.int32, (8, LANES), 0)
           == lax.broadcasted_iota(jnp.int32, (8, LANES), 1)).astype(F32)
    logf_t = _nt_dot(sel, logf, HIGHEST)
    frow = _dot(logf_t, upper, HIGHEST) + carry_r[:, 0:1]
    fcol_ref[0] = fcol[:, 0:8]
    frow_ref[0] = frow
    carry_c[...] = fcol[tm - 1:tm, :]
    carry_r[...] = jnp.broadcast_to(frow[:, tm - 1:tm], carry_r.shape)

    hd = _dot(xb, wdil_ref[...])
    cos_d = cosd_ref[0]
    sin_d = sind_ref[0]
    nd = DIL_HEADS * DIL_HEAD_DIM
    for pblk in range(nd // LANES):
        sl = slice(pblk * LANES, (pblk + 1) * LANES)
        qh = hd[:, sl]
        kh = hd[:, nd + pblk * LANES:nd + (pblk + 1) * LANES]
        qh = qh * cos_d + pltpu.roll(qh, HEAD_LANES, 1) * sin_d
        kh = kh * cos_d + pltpu.roll(kh, HEAD_LANES, 1) * sin_d
        qc_ref[0, :, sl] = qh * (DIL_HEAD_DIM ** -0.5)
        kc_ref[0, :, sl] = kh
    vc_ref[0] = hd[:, 2 * nd:]


def _proj(x, wts, tabs, tm):
    B, S, D = x.shape
    cos_a, sin_a, cos_d, sin_d = tabs
    grid = (B, S // tm)

    def full(a):
        return pl.BlockSpec(a.shape, lambda b, i: (0,) * a.ndim)

    def tok(width, dtype):
        return (jax.ShapeDtypeStruct((B, S, width), dtype),
                pl.BlockSpec((1, tm, width), lambda b, i: (b, i, 0)))

    tab = pl.BlockSpec((1, tm, LANES), lambda b, i: (b, i, 0))
    outs = [tok(MLA_HEADS * LANES, BF16), tok(MLA_HEADS * LANES, BF16),
            tok(MLA_HEADS * MLA_V_DIM, BF16),
            tok(384, BF16), tok(384, BF16), tok(384, BF16),
            tok(8, F32),
            (jax.ShapeDtypeStruct((B, 8, S), F32), pl.BlockSpec((1, 8, tm), lambda b, i: (b, 0, i))),
            tok(384, F32), tok(384, F32), tok(384, F32)]
    w_list = [wts["wa"], wts["wfox"], wts["wdil"], wts["qg"], wts["kvg"], wts["wuq"], wts["wuk"],
              wts["wuv"], wts["bf"]]
    return pl.pallas_call(
        functools.partial(_proj_kernel, tm=tm),
        out_shape=tuple(o[0] for o in outs),
        grid=grid,
        in_specs=[pl.BlockSpec((1, tm, D), lambda b, i: (b, i, 0))] + [full(w) for w in w_list]
        + [tab, tab, tab, tab],
        out_specs=tuple(o[1] for o in outs),
        scratch_shapes=[pltpu.VMEM((1, LANES), F32), pltpu.VMEM((8, LANES), F32)],
        compiler_params=_cparams(("parallel", "arbitrary")),
        name="proj",
    )(x, *w_list, cos_a, sin_a, cos_d, sin_d)


def _flash_kernel(*refs, tq, fox):
    if fox:
        q_ref, k_ref, v_ref, fc_ref, fr_ref, o_ref, m_sc, l_sc, acc_sc = refs
    else:
        q0_ref, q1_ref, k0_ref, k1_ref, v_ref, o_ref, m_sc, l_sc, acc_sc = refs
    i = pl.program_id(2)
    lane = lax.broadcasted_iota(jnp.int32, (1, LANES), 1)
    lo = lane < HEAD_LANES

    if fox:
        q = q_ref[0]
        zero = jnp.zeros_like(q)
        qs = (jnp.where(lo, q, zero), jnp.where(lo, zero, q))
        k_refs = (k_ref, k_ref)
        fcs = (fc_ref[0, 0, :, 0:1], fc_ref[0, 0, :, 1:2])
    else:
        qs = (q0_ref[0], q1_ref[0])
        k_refs = (k0_ref, k1_ref)

    m_sc[...] = jnp.full_like(m_sc, NEG_INF)
    l_sc[...] = jnp.zeros_like(l_sc)
    acc_sc[...] = jnp.zeros_like(acc_sc)

    def step(j, masked):
        ks = pl.multiple_of(j * tq, tq)
        v = v_ref[0, pl.ds(ks, tq), :]
        alphas, pvs = [], []
        for h in range(2):
            k = k_refs[h][0, pl.ds(ks, tq), :]
            s = _nt_dot(qs[h], k)
            if fox:
                s = s + (fcs[h] - fr_ref[0, 0, h:h + 1, pl.ds(ks, tq)])
            if masked:
                row = lax.broadcasted_iota(jnp.int32, (tq, tq), 0)
                col = lax.broadcasted_iota(jnp.int32, (tq, tq), 1)
                s = jnp.where(col <= row, s, NEG_INF)
            m_prev = m_sc[h]
            m_new = jnp.maximum(m_prev, jnp.max(s, axis=1, keepdims=True))
            alpha = jnp.exp(m_prev - m_new)
            p = jnp.exp(s - m_new)
            l_sc[h] = alpha * l_sc[h] + jnp.sum(p, axis=1, keepdims=True)
            m_sc[h] = m_new
            alphas.append(alpha)
            pvs.append(_dot(p.astype(BF16), v))
        acc_sc[...] = (acc_sc[...] * jnp.where(lo, alphas[0], alphas[1])
                       + jnp.where(lo, pvs[0], pvs[1]))

    def body(j, carry):
        step(j, False)
        return carry

    lax.fori_loop(0, i, body, 0)
    step(i, True)
    o_ref[0] = (acc_sc[...] / jnp.where(lo, l_sc[0], l_sc[1])).astype(o_ref.dtype)


def _flash_scratch(tq):
    return [pltpu.VMEM((2, tq, 1), F32), pltpu.VMEM((2, tq, 1), F32), pltpu.VMEM((tq, LANES), F32)]


def _mla_attention(qa, ka, va, tq):
    B, S, _ = qa.shape
    n_pairs = MLA_HEADS // 2
    qspec0 = pl.BlockSpec((1, tq, LANES), lambda b, p, i: (b, i, 2 * p))
    qspec1 = pl.BlockSpec((1, tq, LANES), lambda b, p, i: (b, i, 2 * p + 1))
    kspec0 = pl.BlockSpec((1, S, LANES), lambda b, p, i: (b, 0, 2 * p))
    kspec1 = pl.BlockSpec((1, S, LANES), lambda b, p, i: (b, 0, 2 * p + 1))
    vspec = pl.BlockSpec((1, S, LANES), lambda b, p, i: (b, 0, p))
    return pl.pallas_call(
        functools.partial(_flash_kernel, tq=tq, fox=False),
        out_shape=jax.ShapeDtypeStruct((B, S, n_pairs * LANES), BF16),
        grid=(B, n_pairs, S // tq),
        in_specs=[qspec0, qspec1, kspec0, kspec1, vspec],
        out_specs=pl.BlockSpec((1, tq, LANES), lambda b, p, i: (b, i, p)),
        scratch_shapes=_flash_scratch(tq),
        compiler_params=_cparams(("parallel", "parallel", "arbitrary")),
        name="mla_flash",
    )(qa, qa, ka, ka, va)


def _fox_attention(qb, kb, vb, fcol, frow, tq):
    B, S, _ = qb.shape
    n_pairs = FOX_HEADS // 2
    fc = fcol[:, :, :FOX_HEADS].reshape(B, S, n_pairs, 2).transpose(0, 2, 1, 3)
    fr = frow[:, :FOX_HEADS].reshape(B, n_pairs, 2, S)
    qspec = pl.BlockSpec((1, tq, LANES), lambda b, p, i: (b, i, p))
    kspec = pl.BlockSpec((1, S, LANES), lambda b, p, i: (b, 0, p))
    return pl.pallas_call(
        functools.partial(_flash_kernel, tq=tq, fox=True),
        out_shape=jax.ShapeDtypeStruct((B, S, n_pairs * LANES), BF16),
        grid=(B, n_pairs, S // tq),
        in_specs=[qspec, kspec, kspec,
                  pl.BlockSpec((1, 1, tq, 2), lambda b, p, i: (b, p, i, 0)),
                  pl.BlockSpec((1, 1, 2, S), lambda b, p, i: (b, p, 0, 0))],
        out_specs=qspec,
        scratch_shapes=_flash_scratch(tq),
        compiler_params=_cparams(("parallel", "parallel", "arbitrary")),
        name="fox_flash",
    )(qb, kb, vb, fc, fr)


DIL_TQ = 128


def _dilated_kernel(q_ref, k_ref, v_ref, o_ref, m_sc, l_sc, acc_sc, *, seq):
    lane = lax.broadcasted_iota(jnp.int32, (1, LANES), 1)
    lo = lane < HEAD_LANES
    q_lo = (lane % HEAD_LANES) < (HEAD_LANES // 2)
    tq = DIL_TQ

    m_sc[...] = jnp.full_like(m_sc, NEG_INF)
    l_sc[...] = jnp.zeros_like(l_sc)
    acc_sc[...] = jnp.zeros_like(acc_sc)

    def block(dil, q_start, k_start, n_keys, first):
        rows = pl.ds(q_start, tq, stride=dil) if dil > 1 else pl.ds(q_start, tq)
        krows = pl.ds(k_start, n_keys, stride=dil) if dil > 1 else pl.ds(k_start, n_keys)
        q = q_ref[rows, :]
        k = k_ref[krows, :].astype(BF16)
        v = v_ref[krows, :].astype(BF16)
        zero = jnp.zeros_like(q)
        qs = (jnp.where(q_lo, q, zero).astype(BF16), jnp.where(q_lo, zero, q).astype(BF16))
        qi = lax.broadcasted_iota(jnp.int32, (tq, n_keys), 0)
        ki = lax.broadcasted_iota(jnp.int32, (tq, n_keys), 1)
        if first:
            valid = ki <= qi
        else:
            valid = (ki >= qi) & (ki <= qi + tq)
        m_old = m_sc[rows, :]
        ss = []
        for h in range(2):
            s = _nt_dot(qs[h], k)
            ss.append(jnp.where(valid, s, NEG_INF))
        m_blk = jnp.where(lo, jnp.max(ss[0], axis=1, keepdims=True),
                          jnp.max(ss[1], axis=1, keepdims=True))
        m_new = jnp.maximum(m_old, m_blk)
        alpha = jnp.exp(m_old - m_new)
        sums, pvs = [], []
        for h in range(2):
            p = jnp.exp(ss[h] - m_new[:, h * HEAD_LANES:h * HEAD_LANES + 1])
            sums.append(jnp.sum(p, axis=1, keepdims=True))
            pvs.append(_dot(p.astype(BF16), v))
        m_sc[rows, :] = m_new
        l_sc[rows, :] = alpha * l_sc[rows, :] + jnp.where(lo, sums[0], sums[1])
        acc_sc[rows, :] = alpha * acc_sc[rows, :] + jnp.where(lo, pvs[0], pvs[1])

    for window, dil in DILATED_PAIRS:
        assert window // dil == tq
        sub = seq // dil
        nblk = sub // tq

        def first_body(r, carry, dil=dil):
            block(dil, r, r, tq, True)
            return carry

        lax.fori_loop(0, dil, first_body, 0)

        if nblk > 1:
            def rest_body(t, carry, dil=dil, nblk=nblk):
                r = t // (nblk - 1)
                blk = t % (nblk - 1) + 1
                q_start = r + dil * blk * tq
                block(dil, q_start, q_start - dil * tq, 2 * tq, False)
                return carry

            lax.fori_loop(0, dil * (nblk - 1), rest_body, 0)

    o_ref[...] = (acc_sc[...] / l_sc[...]).astype(o_ref.dtype)


def _dilated_attention(qc, kc, vc):
    B, S, _ = qc.shape
    n_pairs = DIL_HEADS // 2
    spec = pl.BlockSpec((None, S, LANES), lambda b, p: (b, 0, p))
    return pl.pallas_call(
        functools.partial(_dilated_kernel, seq=S),
        out_shape=jax.ShapeDtypeStruct((B, S, n_pairs * LANES), BF16),
        grid=(B, n_pairs),
        in_specs=[spec, spec, spec],
        out_specs=spec,
        scratch_shapes=[pltpu.VMEM((S, LANES), F32)] * 3,
        compiler_params=_cparams(("parallel", "parallel")),
        name="dilated",
    )(qc, kc, vc)


def _mix_out_kernel(oa_ref, ob_ref, oc_ref, x_ref, wa_ref, wb_ref, wc_ref, g_ref, b_ref, o_ref):
    y = _dot(oa_ref[...], wa_ref[...]) + _dot(ob_ref[...], wb_ref[...]) + _dot(oc_ref[...], wc_ref[...])
    o_ref[...] = _layer_norm(DEEPNORM_ALPHA * x_ref[...] + y, g_ref[...], b_ref[...])


def _mix_out(oa, ob, oc, x2d, w_out, g, b, tm):
    T, D = x2d.shape
    na, nb = oa.shape[1], ob.shape[1]
    wa, wb, wc = w_out[:na], w_out[na:na + nb], w_out[na + nb:]

    def rows(a):
        return pl.BlockSpec((tm, a.shape[1]), lambda i: (i, 0))

    def full(a):
        return pl.BlockSpec(a.shape, lambda i: (0, 0))

    return pl.pallas_call(
        _mix_out_kernel,
        out_shape=jax.ShapeDtypeStruct((T, D), F32),
        grid=(T // tm,),
        in_specs=[rows(oa), rows(ob), rows(oc), rows(x2d), full(wa), full(wb), full(wc), full(g), full(b)],
        out_specs=pl.BlockSpec((tm, D), lambda i: (i, 0)),
        compiler_params=_cparams(("parallel",)),
        name="mix_out",
    )(oa, ob, oc, x2d, wa, wb, wc, g, b)


def _mem_kv_kernel(mem_ref, w_ref, k_ref, v_ref):
    kv = _dot(mem_ref[0].astype(BF16), w_ref[...])
    d = k_ref.shape[-1]
    k_ref[0] = kv[:, :d].astype(BF16)
    v_ref[0] = kv[:, d:].astype(BF16)


def _mem_kv(mem, w_kv):
    B, M, D = mem.shape
    spec = pl.BlockSpec((1, M, D), lambda b: (b, 0, 0))
    return pl.pallas_call(
        _mem_kv_kernel,
        out_shape=(jax.ShapeDtypeStruct((B, M, D), BF16),) * 2,
        grid=(B,),
        in_specs=[spec, pl.BlockSpec(w_kv.shape, lambda b: (0, 0))],
        out_specs=(spec, spec),
        compiler_params=_cparams(("parallel",)),
        name="mem_kv",
    )(mem, w_kv)


def _xattn_kernel(x_ref, k_ref, v_ref, wq_ref, wo_ref, g_ref, b_ref, o_ref):
    x = x_ref[0]
    d = x.shape[-1]
    dh = d // XATTN_HEADS
    q = (_dot(x.astype(BF16), wq_ref[...]) * (dh ** -0.5)).astype(BF16)
    outs = []
    for h in range(XATTN_HEADS):
        sl = slice(h * dh, (h + 1) * dh)
        s = _nt_dot(q[:, sl], k_ref[0, :, sl])
        p = jnp.exp(s - jnp.max(s, axis=1, keepdims=True))
        o = _dot(p.astype(BF16), v_ref[0, :, sl])
        outs.append((o / jnp.sum(p, axis=1, keepdims=True)).astype(BF16))
    o_all = jnp.concatenate(outs, axis=1)
    y = _dot(o_all, wo_ref[...])
    o_ref[0] = _layer_norm(DEEPNORM_ALPHA * x + y, g_ref[...], b_ref[...])


def _xattn(x, k_mem, v_mem, w_q, w_o, g, b, tm):
    B, S, D = x.shape
    M = k_mem.shape[1]
    xspec = pl.BlockSpec((1, tm, D), lambda bb, i: (bb, i, 0))
    mspec = pl.BlockSpec((1, M, D), lambda bb, i: (bb, 0, 0))

    def full(a):
        return pl.BlockSpec(a.shape, lambda bb, i: (0, 0))

    return pl.pallas_call(
        _xattn_kernel,
        out_shape=jax.ShapeDtypeStruct((B, S, D), F32),
        grid=(B, S // tm),
        in_specs=[xspec, mspec, mspec, full(w_q), full(w_o), full(g), full(b)],
        out_specs=xspec,
        compiler_params=_cparams(("parallel", "parallel")),
        name="xattn",
    )(x, k_mem, v_mem, w_q, w_o, g, b)


def _router_gates_t(x, rw_t, rbias):
    logits_t = _nt_dot(rw_t, x, HIGHEST)
    scores = 1.0 / (1.0 + jnp.exp(-logits_t))
    biased = scores + rbias
    sc = [scores[e:e + 1, :] for e in range(N_EXPERTS)]
    bs = [biased[e:e + 1, :] for e in range(N_EXPERTS)]
    epg = EXPERTS_PER_GROUP

    def beats(a, b, a_first):
        return (a >= b) if a_first else (a > b)

    grp = []
    for gi in range(N_GROUPS):
        v = bs[gi * epg:(gi + 1) * epg]
        best = None
        for a in range(epg):
            for b in range(a + 1, epg):
                pair = v[a] + v[b]
                best = pair if best is None else jnp.maximum(best, pair)
        grp.append(best)
    gates = []
    for gi in range(N_GROUPS):
        g_sel = None
        for gj in range(N_GROUPS):
            if gj == gi:
                continue
            w = beats(grp[gi], grp[gj], gi < gj)
            g_sel = w if g_sel is None else (g_sel & w)
        in_top = []
        for a in range(epg):
            ea = gi * epg + a
            n_above = jnp.zeros(bs[ea].shape, jnp.int32)
            for b in range(epg):
                if b == a:
                    continue
                eb = gi * epg + b
                n_above = n_above + beats(bs[eb], bs[ea], b < a).astype(jnp.int32)
            in_top.append(g_sel & (n_above < 2))
        denom = None
        for a in range(epg):
            term = jnp.where(in_top[a], sc[gi * epg + a], 0.0)
            denom = term if denom is None else denom + term
        denom = jnp.where(g_sel, denom, 1.0)
        for a in range(epg):
            gates.append(jnp.where(in_top[a], sc[gi * epg + a] / denom, 0.0))
    return jnp.concatenate(gates, axis=0)


def _moe_kernel(x_ref, rwt_ref, rb_ref, wg_ref, wu_ref, wd_ref, g_ref, b_ref, o_ref,
                gates_sc, acc_sc, xb_sc, *, tm):
    e = pl.program_id(1)

    @pl.when(e == 0)
    def _():
        x = x_ref[...]
        gates_t = _router_gates_t(x, rwt_ref[...], rb_ref[...])
        eye = (lax.broadcasted_iota(jnp.int32, (tm, tm), 0)
               == lax.broadcasted_iota(jnp.int32, (tm, tm), 1)).astype(F32)
        gates_sc[...] = _nt_dot(eye, gates_t, HIGHEST)
        acc_sc[...] = jnp.zeros_like(acc_sc)
        xb_sc[...] = x.astype(BF16)

    xb = xb_sc[...]
    hg = _dot(xb, wg_ref[0])
    hu = _dot(xb, wu_ref[0])
    lane_e = lax.broadcasted_iota(jnp.int32, (1, N_EXPERTS), 1)
    gate = jnp.sum(jnp.where(lane_e == e, gates_sc[...], 0.0), axis=1, keepdims=True)
    a = (hg / (1.0 + jnp.exp(-hg))) * hu * gate
    acc_sc[...] += _dot(a.astype(BF16), wd_ref[0])

    @pl.when(e == N_EXPERTS - 1)
    def _():
        o_ref[...] = _layer_norm(DEEPNORM_ALPHA * x_ref[...] + acc_sc[...], g_ref[...], b_ref[...])


def _moe(x2d, rw_t, rbias, wg, wu, wd, g, b, tm):
    T, D = x2d.shape
    xspec = pl.BlockSpec((tm, D), lambda i, e: (i, 0))

    def full(a):
        return pl.BlockSpec(a.shape, lambda i, e: (0, 0))

    return pl.pallas_call(
        functools.partial(_moe_kernel, tm=tm),
        out_shape=jax.ShapeDtypeStruct((T, D), F32),
        grid=(T // tm, N_EXPERTS),
        in_specs=[xspec, full(rw_t), full(rbias),
                  pl.BlockSpec((1, D, D_EXPERT), lambda i, e: (e, 0, 0)),
                  pl.BlockSpec((1, D, D_EXPERT), lambda i, e: (e, 0, 0)),
                  pl.BlockSpec((1, D_EXPERT, D), lambda i, e: (e, 0, 0)),
                  full(g), full(b)],
        out_specs=xspec,
        scratch_shapes=[pltpu.VMEM((tm, N_EXPERTS), F32), pltpu.VMEM((tm, D), F32),
                        pltpu.VMEM((tm, D), BF16)],
        compiler_params=_cparams(("parallel", "arbitrary")),
        name="moe",
    )(x2d, rw_t, rbias, wg, wu, wd, g, b)


def _rope_constants():
    half_d = DIL_HEAD_DIM // 2
    inv_d = ROPE_THETA ** (-jnp.arange(half_d, dtype=F32) / half_d)
    half_a = MLA_ROPE_DIM // 2
    inv_a = ROPE_THETA ** (-jnp.arange(half_a, dtype=F32) / half_a)
    lane = np.arange(LANES)
    invf_d = inv_d[lane % half_d].reshape(1, LANES)
    sgn_d = jnp.asarray(np.where(lane < HEAD_LANES, -1.0, 1.0).reshape(1, LANES), F32)
    in_rope = (lane >= MLA_NOPE_DIM) & (lane < MLA_NOPE_DIM + MLA_ROPE_DIM)
    invf_a = jnp.where(jnp.asarray(in_rope), inv_a[(lane - MLA_NOPE_DIM) % half_a], 0.0).reshape(1, LANES)
    sgn_a = jnp.ones((1, LANES), F32)
    return invf_a, sgn_a, invf_d, sgn_d


def _dil_pair_perm():
    half = DIL_HEAD_DIM // 2
    cols = []
    for p in range(DIL_HEADS // 2):
        h0, h1 = 2 * p, 2 * p + 1
        for h, off in ((h0, 0), (h1, 0), (h0, half), (h1, half)):
            cols.extend(range(h * DIL_HEAD_DIM + off, h * DIL_HEAD_DIM + off + half))
    return np.asarray(cols)


def _layer_weights(w_in, b_forget, q_gain, kv_gain, w_uq, w_ukv):
    D = w_in.shape[0]
    cuts = np.cumsum([MLA_Q_RANK, MLA_KV_RANK, MLA_ROPE_DIM, 3 * FOX_HEADS * FOX_HEAD_DIM, FOX_HEADS])
    w_cq, w_ckv, w_kr, w_fox, w_ff, w_dil = jnp.split(w_in, cuts.tolist(), axis=1)
    half = MLA_ROPE_DIM // 2
    w_kr_rot = jnp.concatenate([-w_kr[:, half:], w_kr[:, :half]], axis=1)

    def place_rope(w):
        return jnp.pad(w, ((0, 0), (MLA_NOPE_DIM, LANES - MLA_NOPE_DIM - MLA_ROPE_DIM)))

    wa = jnp.concatenate([w_cq, w_ckv, place_rope(w_kr), place_rope(w_kr_rot),
                          jnp.pad(w_ff, ((0, 0), (0, LANES - FOX_HEADS)))], axis=1)

    nd = DIL_HEADS * DIL_HEAD_DIM
    perm = _dil_pair_perm()
    wdil = jnp.concatenate([w_dil[:, :nd][:, perm], w_dil[:, nd:2 * nd][:, perm], w_dil[:, 2 * nd:]], axis=1)

    dq = MLA_NOPE_DIM + MLA_ROPE_DIM
    uq = w_uq.reshape(MLA_Q_RANK, MLA_HEADS, dq)
    uq_rope = uq[:, :, MLA_NOPE_DIM:]
    uq_rot = jnp.concatenate([-uq_rope[:, :, half:], uq_rope[:, :, :half]], axis=2)
    pad_tail = LANES - dq
    uq_plain = jnp.pad(uq, ((0, 0), (0, 0), (0, pad_tail))).reshape(MLA_Q_RANK, MLA_HEADS * LANES)
    uq_rotp = jnp.pad(uq_rot, ((0, 0), (0, 0), (MLA_NOPE_DIM, pad_tail))).reshape(MLA_Q_RANK, MLA_HEADS * LANES)
    wuq = jnp.concatenate([uq_plain, uq_rotp], axis=1)

    ukv = w_ukv.reshape(MLA_KV_RANK, MLA_HEADS, MLA_NOPE_DIM + MLA_V_DIM)
    wuk = jnp.pad(ukv[:, :, :MLA_NOPE_DIM], ((0, 0), (0, 0), (0, LANES - MLA_NOPE_DIM)))
    wuk = wuk.reshape(MLA_KV_RANK, MLA_HEADS * LANES)
    wuv = ukv[:, :, MLA_NOPE_DIM:].reshape(MLA_KV_RANK, MLA_HEADS * MLA_V_DIM)

    bf = jnp.pad(b_forget.astype(F32), (0, LANES - FOX_HEADS)).reshape(1, LANES)
    return dict(wa=wa.astype(BF16), wfox=w_fox.astype(BF16), wdil=wdil.astype(BF16),
                qg=q_gain.reshape(1, -1).astype(F32), kvg=kv_gain.reshape(1, -1).astype(F32),
                wuq=wuq.astype(BF16), wuk=wuk.astype(BF16), wuv=wuv.astype(BF16), bf=bf)


def _tiles(S):
    tm = min(512, S)
    return dict(rope=min(512, S), proj=tm, flash=min(512, S), mix=tm, xattn=tm, moe=tm)


def kernel(x, mem, positions, w_in, b_forget, mla_q_gain, mla_kv_gain, mla_w_uq, mla_w_ukv, w_mix_out, ln_mix_g, ln_mix_b, xattn_w_q, xattn_w_kv, xattn_w_o, ln_mem_g, ln_mem_b, router_w, router_bias, expert_w_gate, expert_w_up, expert_w_down, ln_ffn_g, ln_ffn_b):
    B, S, D = x.shape
    depth = w_in.shape[0]
    t = _tiles(S)
    invf_a, sgn_a, invf_d, sgn_d = _rope_constants()
    cos_a, sin_a = _rope_tables(positions, invf_a, sgn_a, t["rope"])
    cos_d, sin_d = _rope_tables(positions, invf_d, sgn_d, t["rope"])
    tabs = (cos_a, sin_a, cos_d, sin_d)
    rw_t = router_w.T.astype(F32)
    rbias = router_bias.reshape(-1, 1).astype(F32)

    def row(v):
        return v.reshape(1, -1).astype(F32)

    for l in range(depth):
        wts = _layer_weights(w_in[l], b_forget[l], mla_q_gain[l], mla_kv_gain[l], mla_w_uq[l], mla_w_ukv[l])
        qa, ka, va, qb, kb, vb, fcol, frow, qc, kc, vc = _proj(x, wts, tabs, t["proj"])
        oa = _mla_attention(qa, ka, va, t["flash"])
        ob = _fox_attention(qb, kb, vb, fcol, frow, t["flash"])
        oc = _dilated_attention(qc, kc, vc)
        x2d = _mix_out(oa.reshape(B * S, -1), ob.reshape(B * S, -1), oc.reshape(B * S, -1),
                       x.reshape(B * S, D), w_mix_out[l].astype(BF16), row(ln_mix_g[l]), row(ln_mix_b[l]),
                       t["mix"])
        x = x2d.reshape(B, S, D)
        k_mem, v_mem = _mem_kv(mem, xattn_w_kv[l].astype(BF16))
        x = _xattn(x, k_mem, v_mem, xattn_w_q[l].astype(BF16), xattn_w_o[l].astype(BF16),
                   row(ln_mem_g[l]), row(ln_mem_b[l]), t["xattn"])
        x2d = _moe(x.reshape(B * S, D), rw_t, rbias, expert_w_gate[l].astype(BF16),
                   expert_w_up[l].astype(BF16), expert_w_down[l].astype(BF16),
                   row(ln_ffn_g[l]), row(ln_ffn_b[l]), t["moe"])
        x = x2d.reshape(B, S, D)
    return x
```

```python
import functools
import math

import numpy as np
import jax
import jax.numpy as jnp
from jax import lax
from jax.experimental import pallas as pl
from jax.experimental.pallas import tpu as pltpu

F32 = jnp.float32
BF16 = jnp.bfloat16
HIGHEST = lax.Precision.HIGHEST

D_MODEL = 1024
DEPTH = 4
MLA_HEADS = 4
MLA_Q_RANK = 256
MLA_KV_RANK = 128
MLA_NOPE_DIM = 64
MLA_ROPE_DIM = 32
MLA_V_DIM = 64
FOX_HEADS = 6
FOX_HEAD_DIM = 64
DIL_HEADS = 6
DIL_HEAD_DIM = 64
DILATED_PAIRS = ((128, 1), (512, 4), (2048, 16))
XATTN_HEADS = 4
N_EXPERTS = 16
N_GROUPS = 4
EXPERTS_PER_GROUP = N_EXPERTS // N_GROUPS
D_EXPERT = 256
ROPE_THETA = 10000.0
NORM_EPS = 1e-5
NEG_INF = -1e30
DEEPNORM_ALPHA = (2 * DEPTH) ** 0.25
LOG2E = math.log2(math.e)

LANES = 128
HEAD_LANES = 64
VMEM_LIMIT_BYTES = 52 * 1024 * 1024

A_CQ = 0
A_CKV = A_CQ + MLA_Q_RANK
A_KR = A_CKV + MLA_KV_RANK
A_KRR = A_KR + LANES
A_FF = A_KRR + LANES
A_COLS = A_FF + LANES


def _cparams(sem):
    return pltpu.CompilerParams(dimension_semantics=sem, vmem_limit_bytes=VMEM_LIMIT_BYTES)


def _nt_dot(a, b, precision=None):
    return lax.dot_general(a, b, (((1,), (1,)), ((), ())), precision=precision,
                           preferred_element_type=F32)


def _dot(a, b, precision=None):
    return jnp.dot(a, b, precision=precision, preferred_element_type=F32)


def _layer_norm(y, g, b):
    mu = jnp.mean(y, axis=-1, keepdims=True)
    yc = y - mu
    var = jnp.mean(yc * yc, axis=-1, keepdims=True)
    return yc * lax.rsqrt(var + NORM_EPS) * g + b


def _rms_norm(y, g):
    ms = jnp.mean(y * y, axis=-1, keepdims=True)
    return y * lax.rsqrt(ms + NORM_EPS) * g


def _rope_table_kernel(pos_ref, invf_ref, sgn_ref, cos_ref, sin_ref):
    ang = pos_ref[0].astype(F32) * invf_ref[...]
    cos_ref[0] = jnp.cos(ang)
    sin_ref[0] = jnp.sin(ang) * sgn_ref[...]


def _rope_tables(positions, invf, sgn, ts):
    B, S = positions.shape
    pos3 = positions.reshape(B, S, 1)
    spec = pl.BlockSpec((1, ts, LANES), lambda b, i: (b, i, 0))
    vec = pl.BlockSpec((1, LANES), lambda b, i: (0, 0))
    return pl.pallas_call(
        _rope_table_kernel,
        out_shape=(jax.ShapeDtypeStruct((B, S, LANES), F32),) * 2,
        grid=(B, S // ts),
        in_specs=[pl.BlockSpec((1, ts, 1), lambda b, i: (b, i, 0)), vec, vec],
        out_specs=(spec, spec),
        compiler_params=_cparams(("parallel", "parallel")),
        name="rope_tables",
    )(pos3, invf, sgn)


def _proj_kernel(x_ref, wa_ref, wfox_ref, wfoxv_ref, wdil_ref, qg_ref, kvg_ref, wuq_ref, wuk_ref,
                 wuv_ref, bf_ref, sel_ref, cosa_ref, sina_ref, cosd_ref, sind_ref,
                 qa_ref, ka_ref, va_ref, qb_ref, kb_ref, vb_ref,
                 qc_ref, kc_ref, vc_ref, carry_c, *, tm):
    i = pl.program_id(1)
    xb = x_ref[0].astype(BF16)

    ha = _dot(xb, wa_ref[...])
    cqn = _rms_norm(ha[:, A_CQ:A_CQ + MLA_Q_RANK], qg_ref[...]).astype(BF16)
    ckvn = _rms_norm(ha[:, A_CKV:A_CKV + MLA_KV_RANK], kvg_ref[...]).astype(BF16)
    cos_a = cosa_ref[0]
    sin_a = sina_ref[0]
    q2 = _dot(cqn, wuq_ref[...])
    k_nope = _dot(ckvn, wuk_ref[...])
    k_rot = ha[:, A_KR:A_KR + LANES] * cos_a + ha[:, A_KRR:A_KRR + LANES] * sin_a
    q_scale = (MLA_NOPE_DIM + MLA_ROPE_DIM) ** -0.5 * LOG2E
    n_half = MLA_HEADS * LANES
    for h in range(MLA_HEADS):
        sl = slice(h * LANES, (h + 1) * LANES)
        qh = q2[:, sl] * cos_a + q2[:, n_half + h * LANES:n_half + (h + 1) * LANES] * sin_a
        qa_ref[0, :, sl] = (qh * q_scale).astype(BF16)
        ka_ref[0, :, sl] = (k_nope[:, sl] + k_rot).astype(BF16)
    va_ref[0] = _nt_dot(wuv_ref[...], ckvn).astype(BF16)

    z = ha[:, A_FF:A_FF + LANES] + bf_ref[...]
    logf = jnp.minimum(z, 0.0) - jnp.log(1.0 + jnp.exp(-jnp.abs(z)))
    lane = lax.broadcasted_iota(jnp.int32, (1, LANES), 1)
    logf = jnp.where(lane < FOX_HEADS, logf, 0.0)

    @pl.when(i == 0)
    def _():
        carry_c[...] = jnp.zeros_like(carry_c)

    r = lax.broadcasted_iota(jnp.int32, (tm, tm), 0)
    c = lax.broadcasted_iota(jnp.int32, (tm, tm), 1)
    lower = (c <= r).astype(F32)
    fcol = _dot(lower, logf, HIGHEST) + carry_c[...]
    carry_c[...] = fcol[tm - 1:tm, :]

    nb = fcol * (-LOG2E)
    hi = nb.astype(BF16)
    r1 = nb - hi.astype(F32)
    mid = r1.astype(BF16)
    lo3 = (r1 - mid.astype(F32)).astype(BF16)
    placed = _dot(jnp.concatenate([hi, mid, lo3], axis=1), sel_ref[...])

    hf = _dot(xb, wfox_ref[...])
    nf = FOX_HEADS * FOX_HEAD_DIM
    dims = lane < FOX_HEAD_DIM
    ones3 = jnp.where((lane >= FOX_HEAD_DIM) & (lane < FOX_HEAD_DIM + 3), 1.0, 0.0)
    fq_scale = FOX_HEAD_DIM ** -0.5 * LOG2E
    for h in range(FOX_HEADS):
        blk = slice((h // 2) * LANES, (h // 2 + 1) * LANES)
        out = slice(h * LANES, (h + 1) * LANES)
        qh = hf[:, blk]
        kh = hf[:, nf + (h // 2) * LANES:nf + (h // 2 + 1) * LANES]
        if h % 2:
            qh = pltpu.roll(qh, HEAD_LANES, 1)
            kh = pltpu.roll(kh, HEAD_LANES, 1)
        qb_ref[0, :, out] = jnp.where(dims, qh * fq_scale, ones3).astype(BF16)
        kb_ref[0, :, out] = jnp.where(dims, kh, placed[:, out]).astype(BF16)
    vb_ref[0] = _nt_dot(wfoxv_ref[...], xb).astype(BF16)

    hd = _dot(xb, wdil_ref[...])
    cos_d = cosd_ref[0]
    sin_d = sind_ref[0]
    nd = DIL_HEADS * DIL_HEAD_DIM
    for pblk in range(nd // LANES):
        sl = slice(pblk * LANES, (pblk + 1) * LANES)
        qh = hd[:, sl]
        kh = hd[:, nd + pblk * LANES:nd + (pblk + 1) * LANES]
        qh = qh * cos_d + pltpu.roll(qh, HEAD_LANES, 1) * sin_d
        kh = kh * cos_d + pltpu.roll(kh, HEAD_LANES, 1) * sin_d
        qc_ref[0, :, sl] = qh * (DIL_HEAD_DIM ** -0.5)
        kc_ref[0, :, sl] = kh
    vc_ref[0] = hd[:, 2 * nd:]


def _proj(x, wts, tabs, tm):
    B, S, D = x.shape
    cos_a, sin_a, cos_d, sin_d = tabs
    grid = (B, S // tm)

    def full(a):
        return pl.BlockSpec(a.shape, lambda b, i: (0,) * a.ndim)

    def tok(width, dtype):
        return (jax.ShapeDtypeStruct((B, S, width), dtype),
                pl.BlockSpec((1, tm, width), lambda b, i: (b, i, 0)))

    tab = pl.BlockSpec((1, tm, LANES), lambda b, i: (b, i, 0))
    def tok_t(height, dtype):
        return (jax.ShapeDtypeStruct((B, height, S), dtype),
                pl.BlockSpec((1, height, tm), lambda b, i: (b, 0, i)))

    outs = [tok(MLA_HEADS * LANES, BF16), tok(MLA_HEADS * LANES, BF16),
            tok_t(MLA_HEADS * MLA_V_DIM, BF16),
            tok(FOX_HEADS * LANES, BF16), tok(FOX_HEADS * LANES, BF16),
            tok_t(FOX_HEADS * FOX_HEAD_DIM, BF16),
            tok(384, F32), tok(384, F32), tok(384, F32)]
    w_list = [wts["wa"], wts["wfox"], wts["wfoxv_t"], wts["wdil"], wts["qg"], wts["kvg"], wts["wuq"],
              wts["wuk"], wts["wuv_t"], wts["bf"], wts["sel"]]
    return pl.pallas_call(
        functools.partial(_proj_kernel, tm=tm),
        out_shape=tuple(o[0] for o in outs),
        grid=grid,
        in_specs=[pl.BlockSpec((1, tm, D), lambda b, i: (b, i, 0))] + [full(w) for w in w_list]
        + [tab, tab, tab, tab],
        out_specs=tuple(o[1] for o in outs),
        scratch_shapes=[pltpu.VMEM((1, LANES), F32)],
        compiler_params=_cparams(("parallel", "arbitrary")),
        name="proj",
    )(x, *w_list, cos_a, sin_a, cos_d, sin_d)


FLASH_CHUNK = 32


def _flash_kernel(q0_ref, q1_ref, k0_ref, k1_ref, vt_ref, o_ref,
                  s0_sc, s1_sc, p0_sc, p1_sc, m_sc, l_sc, acc0_sc, acc1_sc, *, tq):
    i = pl.program_id(2)
    q_refs = (q0_ref, q1_ref)
    k_refs = (k0_ref, k1_ref)
    s_scs = (s0_sc, s1_sc)
    p_scs = (p0_sc, p1_sc)
    acc_scs = (acc0_sc, acc1_sc)
    dv = HEAD_LANES
    n_chunks = tq // FLASH_CHUNK

    m_sc[...] = jnp.full_like(m_sc, NEG_INF)
    l_sc[...] = jnp.zeros_like(l_sc)
    acc0_sc[...] = jnp.zeros_like(acc0_sc)
    acc1_sc[...] = jnp.zeros_like(acc1_sc)

    def step(j, masked):
        ks = pl.multiple_of(j * tq, tq)
        m_news, alphas = [], []
        for h in range(2):
            k = k_refs[h][0, pl.ds(ks, tq), :]
            st = _nt_dot(k, q_refs[h][0])
            if masked:
                key = lax.broadcasted_iota(jnp.int32, (tq, tq), 0)
                qry = lax.broadcasted_iota(jnp.int32, (tq, tq), 1)
                st = jnp.where(key <= qry, st, NEG_INF)
            s_scs[h][...] = st
            m8 = jnp.max(st.reshape(tq // 8, 8, tq), axis=0)
            m_prev = m_sc[h]
            m_new = jnp.maximum(m_prev, jnp.max(m8, axis=0, keepdims=True))
            m_news.append(m_new)
            alphas.append(jnp.exp2(m_prev - m_new))
            m_sc[h] = m_new
        for h in range(2):
            l8 = jnp.zeros((8, tq), F32)
            for c in range(n_chunks):
                rows = slice(c * FLASH_CHUNK, (c + 1) * FLASH_CHUNK)
                p = jnp.exp2(s_scs[h][rows, :] - m_news[h])
                l8 = l8 + jnp.sum(p.reshape(FLASH_CHUNK // 8, 8, tq), axis=0)
                p_scs[h][rows, :] = p.astype(BF16)
            vt = vt_ref[0, h * dv:(h + 1) * dv, pl.ds(ks, tq)]
            pv = _dot(vt, p_scs[h][...])
            acc_scs[h][...] = acc_scs[h][...] * alphas[h] + pv
            l_sc[h] = l_sc[h] * alphas[h] + jnp.sum(l8, axis=0, keepdims=True)

    def body(j, carry):
        step(j, False)
        return carry

    lax.fori_loop(0, i, body, 0)
    step(i, True)
    for h in range(2):
        o_ref[0, h * dv:(h + 1) * dv, :] = (acc_scs[h][...] / l_sc[h]).astype(o_ref.dtype)


def _flash_attention(q, k, vt, n_heads, tq, name):
    B, S, _ = q.shape
    n_pairs = n_heads // 2
    dv2 = 2 * HEAD_LANES
    qspec0 = pl.BlockSpec((1, tq, LANES), lambda b, p, i: (b, i, 2 * p))
    qspec1 = pl.BlockSpec((1, tq, LANES), lambda b, p, i: (b, i, 2 * p + 1))
    kspec0 = pl.BlockSpec((1, S, LANES), lambda b, p, i: (b, 0, 2 * p))
    kspec1 = pl.BlockSpec((1, S, LANES), lambda b, p, i: (b, 0, 2 * p + 1))
    vspec = pl.BlockSpec((1, dv2, S), lambda b, p, i: (b, p, 0))
    return pl.pallas_call(
        functools.partial(_flash_kernel, tq=tq),
        out_shape=jax.ShapeDtypeStruct((B, n_heads * HEAD_LANES, S), BF16),
        grid=(B, n_pairs, S // tq),
        in_specs=[qspec0, qspec1, kspec0, kspec1, vspec],
        out_specs=pl.BlockSpec((1, dv2, tq), lambda b, p, i: (b, p, i)),
        scratch_shapes=[pltpu.VMEM((tq, tq), F32), pltpu.VMEM((tq, tq), F32),
                        pltpu.VMEM((tq, tq), BF16), pltpu.VMEM((tq, tq), BF16),
                        pltpu.VMEM((2, 1, tq), F32), pltpu.VMEM((2, 1, tq), F32),
                        pltpu.VMEM((HEAD_LANES, tq), F32), pltpu.VMEM((HEAD_LANES, tq), F32)],
        compiler_params=_cparams(("parallel", "parallel", "arbitrary")),
        name=name,
    )(q, q, k, k, vt)


DIL_TQ = 128


def _dilated_kernel(q_ref, k_ref, v_ref, o_ref, m_sc, l_sc, acc_sc, *, seq):
    lane = lax.broadcasted_iota(jnp.int32, (1, LANES), 1)
    lo = lane < HEAD_LANES
    q_lo = (lane % HEAD_LANES) < (HEAD_LANES // 2)
    tq = DIL_TQ

    m_sc[...] = jnp.full_like(m_sc, NEG_INF)
    l_sc[...] = jnp.zeros_like(l_sc)
    acc_sc[...] = jnp.zeros_like(acc_sc)

    def block(dil, q_start, k_start, n_keys, first):
        rows = pl.ds(q_start, tq, stride=dil) if dil > 1 else pl.ds(q_start, tq)
        krows = pl.ds(k_start, n_keys, stride=dil) if dil > 1 else pl.ds(k_start, n_keys)
        q = q_ref[rows, :]
        k = k_ref[krows, :].astype(BF16)
        v = v_ref[krows, :].astype(BF16)
        zero = jnp.zeros_like(q)
        qs = (jnp.where(q_lo, q, zero).astype(BF16), jnp.where(q_lo, zero, q).astype(BF16))
        qi = lax.broadcasted_iota(jnp.int32, (tq, n_keys), 0)
        ki = lax.broadcasted_iota(jnp.int32, (tq, n_keys), 1)
        if first:
            valid = ki <= qi
        else:
            valid = (ki >= qi) & (ki <= qi + tq)
        m_old = m_sc[rows, :]
        ss = []
        for h in range(2):
            s = _nt_dot(qs[h], k)
            ss.append(jnp.where(valid, s, NEG_INF))
        m_blk = jnp.where(lo, jnp.max(ss[0], axis=1, keepdims=True),
                          jnp.max(ss[1], axis=1, keepdims=True))
        m_new = jnp.maximum(m_old, m_blk)
        alpha = jnp.exp(m_old - m_new)
        sums, pvs = [], []
        for h in range(2):
            p = jnp.exp(ss[h] - m_new[:, h * HEAD_LANES:h * HEAD_LANES + 1])
            sums.append(jnp.sum(p, axis=1, keepdims=True))
            pvs.append(_dot(p.astype(BF16), v))
        m_sc[rows, :] = m_new
        l_sc[rows, :] = alpha * l_sc[rows, :] + jnp.where(lo, sums[0], sums[1])
        acc_sc[rows, :] = alpha * acc_sc[rows, :] + jnp.where(lo, pvs[0], pvs[1])

    for window, dil in DILATED_PAIRS:
        assert window // dil == tq
        sub = seq // dil
        nblk = sub // tq

        def first_body(r, carry, dil=dil):
            block(dil, r, r, tq, True)
            return carry

        lax.fori_loop(0, dil, first_body, 0)

        if nblk > 1:
            def rest_body(t, carry, dil=dil, nblk=nblk):
                r = t // (nblk - 1)
                blk = t % (nblk - 1) + 1
                q_start = r + dil * blk * tq
                block(dil, q_start, q_start - dil * tq, 2 * tq, False)
                return carry

            lax.fori_loop(0, dil * (nblk - 1), rest_body, 0)

    o_ref[...] = (acc_sc[...] / l_sc[...]).astype(o_ref.dtype)


def _dilated_attention(qc, kc, vc):
    B, S, _ = qc.shape
    n_pairs = DIL_HEADS // 2
    spec = pl.BlockSpec((None, S, LANES), lambda b, p: (b, 0, p))
    return pl.pallas_call(
        functools.partial(_dilated_kernel, seq=S),
        out_shape=jax.ShapeDtypeStruct((B, S, n_pairs * LANES), BF16),
        grid=(B, n_pairs),
        in_specs=[spec, spec, spec],
        out_specs=spec,
        scratch_shapes=[pltpu.VMEM((S, LANES), F32)] * 3,
        compiler_params=_cparams(("parallel", "parallel")),
        name="dilated",
    )(qc, kc, vc)


def _tn_dot(a_t, b):
    return lax.dot_general(a_t, b, (((0,), (0,)), ((), ())), preferred_element_type=F32)


def _mix_out_kernel(oat_ref, obt_ref, oc_ref, x_ref, wa_ref, wb_ref, wc_ref, g_ref, b_ref, o_ref):
    y = (_tn_dot(oat_ref[0], wa_ref[...]) + _tn_dot(obt_ref[0], wb_ref[...])
         + _dot(oc_ref[0], wc_ref[...]))
    o_ref[0] = _layer_norm(DEEPNORM_ALPHA * x_ref[0] + y, g_ref[...], b_ref[...])


def _mix_out(oa_t, ob_t, oc, x, w_out, g, b, tm):
    B, S, D = x.shape
    na, nb = oa_t.shape[1], ob_t.shape[1]
    wa, wb, wc = w_out[:na], w_out[na:na + nb], w_out[na + nb:]

    def cols(a):
        return pl.BlockSpec((1, a.shape[1], tm), lambda bb, i: (bb, 0, i))

    def rows(a):
        return pl.BlockSpec((1, tm, a.shape[2]), lambda bb, i: (bb, i, 0))

    def full(a):
        return pl.BlockSpec(a.shape, lambda bb, i: (0, 0))

    return pl.pallas_call(
        _mix_out_kernel,
        out_shape=jax.ShapeDtypeStruct((B, S, D), F32),
        grid=(B, S // tm),
        in_specs=[cols(oa_t), cols(ob_t), rows(oc), rows(x), full(wa), full(wb), full(wc), full(g), full(b)],
        out_specs=rows(x),
        compiler_params=_cparams(("parallel", "parallel")),
        name="mix_out",
    )(oa_t, ob_t, oc, x, wa, wb, wc, g, b)


def _mem_kv_kernel(mem_ref, w_ref, k_ref, v_ref):
    kv = _dot(mem_ref[0].astype(BF16), w_ref[...])
    d = k_ref.shape[-1]
    k_ref[0] = kv[:, :d].astype(BF16)
    v_ref[0] = kv[:, d:].astype(BF16)


def _mem_kv(mem, w_kv):
    B, M, D = mem.shape
    spec = pl.BlockSpec((1, M, D), lambda b: (b, 0, 0))
    return pl.pallas_call(
        _mem_kv_kernel,
        out_shape=(jax.ShapeDtypeStruct((B, M, D), BF16),) * 2,
        grid=(B,),
        in_specs=[spec, pl.BlockSpec(w_kv.shape, lambda b: (0, 0))],
        out_specs=(spec, spec),
        compiler_params=_cparams(("parallel",)),
        name="mem_kv",
    )(mem, w_kv)


def _xattn_kernel(x_ref, k_ref, v_ref, wq_ref, wo_ref, g_ref, b_ref, o_ref):
    x = x_ref[0]
    d = x.shape[-1]
    dh = d // XATTN_HEADS
    q = (_dot(x.astype(BF16), wq_ref[...]) * (dh ** -0.5)).astype(BF16)
    outs = []
    for h in range(XATTN_HEADS):
        sl = slice(h * dh, (h + 1) * dh)
        s = _nt_dot(q[:, sl], k_ref[0, :, sl])
        p = jnp.exp(s - jnp.max(s, axis=1, keepdims=True))
        o = _dot(p.astype(BF16), v_ref[0, :, sl])
        outs.append((o / jnp.sum(p, axis=1, keepdims=True)).astype(BF16))
    o_all = jnp.concatenate(outs, axis=1)
    y = _dot(o_all, wo_ref[...])
    o_ref[0] = _layer_norm(DEEPNORM_ALPHA * x + y, g_ref[...], b_ref[...])


def _xattn(x, k_mem, v_mem, w_q, w_o, g, b, tm):
    B, S, D = x.shape
    M = k_mem.shape[1]
    xspec = pl.BlockSpec((1, tm, D), lambda bb, i: (bb, i, 0))
    mspec = pl.BlockSpec((1, M, D), lambda bb, i: (bb, 0, 0))

    def full(a):
        return pl.BlockSpec(a.shape, lambda bb, i: (0, 0))

    return pl.pallas_call(
        _xattn_kernel,
        out_shape=jax.ShapeDtypeStruct((B, S, D), F32),
        grid=(B, S // tm),
        in_specs=[xspec, mspec, mspec, full(w_q), full(w_o), full(g), full(b)],
        out_specs=xspec,
        compiler_params=_cparams(("parallel", "parallel")),
        name="xattn",
    )(x, k_mem, v_mem, w_q, w_o, g, b)


def _router_gates_t(x, rw_t, rbias):
    logits_t = _nt_dot(rw_t, x, HIGHEST)
    scores = 1.0 / (1.0 + jnp.exp(-logits_t))
    biased = scores + rbias
    sc = [scores[e:e + 1, :] for e in range(N_EXPERTS)]
    bs = [biased[e:e + 1, :] for e in range(N_EXPERTS)]
    epg = EXPERTS_PER_GROUP

    def beats(a, b, a_first):
        return (a >= b) if a_first else (a > b)

    grp = []
    for gi in range(N_GROUPS):
        v = bs[gi * epg:(gi + 1) * epg]
        best = None
        for a in range(epg):
            for b in range(a + 1, epg):
                pair = v[a] + v[b]
                best = pair if best is None else jnp.maximum(best, pair)
        grp.append(best)
    gates = []
    for gi in range(N_GROUPS):
        g_sel = None
        for gj in range(N_GROUPS):
            if gj == gi:
                continue
            w = beats(grp[gi], grp[gj], gi < gj)
            g_sel = w if g_sel is None else (g_sel & w)
        in_top = []
        for a in range(epg):
            ea = gi * epg + a
            n_above = jnp.zeros(bs[ea].shape, jnp.int32)
            for b in range(epg):
                if b == a:
                    continue
                eb = gi * epg + b
                n_above = n_above + beats(bs[eb], bs[ea], b < a).astype(jnp.int32)
            in_top.append(g_sel & (n_above < 2))
        denom = None
        for a in range(epg):
            term = jnp.where(in_top[a], sc[gi * epg + a], 0.0)
            denom = term if denom is None else denom + term
        denom = jnp.where(g_sel, denom, 1.0)
        for a in range(epg):
            gates.append(jnp.where(in_top[a], sc[gi * epg + a] / denom, 0.0))
    return jnp.concatenate(gates, axis=0)


def _moe_kernel(x_ref, rwt_ref, rb_ref, wg_ref, wu_ref, wd_ref, g_ref, b_ref, o_ref,
                gates_sc, acc_sc, xb_sc, *, tm):
    e = pl.program_id(1)

    @pl.when(e == 0)
    def _():
        x = x_ref[...]
        gates_t = _router_gates_t(x, rwt_ref[...], rb_ref[...])
        eye = (lax.broadcasted_iota(jnp.int32, (tm, tm), 0)
               == lax.broadcasted_iota(jnp.int32, (tm, tm), 1)).astype(F32)
        gates_sc[...] = _nt_dot(eye, gates_t, HIGHEST)
        acc_sc[...] = jnp.zeros_like(acc_sc)
        xb_sc[...] = x.astype(BF16)

    xb = xb_sc[...]
    hg = _dot(xb, wg_ref[0])
    hu = _dot(xb, wu_ref[0])
    lane_e = lax.broadcasted_iota(jnp.int32, (1, N_EXPERTS), 1)
    gate = jnp.sum(jnp.where(lane_e == e, gates_sc[...], 0.0), axis=1, keepdims=True)
    a = (hg / (1.0 + jnp.exp(-hg))) * hu * gate
    acc_sc[...] += _dot(a.astype(BF16), wd_ref[0])

    @pl.when(e == N_EXPERTS - 1)
    def _():
        o_ref[...] = _layer_norm(DEEPNORM_ALPHA * x_ref[...] + acc_sc[...], g_ref[...], b_ref[...])


def _moe(x2d, rw_t, rbias, wg, wu, wd, g, b, tm):
    T, D = x2d.shape
    xspec = pl.BlockSpec((tm, D), lambda i, e: (i, 0))

    def full(a):
        return pl.BlockSpec(a.shape, lambda i, e: (0, 0))

    return pl.pallas_call(
        functools.partial(_moe_kernel, tm=tm),
        out_shape=jax.ShapeDtypeStruct((T, D), F32),
        grid=(T // tm, N_EXPERTS),
        in_specs=[xspec, full(rw_t), full(rbias),
                  pl.BlockSpec((1, D, D_EXPERT), lambda i, e: (e, 0, 0)),
                  pl.BlockSpec((1, D, D_EXPERT), lambda i, e: (e, 0, 0)),
                  pl.BlockSpec((1, D_EXPERT, D), lambda i, e: (e, 0, 0)),
                  full(g), full(b)],
        out_specs=xspec,
        scratch_shapes=[pltpu.VMEM((tm, N_EXPERTS), F32), pltpu.VMEM((tm, D), F32),
                        pltpu.VMEM((tm, D), BF16)],
        compiler_params=_cparams(("parallel", "arbitrary")),
        name="moe",
    )(x2d, rw_t, rbias, wg, wu, wd, g, b)


def _rope_constants():
    half_d = DIL_HEAD_DIM // 2
    inv_d = ROPE_THETA ** (-jnp.arange(half_d, dtype=F32) / half_d)
    half_a = MLA_ROPE_DIM // 2
    inv_a = ROPE_THETA ** (-jnp.arange(half_a, dtype=F32) / half_a)
    lane = np.arange(LANES)
    invf_d = inv_d[lane % half_d].reshape(1, LANES)
    sgn_d = jnp.asarray(np.where(lane < HEAD_LANES, -1.0, 1.0).reshape(1, LANES), F32)
    in_rope = (lane >= MLA_NOPE_DIM) & (lane < MLA_NOPE_DIM + MLA_ROPE_DIM)
    invf_a = jnp.where(jnp.asarray(in_rope), inv_a[(lane - MLA_NOPE_DIM) % half_a], 0.0).reshape(1, LANES)
    sgn_a = jnp.ones((1, LANES), F32)
    return invf_a, sgn_a, invf_d, sgn_d


def _dil_pair_perm():
    half = DIL_HEAD_DIM // 2
    cols = []
    for p in range(DIL_HEADS // 2):
        h0, h1 = 2 * p, 2 * p + 1
        for h, off in ((h0, 0), (h1, 0), (h0, half), (h1, half)):
            cols.extend(range(h * DIL_HEAD_DIM + off, h * DIL_HEAD_DIM + off + half))
    return np.asarray(cols)


def _layer_weights(w_in, b_forget, q_gain, kv_gain, w_uq, w_ukv):
    D = w_in.shape[0]
    cuts = np.cumsum([MLA_Q_RANK, MLA_KV_RANK, MLA_ROPE_DIM, 3 * FOX_HEADS * FOX_HEAD_DIM, FOX_HEADS])
    w_cq, w_ckv, w_kr, w_fox, w_ff, w_dil = jnp.split(w_in, cuts.tolist(), axis=1)
    half = MLA_ROPE_DIM // 2
    w_kr_rot = jnp.concatenate([-w_kr[:, half:], w_kr[:, :half]], axis=1)

    def place_rope(w):
        return jnp.pad(w, ((0, 0), (MLA_NOPE_DIM, LANES - MLA_NOPE_DIM - MLA_ROPE_DIM)))

    wa = jnp.concatenate([w_cq, w_ckv, place_rope(w_kr), place_rope(w_kr_rot),
                          jnp.pad(w_ff, ((0, 0), (0, LANES - FOX_HEADS)))], axis=1)

    nd = DIL_HEADS * DIL_HEAD_DIM
    perm = _dil_pair_perm()
    wdil = jnp.concatenate([w_dil[:, :nd][:, perm], w_dil[:, nd:2 * nd][:, perm], w_dil[:, 2 * nd:]], axis=1)

    dq = MLA_NOPE_DIM + MLA_ROPE_DIM
    uq = w_uq.reshape(MLA_Q_RANK, MLA_HEADS, dq)
    uq_rope = uq[:, :, MLA_NOPE_DIM:]
    uq_rot = jnp.concatenate([-uq_rope[:, :, half:], uq_rope[:, :, :half]], axis=2)
    pad_tail = LANES - dq
    uq_plain = jnp.pad(uq, ((0, 0), (0, 0), (0, pad_tail))).reshape(MLA_Q_RANK, MLA_HEADS * LANES)
    uq_rotp = jnp.pad(uq_rot, ((0, 0), (0, 0), (MLA_NOPE_DIM, pad_tail))).reshape(MLA_Q_RANK, MLA_HEADS * LANES)
    wuq = jnp.concatenate([uq_plain, uq_rotp], axis=1)

    ukv = w_ukv.reshape(MLA_KV_RANK, MLA_HEADS, MLA_NOPE_DIM + MLA_V_DIM)
    wuk = jnp.pad(ukv[:, :, :MLA_NOPE_DIM], ((0, 0), (0, 0), (0, LANES - MLA_NOPE_DIM)))
    wuk = wuk.reshape(MLA_KV_RANK, MLA_HEADS * LANES)
    wuv = ukv[:, :, MLA_NOPE_DIM:].reshape(MLA_KV_RANK, MLA_HEADS * MLA_V_DIM)

    bf = jnp.pad(b_forget.astype(F32), (0, LANES - FOX_HEADS)).reshape(1, LANES)
    nf = FOX_HEADS * FOX_HEAD_DIM
    return dict(wa=wa.astype(BF16), wfox=w_fox[:, :2 * nf].astype(BF16),
                wfoxv_t=w_fox[:, 2 * nf:].T.astype(BF16), wdil=wdil.astype(BF16),
                qg=q_gain.reshape(1, -1).astype(F32), kvg=kv_gain.reshape(1, -1).astype(F32),
                wuq=wuq.astype(BF16), wuk=wuk.astype(BF16), wuv_t=wuv.T.astype(BF16), bf=bf,
                sel=_fox_bias_placement())


def _fox_bias_placement():
    sel = np.zeros((3 * LANES, FOX_HEADS * LANES), np.float32)
    for j in range(3):
        for h in range(FOX_HEADS):
            sel[j * LANES + h, h * LANES + FOX_HEAD_DIM + j] = 1.0
    return jnp.asarray(sel, BF16)


def _tiles(S):
    tm = min(512, S)
    return dict(rope=min(512, S), proj=tm, flash=min(512, S), mix=tm, xattn=tm, moe=tm)


def kernel(x, mem, positions, w_in, b_forget, mla_q_gain, mla_kv_gain, mla_w_uq, mla_w_ukv, w_mix_out, ln_mix_g, ln_mix_b, xattn_w_q, xattn_w_kv, xattn_w_o, ln_mem_g, ln_mem_b, router_w, router_bias, expert_w_gate, expert_w_up, expert_w_down, ln_ffn_g, ln_ffn_b):
    B, S, D = x.shape
    depth = w_in.shape[0]
    t = _tiles(S)
    invf_a, sgn_a, invf_d, sgn_d = _rope_constants()
    cos_a, sin_a = _rope_tables(positions, invf_a, sgn_a, t["rope"])
    cos_d, sin_d = _rope_tables(positions, invf_d, sgn_d, t["rope"])
    tabs = (cos_a, sin_a, cos_d, sin_d)
    rw_t = router_w.T.astype(F32)
    rbias = router_bias.reshape(-1, 1).astype(F32)

    def row(v):
        return v.reshape(1, -1).astype(F32)

    for l in range(depth):
        wts = _layer_weights(w_in[l], b_forget[l], mla_q_gain[l], mla_kv_gain[l], mla_w_uq[l], mla_w_ukv[l])
        qa, ka, va_t, qb, kb, vb_t, qc, kc, vc = _proj(x, wts, tabs, t["proj"])
        oa_t = _flash_attention(qa, ka, va_t, MLA_HEADS, t["flash"], "mla_flash")
        ob_t = _flash_attention(qb, kb, vb_t, FOX_HEADS, t["flash"], "fox_flash")
        oc = _dilated_attention(qc, kc, vc)
        x = _mix_out(oa_t, ob_t, oc, x, w_mix_out[l].astype(BF16), row(ln_mix_g[l]), row(ln_mix_b[l]),
                     t["mix"])
        k_mem, v_mem = _mem_kv(mem, xattn_w_kv[l].astype(BF16))
        x = _xattn(x, k_mem, v_mem, xattn_w_q[l].astype(BF16), xattn_w_o[l].astype(BF16),
                   row(ln_mem_g[l]), row(ln_mem_b[l]), t["xattn"])
        x2d = _moe(x.reshape(B * S, D), rw_t, rbias, expert_w_gate[l].astype(BF16),
                   expert_w_up[l].astype(BF16), expert_w_down[l].astype(BF16),
                   row(ln_ffn_g[l]), row(ln_ffn_b[l]), t["moe"])
        x = x2d.reshape(B, S, D)
    return x
```

```python
import functools
import math

import numpy as np
import jax
import jax.numpy as jnp
from jax import lax
from jax.experimental import pallas as pl
from jax.experimental.pallas import tpu as pltpu

F32 = jnp.float32
BF16 = jnp.bfloat16
HIGHEST = lax.Precision.HIGHEST

D_MODEL = 1024
DEPTH = 4
MLA_HEADS = 4
MLA_Q_RANK = 256
MLA_KV_RANK = 128
MLA_NOPE_DIM = 64
MLA_ROPE_DIM = 32
MLA_V_DIM = 64
FOX_HEADS = 6
FOX_HEAD_DIM = 64
DIL_HEADS = 6
DIL_HEAD_DIM = 64
DILATED_PAIRS = ((128, 1), (512, 4), (2048, 16))
XATTN_HEADS = 4
N_EXPERTS = 16
N_GROUPS = 4
EXPERTS_PER_GROUP = N_EXPERTS // N_GROUPS
D_EXPERT = 256
ROPE_THETA = 10000.0
NORM_EPS = 1e-5
NEG_INF = -1e30
DEEPNORM_ALPHA = (2 * DEPTH) ** 0.25
LOG2E = math.log2(math.e)

LANES = 128
HEAD_LANES = 64
VMEM_LIMIT_BYTES = 52 * 1024 * 1024

A_CQ = 0
A_CKV = A_CQ + MLA_Q_RANK
A_KR = A_CKV + MLA_KV_RANK
A_KRR = A_KR + LANES
A_FF = A_KRR + LANES
A_COLS = A_FF + LANES


def _cparams(sem):
    return pltpu.CompilerParams(dimension_semantics=sem, vmem_limit_bytes=VMEM_LIMIT_BYTES)


def _nt_dot(a, b, precision=None):
    return lax.dot_general(a, b, (((1,), (1,)), ((), ())), precision=precision,
                           preferred_element_type=F32)


def _dot(a, b, precision=None):
    return jnp.dot(a, b, precision=precision, preferred_element_type=F32)


def _layer_norm(y, g, b):
    mu = jnp.mean(y, axis=-1, keepdims=True)
    yc = y - mu
    var = jnp.mean(yc * yc, axis=-1, keepdims=True)
    return yc * lax.rsqrt(var + NORM_EPS) * g + b


def _rms_norm(y, g):
    ms = jnp.mean(y * y, axis=-1, keepdims=True)
    return y * lax.rsqrt(ms + NORM_EPS) * g


def _rope_table_kernel(pos_ref, invf_ref, sgn_ref, cos_ref, sin_ref):
    ang = pos_ref[0].astype(F32) * invf_ref[...]
    cos_ref[0] = jnp.cos(ang)
    sin_ref[0] = jnp.sin(ang) * sgn_ref[...]


def _rope_tables(positions, invf, sgn, ts):
    B, S = positions.shape
    pos3 = positions.reshape(B, S, 1)
    spec = pl.BlockSpec((1, ts, LANES), lambda b, i: (b, i, 0))
    vec = pl.BlockSpec((1, LANES), lambda b, i: (0, 0))
    return pl.pallas_call(
        _rope_table_kernel,
        out_shape=(jax.ShapeDtypeStruct((B, S, LANES), F32),) * 2,
        grid=(B, S // ts),
        in_specs=[pl.BlockSpec((1, ts, 1), lambda b, i: (b, i, 0)), vec, vec],
        out_specs=(spec, spec),
        compiler_params=_cparams(("parallel", "parallel")),
        name="rope_tables",
    )(pos3, invf, sgn)


def _proj_kernel(x_ref, wa_ref, wfox_ref, wfoxv_ref, wdil_ref, qg_ref, kvg_ref, wuq_ref, wuk_ref,
                 wuv_ref, bf_ref, sel_ref, cosa_ref, sina_ref, cosd_ref, sind_ref,
                 qa_ref, ka_ref, va_ref, qb_ref, kb_ref, vb_ref,
                 qc_ref, kc_ref, vc_ref, carry_c, *, tm):
    i = pl.program_id(1)
    xb = x_ref[0].astype(BF16)

    ha = _dot(xb, wa_ref[...])
    cqn = _rms_norm(ha[:, A_CQ:A_CQ + MLA_Q_RANK], qg_ref[...]).astype(BF16)
    ckvn = _rms_norm(ha[:, A_CKV:A_CKV + MLA_KV_RANK], kvg_ref[...]).astype(BF16)
    cos_a = cosa_ref[0]
    sin_a = sina_ref[0]
    q2 = _dot(cqn, wuq_ref[...])
    k_nope = _dot(ckvn, wuk_ref[...])
    k_rot = ha[:, A_KR:A_KR + LANES] * cos_a + ha[:, A_KRR:A_KRR + LANES] * sin_a
    q_scale = (MLA_NOPE_DIM + MLA_ROPE_DIM) ** -0.5 * LOG2E
    n_half = MLA_HEADS * LANES
    for h in range(MLA_HEADS):
        sl = slice(h * LANES, (h + 1) * LANES)
        qh = q2[:, sl] * cos_a + q2[:, n_half + h * LANES:n_half + (h + 1) * LANES] * sin_a
        qa_ref[0, :, sl] = (qh * q_scale).astype(BF16)
        ka_ref[0, :, sl] = (k_nope[:, sl] + k_rot).astype(BF16)
    va_ref[0] = _nt_dot(wuv_ref[...], ckvn).astype(BF16)

    z = ha[:, A_FF:A_FF + LANES] + bf_ref[...]
    logf = jnp.minimum(z, 0.0) - jnp.log(1.0 + jnp.exp(-jnp.abs(z)))
    lane = lax.broadcasted_iota(jnp.int32, (1, LANES), 1)
    logf = jnp.where(lane < FOX_HEADS, logf, 0.0)

    @pl.when(i == 0)
    def _():
        carry_c[...] = jnp.zeros_like(carry_c)

    r = lax.broadcasted_iota(jnp.int32, (tm, tm), 0)
    c = lax.broadcasted_iota(jnp.int32, (tm, tm), 1)
    lower = (c <= r).astype(F32)
    fcol = _dot(lower, logf, HIGHEST) + carry_c[...]
    carry_c[...] = fcol[tm - 1:tm, :]

    nb = fcol * (-LOG2E)
    hi = nb.astype(BF16)
    r1 = nb - hi.astype(F32)
    mid = r1.astype(BF16)
    lo3 = (r1 - mid.astype(F32)).astype(BF16)
    placed = _dot(jnp.concatenate([hi, mid, lo3], axis=1), sel_ref[...])

    hf = _dot(xb, wfox_ref[...])
    nf = FOX_HEADS * FOX_HEAD_DIM
    dims = lane < FOX_HEAD_DIM
    ones3 = jnp.where((lane >= FOX_HEAD_DIM) & (lane < FOX_HEAD_DIM + 3), 1.0, 0.0)
    fq_scale = FOX_HEAD_DIM ** -0.5 * LOG2E
    for h in range(FOX_HEADS):
        blk = slice((h // 2) * LANES, (h // 2 + 1) * LANES)
        out = slice(h * LANES, (h + 1) * LANES)
        qh = hf[:, blk]
        kh = hf[:, nf + (h // 2) * LANES:nf + (h // 2 + 1) * LANES]
        if h % 2:
            qh = pltpu.roll(qh, HEAD_LANES, 1)
            kh = pltpu.roll(kh, HEAD_LANES, 1)
        qb_ref[0, :, out] = jnp.where(dims, qh * fq_scale, ones3).astype(BF16)
        kb_ref[0, :, out] = jnp.where(dims, kh, placed[:, out]).astype(BF16)
    vb_ref[0] = _nt_dot(wfoxv_ref[...], xb).astype(BF16)

    hd = _dot(xb, wdil_ref[...])
    cos_d = cosd_ref[0]
    sin_d = sind_ref[0]
    nd = DIL_HEADS * DIL_HEAD_DIM
    for pblk in range(nd // LANES):
        sl = slice(pblk * LANES, (pblk + 1) * LANES)
        qh = hd[:, sl]
        kh = hd[:, nd + pblk * LANES:nd + (pblk + 1) * LANES]
        qh = qh * cos_d + pltpu.roll(qh, HEAD_LANES, 1) * sin_d
        kh = kh * cos_d + pltpu.roll(kh, HEAD_LANES, 1) * sin_d
        qc_ref[0, :, sl] = (qh * (DIL_HEAD_DIM ** -0.5 * LOG2E)).astype(BF16)
        kc_ref[0, :, sl] = kh.astype(BF16)
    vc_ref[0] = hd[:, 2 * nd:].astype(BF16)


def _proj(x, wts, tabs, tm):
    B, S, D = x.shape
    cos_a, sin_a, cos_d, sin_d = tabs
    grid = (B, S // tm)

    def full(a):
        return pl.BlockSpec(a.shape, lambda b, i: (0,) * a.ndim)

    def tok(width, dtype):
        return (jax.ShapeDtypeStruct((B, S, width), dtype),
                pl.BlockSpec((1, tm, width), lambda b, i: (b, i, 0)))

    tab = pl.BlockSpec((1, tm, LANES), lambda b, i: (b, i, 0))
    def tok_t(height, dtype):
        return (jax.ShapeDtypeStruct((B, height, S), dtype),
                pl.BlockSpec((1, height, tm), lambda b, i: (b, 0, i)))

    outs = [tok(MLA_HEADS * LANES, BF16), tok(MLA_HEADS * LANES, BF16),
            tok_t(MLA_HEADS * MLA_V_DIM, BF16),
            tok(FOX_HEADS * LANES, BF16), tok(FOX_HEADS * LANES, BF16),
            tok_t(FOX_HEADS * FOX_HEAD_DIM, BF16),
            tok(384, BF16), tok(384, BF16), tok(384, BF16)]
    w_list = [wts["wa"], wts["wfox"], wts["wfoxv_t"], wts["wdil"], wts["qg"], wts["kvg"], wts["wuq"],
              wts["wuk"], wts["wuv_t"], wts["bf"], wts["sel"]]
    return pl.pallas_call(
        functools.partial(_proj_kernel, tm=tm),
        out_shape=tuple(o[0] for o in outs),
        grid=grid,
        in_specs=[pl.BlockSpec((1, tm, D), lambda b, i: (b, i, 0))] + [full(w) for w in w_list]
        + [tab, tab, tab, tab],
        out_specs=tuple(o[1] for o in outs),
        scratch_shapes=[pltpu.VMEM((1, LANES), F32)],
        compiler_params=_cparams(("parallel", "arbitrary")),
        name="proj",
    )(x, *w_list, cos_a, sin_a, cos_d, sin_d)


FLASH_CHUNK = 32


def _flash_kernel(q0_ref, q1_ref, k0_ref, k1_ref, vt_ref, o_ref,
                  s0_sc, s1_sc, p0_sc, p1_sc, m_sc, l_sc, acc0_sc, acc1_sc, *, tq):
    i = pl.program_id(2)
    q_refs = (q0_ref, q1_ref)
    k_refs = (k0_ref, k1_ref)
    s_scs = (s0_sc, s1_sc)
    p_scs = (p0_sc, p1_sc)
    acc_scs = (acc0_sc, acc1_sc)
    dv = HEAD_LANES
    n_chunks = tq // FLASH_CHUNK

    m_sc[...] = jnp.full_like(m_sc, NEG_INF)
    l_sc[...] = jnp.zeros_like(l_sc)
    acc0_sc[...] = jnp.zeros_like(acc0_sc)
    acc1_sc[...] = jnp.zeros_like(acc1_sc)

    def step(j, masked):
        ks = pl.multiple_of(j * tq, tq)
        m_news, alphas = [], []
        for h in range(2):
            k = k_refs[h][0, pl.ds(ks, tq), :]
            st = _nt_dot(k, q_refs[h][0])
            if masked:
                key = lax.broadcasted_iota(jnp.int32, (tq, tq), 0)
                qry = lax.broadcasted_iota(jnp.int32, (tq, tq), 1)
                st = jnp.where(key <= qry, st, NEG_INF)
            s_scs[h][...] = st
            m8 = jnp.max(st.reshape(tq // 8, 8, tq), axis=0)
            m_prev = m_sc[h]
            m_new = jnp.maximum(m_prev, jnp.max(m8, axis=0, keepdims=True))
            m_news.append(m_new)
            alphas.append(jnp.exp2(m_prev - m_new))
            m_sc[h] = m_new
        for h in range(2):
            l8 = jnp.zeros((8, tq), F32)
            for c in range(n_chunks):
                rows = slice(c * FLASH_CHUNK, (c + 1) * FLASH_CHUNK)
                p = jnp.exp2(s_scs[h][rows, :] - m_news[h])
                l8 = l8 + jnp.sum(p.reshape(FLASH_CHUNK // 8, 8, tq), axis=0)
                p_scs[h][rows, :] = p.astype(BF16)
            vt = vt_ref[0, h * dv:(h + 1) * dv, pl.ds(ks, tq)]
            pv = _dot(vt, p_scs[h][...])
            acc_scs[h][...] = acc_scs[h][...] * alphas[h] + pv
            l_sc[h] = l_sc[h] * alphas[h] + jnp.sum(l8, axis=0, keepdims=True)

    def body(j, carry):
        step(j, False)
        return carry

    lax.fori_loop(0, i, body, 0)
    step(i, True)
    for h in range(2):
        o_ref[0, h * dv:(h + 1) * dv, :] = (acc_scs[h][...] / l_sc[h]).astype(o_ref.dtype)


def _flash_attention(q, k, vt, n_heads, tq, name):
    B, S, _ = q.shape
    n_pairs = n_heads // 2
    dv2 = 2 * HEAD_LANES
    qspec0 = pl.BlockSpec((1, tq, LANES), lambda b, p, i: (b, i, 2 * p))
    qspec1 = pl.BlockSpec((1, tq, LANES), lambda b, p, i: (b, i, 2 * p + 1))
    kspec0 = pl.BlockSpec((1, S, LANES), lambda b, p, i: (b, 0, 2 * p))
    kspec1 = pl.BlockSpec((1, S, LANES), lambda b, p, i: (b, 0, 2 * p + 1))
    vspec = pl.BlockSpec((1, dv2, S), lambda b, p, i: (b, p, 0))
    return pl.pallas_call(
        functools.partial(_flash_kernel, tq=tq),
        out_shape=jax.ShapeDtypeStruct((B, n_heads * HEAD_LANES, S), BF16),
        grid=(B, n_pairs, S // tq),
        in_specs=[qspec0, qspec1, kspec0, kspec1, vspec],
        out_specs=pl.BlockSpec((1, dv2, tq), lambda b, p, i: (b, p, i)),
        scratch_shapes=[pltpu.VMEM((tq, tq), F32), pltpu.VMEM((tq, tq), F32),
                        pltpu.VMEM((tq, tq), BF16), pltpu.VMEM((tq, tq), BF16),
                        pltpu.VMEM((2, 1, tq), F32), pltpu.VMEM((2, 1, tq), F32),
                        pltpu.VMEM((HEAD_LANES, tq), F32), pltpu.VMEM((HEAD_LANES, tq), F32)],
        compiler_params=_cparams(("parallel", "parallel", "arbitrary")),
        name=name,
    )(q, q, k, k, vt)


DIL_PLANES = 16
DIL_TQ = 128
DIL_UNROLL = 4


def _plane_slot(r16):
    return 4 * (r16 % 4) + r16 // 4


def _dilated_geometry(dil):
    planes = DIL_PLANES // dil
    rows = DIL_TQ // planes
    if dil == 1:
        order = [(c // 4) + 4 * (c % 4) for c in range(planes)]
    else:
        order = list(range(planes))
    return planes, rows, order


def _dilated_bias(dil, first):
    planes, rows, order = _dilated_geometry(dil)
    krows = 2 * rows
    a = np.arange(planes * rows)
    b = np.arange(planes * krows)
    uq = np.asarray(order)[a // rows] + planes * (a % rows + (0 if first else rows))
    uk = np.asarray(order)[b // krows] + planes * (b % krows)
    diff = uq[:, None] - uk[None, :]
    valid = (diff >= 0) & (diff <= DIL_TQ)
    return jnp.asarray(np.where(valid, 0.0, NEG_INF), F32)


def _dilated_kernel(q_ref, k_ref, v_ref, b1f_ref, b1r_ref, b4f_ref, b4r_ref, b16f_ref, b16r_ref, o_ref,
                    q_st, k_st, v_st, m_sc, l_sc, acc_sc, *, rows_per_plane):
    pair = pl.program_id(1)
    n_pairs = DIL_HEADS // 2
    lane = lax.broadcasted_iota(jnp.int32, (1, LANES), 1)
    lo = lane < HEAD_LANES
    q_lo = (lane % HEAD_LANES) < (HEAD_LANES // 2)
    tq = DIL_TQ

    def plane_lanes(r16):
        return pl.ds(pl.multiple_of((r16 * n_pairs + pair) * LANES, LANES), LANES)

    for r16 in range(DIL_PLANES):
        slot = _plane_slot(r16)
        q_st[slot] = q_ref[0, :, plane_lanes(r16)].astype(F32)
        k_st[slot] = k_ref[0, :, plane_lanes(r16)].astype(F32)
        v_st[slot] = v_ref[0, :, plane_lanes(r16)].astype(F32)
    m_sc[...] = jnp.full_like(m_sc, NEG_INF)
    l_sc[...] = jnp.zeros_like(l_sc)
    acc_sc[...] = jnp.zeros_like(acc_sc)

    def block(planes, rows, slot0, row0, first, bias_ref):
        qsl = (pl.ds(slot0, planes), pl.ds(row0, rows), slice(None))
        k_row0 = row0 if first else row0 - rows
        ksl = (pl.ds(slot0, planes), pl.ds(k_row0, 2 * rows), slice(None))
        q = q_st[qsl].reshape(tq, LANES)
        k = k_st[ksl].reshape(-1, LANES).astype(BF16)
        v = v_st[ksl].reshape(-1, LANES).astype(BF16)
        zero = jnp.zeros_like(q)
        qs = (jnp.where(q_lo, q, zero).astype(BF16), jnp.where(q_lo, zero, q).astype(BF16))
        bias = bias_ref[...]
        v_ext = jnp.concatenate([v, jnp.ones_like(v)], axis=1)
        maxes, sums, pvs = [], [], []
        for h in range(2):
            s = _nt_dot(qs[h], k) + bias
            mh = jnp.max(s, axis=1, keepdims=True)
            p = jnp.exp2(s - mh)
            maxes.append(mh)
            pvl = _dot(p.astype(BF16), v_ext)
            pvs.append(pvl[:, :LANES])
            sums.append(pvl[:, LANES:])
        m_blk = jnp.where(lo, maxes[0], maxes[1])
        m_old = m_sc[qsl].reshape(tq, LANES)
        m_new = jnp.maximum(m_old, m_blk)
        a_old = jnp.exp2(m_old - m_new)
        a_blk = jnp.exp2(m_blk - m_new)
        l_new = a_old * l_sc[qsl].reshape(tq, LANES) + a_blk * jnp.where(lo, sums[0], sums[1])
        acc_new = a_old * acc_sc[qsl].reshape(tq, LANES) + a_blk * jnp.where(lo, pvs[0], pvs[1])
        m_sc[qsl] = m_new.reshape(planes, rows, LANES)
        l_sc[qsl] = l_new.reshape(planes, rows, LANES)
        acc_sc[qsl] = acc_new.reshape(planes, rows, LANES)

    bias_refs = {1: (b1f_ref, b1r_ref), 4: (b4f_ref, b4r_ref), 16: (b16f_ref, b16r_ref)}
    for window, dil in DILATED_PAIRS:
        assert window // dil == tq
        planes, rows, _ = _dilated_geometry(dil)
        nblk = rows_per_plane // rows
        bf_ref, br_ref = bias_refs[dil]

        def first_body(sub, carry, planes=planes, rows=rows, bf_ref=bf_ref):
            block(planes, rows, sub * planes, 0, True, bf_ref)
            return carry

        lax.fori_loop(0, dil, first_body, 0, unroll=min(dil, DIL_UNROLL))

        if nblk > 1:
            def rest_body(t, carry, planes=planes, rows=rows, nblk=nblk, br_ref=br_ref):
                sub = t // (nblk - 1)
                blk = t % (nblk - 1) + 1
                block(planes, rows, sub * planes, pl.multiple_of(blk * rows, rows), False, br_ref)
                return carry

            lax.fori_loop(0, dil * (nblk - 1), rest_body, 0, unroll=DIL_UNROLL)

    for r16 in range(DIL_PLANES):
        slot = _plane_slot(r16)
        o_ref[0, :, plane_lanes(r16)] = (acc_sc[slot] / l_sc[slot]).astype(o_ref.dtype)


def _dilated_attention(qc, kc, vc):
    B, S, C = qc.shape
    n_pairs = DIL_HEADS // 2
    rpp = S // DIL_PLANES
    assert S % DIL_PLANES == 0 and rpp % DIL_TQ == 0 and rpp >= 2 * DIL_TQ
    width = DIL_PLANES * C

    def view(a):
        return a.reshape(B, rpp, width)

    spec = pl.BlockSpec((1, rpp, width), lambda b, p: (b, 0, 0))
    biases = [_dilated_bias(d, f) for _, d in DILATED_PAIRS for f in (True, False)]
    bspecs = [pl.BlockSpec(bb.shape, lambda b, p: (0, 0)) for bb in biases]
    plane = pltpu.VMEM((DIL_PLANES, rpp, LANES), F32)
    out = pl.pallas_call(
        functools.partial(_dilated_kernel, rows_per_plane=rpp),
        out_shape=jax.ShapeDtypeStruct((B, rpp, width), BF16),
        grid=(B, n_pairs),
        in_specs=[spec, spec, spec] + bspecs,
        out_specs=spec,
        scratch_shapes=[plane] * 6,
        compiler_params=_cparams(("parallel", "arbitrary")),
        name="dilated",
    )(view(qc), view(kc), view(vc), *biases)
    return out.reshape(B, S, C)


def _tn_dot(a_t, b):
    return lax.dot_general(a_t, b, (((0,), (0,)), ((), ())), preferred_element_type=F32)


def _mix_out_kernel(oat_ref, obt_ref, oc_ref, x_ref, wa_ref, wb_ref, wc_ref, g_ref, b_ref, o_ref):
    y = (_tn_dot(oat_ref[0], wa_ref[...]) + _tn_dot(obt_ref[0], wb_ref[...])
         + _dot(oc_ref[0], wc_ref[...]))
    o_ref[0] = _layer_norm(DEEPNORM_ALPHA * x_ref[0] + y, g_ref[...], b_ref[...])


def _mix_out(oa_t, ob_t, oc, x, w_out, g, b, tm):
    B, S, D = x.shape
    na, nb = oa_t.shape[1], ob_t.shape[1]
    wa, wb, wc = w_out[:na], w_out[na:na + nb], w_out[na + nb:]

    def cols(a):
        return pl.BlockSpec((1, a.shape[1], tm), lambda bb, i: (bb, 0, i))

    def rows(a):
        return pl.BlockSpec((1, tm, a.shape[2]), lambda bb, i: (bb, i, 0))

    def full(a):
        return pl.BlockSpec(a.shape, lambda bb, i: (0, 0))

    return pl.pallas_call(
        _mix_out_kernel,
        out_shape=jax.ShapeDtypeStruct((B, S, D), F32),
        grid=(B, S // tm),
        in_specs=[cols(oa_t), cols(ob_t), rows(oc), rows(x), full(wa), full(wb), full(wc), full(g), full(b)],
        out_specs=rows(x),
        compiler_params=_cparams(("parallel", "parallel")),
        name="mix_out",
    )(oa_t, ob_t, oc, x, wa, wb, wc, g, b)


def _mem_kv_kernel(mem_ref, w_ref, k_ref, v_ref):
    kv = _dot(mem_ref[0].astype(BF16), w_ref[...])
    d = k_ref.shape[-1]
    k_ref[0] = kv[:, :d].astype(BF16)
    v_ref[0] = kv[:, d:].astype(BF16)


def _mem_kv(mem, w_kv):
    B, M, D = mem.shape
    spec = pl.BlockSpec((1, M, D), lambda b: (b, 0, 0))
    return pl.pallas_call(
        _mem_kv_kernel,
        out_shape=(jax.ShapeDtypeStruct((B, M, D), BF16),) * 2,
        grid=(B,),
        in_specs=[spec, pl.BlockSpec(w_kv.shape, lambda b: (0, 0))],
        out_specs=(spec, spec),
        compiler_params=_cparams(("parallel",)),
        name="mem_kv",
    )(mem, w_kv)


def _xattn_kernel(x_ref, k_ref, v_ref, wq_ref, wo_ref, g_ref, b_ref, o_ref):
    x = x_ref[0]
    d = x.shape[-1]
    dh = d // XATTN_HEADS
    q = (_dot(x.astype(BF16), wq_ref[...]) * (dh ** -0.5)).astype(BF16)
    outs = []
    for h in range(XATTN_HEADS):
        sl = slice(h * dh, (h + 1) * dh)
        s = _nt_dot(q[:, sl], k_ref[0, :, sl])
        p = jnp.exp(s - jnp.max(s, axis=1, keepdims=True))
        o = _dot(p.astype(BF16), v_ref[0, :, sl])
        outs.append((o / jnp.sum(p, axis=1, keepdims=True)).astype(BF16))
    o_all = jnp.concatenate(outs, axis=1)
    y = _dot(o_all, wo_ref[...])
    o_ref[0] = _layer_norm(DEEPNORM_ALPHA * x + y, g_ref[...], b_ref[...])


def _xattn(x, k_mem, v_mem, w_q, w_o, g, b, tm):
    B, S, D = x.shape
    M = k_mem.shape[1]
    xspec = pl.BlockSpec((1, tm, D), lambda bb, i: (bb, i, 0))
    mspec = pl.BlockSpec((1, M, D), lambda bb, i: (bb, 0, 0))

    def full(a):
        return pl.BlockSpec(a.shape, lambda bb, i: (0, 0))

    return pl.pallas_call(
        _xattn_kernel,
        out_shape=jax.ShapeDtypeStruct((B, S, D), F32),
        grid=(B, S // tm),
        in_specs=[xspec, mspec, mspec, full(w_q), full(w_o), full(g), full(b)],
        out_specs=xspec,
        compiler_params=_cparams(("parallel", "parallel")),
        name="xattn",
    )(x, k_mem, v_mem, w_q, w_o, g, b)


def _router_gates_t(x, rw_t, rbias):
    logits_t = _nt_dot(rw_t, x, HIGHEST)
    scores = 1.0 / (1.0 + jnp.exp(-logits_t))
    biased = scores + rbias
    sc = [scores[e:e + 1, :] for e in range(N_EXPERTS)]
    bs = [biased[e:e + 1, :] for e in range(N_EXPERTS)]
    epg = EXPERTS_PER_GROUP

    def beats(a, b, a_first):
        return (a >= b) if a_first else (a > b)

    grp = []
    for gi in range(N_GROUPS):
        v = bs[gi * epg:(gi + 1) * epg]
        best = None
        for a in range(epg):
            for b in range(a + 1, epg):
                pair = v[a] + v[b]
                best = pair if best is None else jnp.maximum(best, pair)
        grp.append(best)
    gates = []
    for gi in range(N_GROUPS):
        g_sel = None
        for gj in range(N_GROUPS):
            if gj == gi:
                continue
            w = beats(grp[gi], grp[gj], gi < gj)
            g_sel = w if g_sel is None else (g_sel & w)
        in_top = []
        for a in range(epg):
            ea = gi * epg + a
            n_above = jnp.zeros(bs[ea].shape, jnp.int32)
            for b in range(epg):
                if b == a:
                    continue
                eb = gi * epg + b
                n_above = n_above + beats(bs[eb], bs[ea], b < a).astype(jnp.int32)
            in_top.append(g_sel & (n_above < 2))
        denom = None
        for a in range(epg):
            term = jnp.where(in_top[a], sc[gi * epg + a], 0.0)
            denom = term if denom is None else denom + term
        denom = jnp.where(g_sel, denom, 1.0)
        for a in range(epg):
            gates.append(jnp.where(in_top[a], sc[gi * epg + a] / denom, 0.0))
    return jnp.concatenate(gates, axis=0)


def _moe_kernel(x_ref, rwt_ref, rb_ref, wg_ref, wu_ref, wd_ref, g_ref, b_ref, o_ref,
                gates_sc, acc_sc, xb_sc, *, tm):
    e = pl.program_id(1)

    @pl.when(e == 0)
    def _():
        x = x_ref[...]
        gates_t = _router_gates_t(x, rwt_ref[...], rb_ref[...])
        eye = (lax.broadcasted_iota(jnp.int32, (tm, tm), 0)
               == lax.broadcasted_iota(jnp.int32, (tm, tm), 1)).astype(F32)
        gates_sc[...] = _nt_dot(eye, gates_t, HIGHEST)
        acc_sc[...] = jnp.zeros_like(acc_sc)
        xb_sc[...] = x.astype(BF16)

    xb = xb_sc[...]
    hg = _dot(xb, wg_ref[0])
    hu = _dot(xb, wu_ref[0])
    lane_e = lax.broadcasted_iota(jnp.int32, (1, N_EXPERTS), 1)
    gate = jnp.sum(jnp.where(lane_e == e, gates_sc[...], 0.0), axis=1, keepdims=True)
    a = (hg / (1.0 + jnp.exp(-hg))) * hu * gate
    acc_sc[...] += _dot(a.astype(BF16), wd_ref[0])

    @pl.when(e == N_EXPERTS - 1)
    def _():
        o_ref[...] = _layer_norm(DEEPNORM_ALPHA * x_ref[...] + acc_sc[...], g_ref[...], b_ref[...])


def _moe(x2d, rw_t, rbias, wg, wu, wd, g, b, tm):
    T, D = x2d.shape
    xspec = pl.BlockSpec((tm, D), lambda i, e: (i, 0))

    def full(a):
        return pl.BlockSpec(a.shape, lambda i, e: (0, 0))

    return pl.pallas_call(
        functools.partial(_moe_kernel, tm=tm),
        out_shape=jax.ShapeDtypeStruct((T, D), F32),
        grid=(T // tm, N_EXPERTS),
        in_specs=[xspec, full(rw_t), full(rbias),
                  pl.BlockSpec((1, D, D_EXPERT), lambda i, e: (e, 0, 0)),
                  pl.BlockSpec((1, D, D_EXPERT), lambda i, e: (e, 0, 0)),
                  pl.BlockSpec((1, D_EXPERT, D), lambda i, e: (e, 0, 0)),
                  full(g), full(b)],
        out_specs=xspec,
        scratch_shapes=[pltpu.VMEM((tm, N_EXPERTS), F32), pltpu.VMEM((tm, D), F32),
                        pltpu.VMEM((tm, D), BF16)],
        compiler_params=_cparams(("parallel", "arbitrary")),
        name="moe",
    )(x2d, rw_t, rbias, wg, wu, wd, g, b)


def _rope_constants():
    half_d = DIL_HEAD_DIM // 2
    inv_d = ROPE_THETA ** (-jnp.arange(half_d, dtype=F32) / half_d)
    half_a = MLA_ROPE_DIM // 2
    inv_a = ROPE_THETA ** (-jnp.arange(half_a, dtype=F32) / half_a)
    lane = np.arange(LANES)
    invf_d = inv_d[lane % half_d].reshape(1, LANES)
    sgn_d = jnp.asarray(np.where(lane < HEAD_LANES, -1.0, 1.0).reshape(1, LANES), F32)
    in_rope = (lane >= MLA_NOPE_DIM) & (lane < MLA_NOPE_DIM + MLA_ROPE_DIM)
    invf_a = jnp.where(jnp.asarray(in_rope), inv_a[(lane - MLA_NOPE_DIM) % half_a], 0.0).reshape(1, LANES)
    sgn_a = jnp.ones((1, LANES), F32)
    return invf_a, sgn_a, invf_d, sgn_d


def _dil_pair_perm():
    half = DIL_HEAD_DIM // 2
    cols = []
    for p in range(DIL_HEADS // 2):
        h0, h1 = 2 * p, 2 * p + 1
        for h, off in ((h0, 0), (h1, 0), (h0, half), (h1, half)):
            cols.extend(range(h * DIL_HEAD_DIM + off, h * DIL_HEAD_DIM + off + half))
    return np.asarray(cols)


def _layer_weights(w_in, b_forget, q_gain, kv_gain, w_uq, w_ukv):
    D = w_in.shape[0]
    cuts = np.cumsum([MLA_Q_RANK, MLA_KV_RANK, MLA_ROPE_DIM, 3 * FOX_HEADS * FOX_HEAD_DIM, FOX_HEADS])
    w_cq, w_ckv, w_kr, w_fox, w_ff, w_dil = jnp.split(w_in, cuts.tolist(), axis=1)
    half = MLA_ROPE_DIM // 2
    w_kr_rot = jnp.concatenate([-w_kr[:, half:], w_kr[:, :half]], axis=1)

    def place_rope(w):
        return jnp.pad(w, ((0, 0), (MLA_NOPE_DIM, LANES - MLA_NOPE_DIM - MLA_ROPE_DIM)))

    wa = jnp.concatenate([w_cq, w_ckv, place_rope(w_kr), place_rope(w_kr_rot),
                          jnp.pad(w_ff, ((0, 0), (0, LANES - FOX_HEADS)))], axis=1)

    nd = DIL_HEADS * DIL_HEAD_DIM
    perm = _dil_pair_perm()
    wdil = jnp.concatenate([w_dil[:, :nd][:, perm], w_dil[:, nd:2 * nd][:, perm], w_dil[:, 2 * nd:]], axis=1)

    dq = MLA_NOPE_DIM + MLA_ROPE_DIM
    uq = w_uq.reshape(MLA_Q_RANK, MLA_HEADS, dq)
    uq_rope = uq[:, :, MLA_NOPE_DIM:]
    uq_rot = jnp.concatenate([-uq_rope[:, :, half:], uq_rope[:, :, :half]], axis=2)
    pad_tail = LANES - dq
    uq_plain = jnp.pad(uq, ((0, 0), (0, 0), (0, pad_tail))).reshape(MLA_Q_RANK, MLA_HEADS * LANES)
    uq_rotp = jnp.pad(uq_rot, ((0, 0), (0, 0), (MLA_NOPE_DIM, pad_tail))).reshape(MLA_Q_RANK, MLA_HEADS * LANES)
    wuq = jnp.concatenate([uq_plain, uq_rotp], axis=1)

    ukv = w_ukv.reshape(MLA_KV_RANK, MLA_HEADS, MLA_NOPE_DIM + MLA_V_DIM)
    wuk = jnp.pad(ukv[:, :, :MLA_NOPE_DIM], ((0, 0), (0, 0), (0, LANES - MLA_NOPE_DIM)))
    wuk = wuk.reshape(MLA_KV_RANK, MLA_HEADS * LANES)
    wuv = ukv[:, :, MLA_NOPE_DIM:].reshape(MLA_KV_RANK, MLA_HEADS * MLA_V_DIM)

    bf = jnp.pad(b_forget.astype(F32), (0, LANES - FOX_HEADS)).reshape(1, LANES)
    nf = FOX_HEADS * FOX_HEAD_DIM
    return dict(wa=wa.astype(BF16), wfox=w_fox[:, :2 * nf].astype(BF16),
                wfoxv_t=w_fox[:, 2 * nf:].T.astype(BF16), wdil=wdil.astype(BF16),
                qg=q_gain.reshape(1, -1).astype(F32), kvg=kv_gain.reshape(1, -1).astype(F32),
                wuq=wuq.astype(BF16), wuk=wuk.astype(BF16), wuv_t=wuv.T.astype(BF16), bf=bf,
                sel=_fox_bias_placement())


def _fox_bias_placement():
    sel = np.zeros((3 * LANES, FOX_HEADS * LANES), np.float32)
    for j in range(3):
        for h in range(FOX_HEADS):
            sel[j * LANES + h, h * LANES + FOX_HEAD_DIM + j] = 1.0
    return jnp.asarray(sel, BF16)


def _tiles(S):
    tm = min(512, S)
    return dict(rope=min(512, S), proj=tm, flash=min(512, S), mix=tm, xattn=tm, moe=tm)


def kernel(x, mem, positions, w_in, b_forget, mla_q_gain, mla_kv_gain, mla_w_uq, mla_w_ukv, w_mix_out, ln_mix_g, ln_mix_b, xattn_w_q, xattn_w_kv, xattn_w_o, ln_mem_g, ln_mem_b, router_w, router_bias, expert_w_gate, expert_w_up, expert_w_down, ln_ffn_g, ln_ffn_b):
    B, S, D = x.shape
    depth = w_in.shape[0]
    t = _tiles(S)
    invf_a, sgn_a, invf_d, sgn_d = _rope_constants()
    cos_a, sin_a = _rope_tables(positions, invf_a, sgn_a, t["rope"])
    cos_d, sin_d = _rope_tables(positions, invf_d, sgn_d, t["rope"])
    tabs = (cos_a, sin_a, cos_d, sin_d)
    rw_t = router_w.T.astype(F32)
    rbias = router_bias.reshape(-1, 1).astype(F32)

    def row(v):
        return v.reshape(1, -1).astype(F32)

    for l in range(depth):
        wts = _layer_weights(w_in[l], b_forget[l], mla_q_gain[l], mla_kv_gain[l], mla_w_uq[l], mla_w_ukv[l])
        qa, ka, va_t, qb, kb, vb_t, qc, kc, vc = _proj(x, wts, tabs, t["proj"])
        oa_t = _flash_attention(qa, ka, va_t, MLA_HEADS, t["flash"], "mla_flash")
        ob_t = _flash_attention(qb, kb, vb_t, FOX_HEADS, t["flash"], "fox_flash")
        oc = _dilated_attention(qc, kc, vc)
        x = _mix_out(oa_t, ob_t, oc, x, w_mix_out[l].astype(BF16), row(ln_mix_g[l]), row(ln_mix_b[l]),
                     t["mix"])
        k_mem, v_mem = _mem_kv(mem, xattn_w_kv[l].astype(BF16))
        x = _xattn(x, k_mem, v_mem, xattn_w_q[l].astype(BF16), xattn_w_o[l].astype(BF16),
                   row(ln_mem_g[l]), row(ln_mem_b[l]), t["xattn"])
        x2d = _moe(x.reshape(B * S, D), rw_t, rbias, expert_w_gate[l].astype(BF16),
                   expert_w_up[l].astype(BF16), expert_w_down[l].astype(BF16),
                   row(ln_ffn_g[l]), row(ln_ffn_b[l]), t["moe"])
        x = x2d.reshape(B, S, D)
    return x
```

```python
import functools
import math

import numpy as np
import jax
import jax.numpy as jnp
from jax import lax
from jax.experimental import pallas as pl
from jax.experimental.pallas import tpu as pltpu

F32 = jnp.float32
BF16 = jnp.bfloat16
HIGHEST = lax.Precision.HIGHEST

D_MODEL = 1024
DEPTH = 4
MLA_HEADS = 4
MLA_Q_RANK = 256
MLA_KV_RANK = 128
MLA_NOPE_DIM = 64
MLA_ROPE_DIM = 32
MLA_V_DIM = 64
FOX_HEADS = 6
FOX_HEAD_DIM = 64
DIL_HEADS = 6
DIL_HEAD_DIM = 64
DILATED_PAIRS = ((128, 1), (512, 4), (2048, 16))
XATTN_HEADS = 4
N_EXPERTS = 16
N_GROUPS = 4
EXPERTS_PER_GROUP = N_EXPERTS // N_GROUPS
D_EXPERT = 256
ROPE_THETA = 10000.0
NORM_EPS = 1e-5
NEG_INF = -1e30
DEEPNORM_ALPHA = (2 * DEPTH) ** 0.25
LOG2E = math.log2(math.e)

LANES = 128
HEAD_LANES = 64
VMEM_LIMIT_BYTES = 52 * 1024 * 1024

A_CQ = 0
A_CKV = A_CQ + MLA_Q_RANK
A_KR = A_CKV + MLA_KV_RANK
A_KRR = A_KR + LANES
A_FF = A_KRR + LANES
A_COLS = A_FF + LANES


def _cparams(sem):
    return pltpu.CompilerParams(dimension_semantics=sem, vmem_limit_bytes=VMEM_LIMIT_BYTES)


def _nt_dot(a, b, precision=None):
    return lax.dot_general(a, b, (((1,), (1,)), ((), ())), precision=precision,
                           preferred_element_type=F32)


def _dot(a, b, precision=None):
    return jnp.dot(a, b, precision=precision, preferred_element_type=F32)


def _layer_norm(y, g, b):
    mu = jnp.mean(y, axis=-1, keepdims=True)
    yc = y - mu
    var = jnp.mean(yc * yc, axis=-1, keepdims=True)
    return yc * lax.rsqrt(var + NORM_EPS) * g + b


def _rms_norm(y, g):
    ms = jnp.mean(y * y, axis=-1, keepdims=True)
    return y * lax.rsqrt(ms + NORM_EPS) * g


def _rope_table_kernel(pos_ref, invf_ref, sgn_ref, cos_ref, sin_ref):
    ang = pos_ref[0].astype(F32) * invf_ref[...]
    cos_ref[0] = jnp.cos(ang)
    sin_ref[0] = jnp.sin(ang) * sgn_ref[...]


def _rope_tables(positions, invf, sgn, ts):
    B, S = positions.shape
    pos3 = positions.reshape(B, S, 1)
    spec = pl.BlockSpec((1, ts, LANES), lambda b, i: (b, i, 0))
    vec = pl.BlockSpec((1, LANES), lambda b, i: (0, 0))
    return pl.pallas_call(
        _rope_table_kernel,
        out_shape=(jax.ShapeDtypeStruct((B, S, LANES), F32),) * 2,
        grid=(B, S // ts),
        in_specs=[pl.BlockSpec((1, ts, 1), lambda b, i: (b, i, 0)), vec, vec],
        out_specs=(spec, spec),
        compiler_params=_cparams(("parallel", "parallel")),
        name="rope_tables",
    )(pos3, invf, sgn)


def _proj_kernel(x_ref, wa_ref, wfox_ref, wfoxv_ref, wdil_ref, qg_ref, kvg_ref, wuq_ref, wuk_ref,
                 wuv_ref, bf_ref, sel_ref, cosa_ref, sina_ref, cosd_ref, sind_ref,
                 qa_ref, ka_ref, va_ref, qb_ref, kb_ref, vb_ref,
                 qc_ref, kc_ref, vc_ref, carry_c, *dil_scs, tm):
    i = pl.program_id(1)
    xb = x_ref[0].astype(BF16)

    ha = _dot(xb, wa_ref[...])
    cqn = _rms_norm(ha[:, A_CQ:A_CQ + MLA_Q_RANK], qg_ref[...]).astype(BF16)
    ckvn = _rms_norm(ha[:, A_CKV:A_CKV + MLA_KV_RANK], kvg_ref[...]).astype(BF16)
    cos_a = cosa_ref[0]
    sin_a = sina_ref[0]
    q2 = _dot(cqn, wuq_ref[...])
    k_nope = _dot(ckvn, wuk_ref[...])
    k_rot = ha[:, A_KR:A_KR + LANES] * cos_a + ha[:, A_KRR:A_KRR + LANES] * sin_a
    q_scale = (MLA_NOPE_DIM + MLA_ROPE_DIM) ** -0.5 * LOG2E
    n_half = MLA_HEADS * LANES
    for h in range(MLA_HEADS):
        sl = slice(h * LANES, (h + 1) * LANES)
        qh = q2[:, sl] * cos_a + q2[:, n_half + h * LANES:n_half + (h + 1) * LANES] * sin_a
        qa_ref[0, :, sl] = (qh * q_scale).astype(BF16)
        ka_ref[0, :, sl] = (k_nope[:, sl] + k_rot).astype(BF16)
    va_ref[0] = _nt_dot(wuv_ref[...], ckvn).astype(BF16)

    z = ha[:, A_FF:A_FF + LANES] + bf_ref[...]
    logf = jnp.minimum(z, 0.0) - jnp.log(1.0 + jnp.exp(-jnp.abs(z)))
    lane = lax.broadcasted_iota(jnp.int32, (1, LANES), 1)
    logf = jnp.where(lane < FOX_HEADS, logf, 0.0)

    @pl.when(i == 0)
    def _():
        carry_c[...] = jnp.zeros_like(carry_c)

    r = lax.broadcasted_iota(jnp.int32, (tm, tm), 0)
    c = lax.broadcasted_iota(jnp.int32, (tm, tm), 1)
    lower = (c <= r).astype(F32)
    fcol = _dot(lower, logf, HIGHEST) + carry_c[...]
    carry_c[...] = fcol[tm - 1:tm, :]

    nb = fcol * (-LOG2E)
    hi = nb.astype(BF16)
    r1 = nb - hi.astype(F32)
    mid = r1.astype(BF16)
    lo3 = (r1 - mid.astype(F32)).astype(BF16)
    placed = _dot(jnp.concatenate([hi, mid, lo3], axis=1), sel_ref[...])

    hf = _dot(xb, wfox_ref[...])
    nf = FOX_HEADS * FOX_HEAD_DIM
    dims = lane < FOX_HEAD_DIM
    ones3 = jnp.where((lane >= FOX_HEAD_DIM) & (lane < FOX_HEAD_DIM + 3), 1.0, 0.0)
    fq_scale = FOX_HEAD_DIM ** -0.5 * LOG2E
    for h in range(FOX_HEADS):
        blk = slice((h // 2) * LANES, (h // 2 + 1) * LANES)
        out = slice(h * LANES, (h + 1) * LANES)
        qh = hf[:, blk]
        kh = hf[:, nf + (h // 2) * LANES:nf + (h // 2 + 1) * LANES]
        if h % 2:
            qh = pltpu.roll(qh, HEAD_LANES, 1)
            kh = pltpu.roll(kh, HEAD_LANES, 1)
        qb_ref[0, :, out] = jnp.where(dims, qh * fq_scale, ones3).astype(BF16)
        kb_ref[0, :, out] = jnp.where(dims, kh, placed[:, out]).astype(BF16)
    vb_ref[0] = _nt_dot(wfoxv_ref[...], xb).astype(BF16)

    hd = _dot(xb, wdil_ref[...])
    cos_d = cosd_ref[0]
    sin_d = sind_ref[0]
    nd = DIL_HEADS * DIL_HEAD_DIM
    for pblk in range(nd // LANES):
        sl = slice(pblk * LANES, (pblk + 1) * LANES)
        qh = hd[:, sl]
        kh = hd[:, nd + pblk * LANES:nd + (pblk + 1) * LANES]
        qh = qh * cos_d + pltpu.roll(qh, HEAD_LANES, 1) * sin_d
        kh = kh * cos_d + pltpu.roll(kh, HEAD_LANES, 1) * sin_d
        dil_scs[pblk][...] = qh * (DIL_HEAD_DIM ** -0.5 * LOG2E)
        dil_scs[3 + pblk][...] = kh
        dil_scs[6 + pblk][...] = hd[:, 2 * nd + pblk * LANES:2 * nd + (pblk + 1) * LANES]
    for a, ref in enumerate((qc_ref, kc_ref, vc_ref)):
        for r in range(DIL_PLANES):
            rows = pl.ds(r, tm // DIL_PLANES, stride=DIL_PLANES)
            for pblk in range(nd // LANES):
                lanes = slice(r * nd + pblk * LANES, r * nd + (pblk + 1) * LANES)
                ref[0, :, lanes] = dil_scs[3 * a + pblk][rows, :].astype(BF16)


def _proj(x, wts, tabs, tm):
    B, S, D = x.shape
    cos_a, sin_a, cos_d, sin_d = tabs
    grid = (B, S // tm)

    def full(a):
        return pl.BlockSpec(a.shape, lambda b, i: (0,) * a.ndim)

    def tok(width, dtype):
        return (jax.ShapeDtypeStruct((B, S, width), dtype),
                pl.BlockSpec((1, tm, width), lambda b, i: (b, i, 0)))

    tab = pl.BlockSpec((1, tm, LANES), lambda b, i: (b, i, 0))
    nd = DIL_HEADS * DIL_HEAD_DIM

    def planes(dtype):
        return (jax.ShapeDtypeStruct((B, S // DIL_PLANES, DIL_PLANES * nd), dtype),
                pl.BlockSpec((1, tm // DIL_PLANES, DIL_PLANES * nd), lambda b, i: (b, i, 0)))

    def tok_t(height, dtype):
        return (jax.ShapeDtypeStruct((B, height, S), dtype),
                pl.BlockSpec((1, height, tm), lambda b, i: (b, 0, i)))

    outs = [tok(MLA_HEADS * LANES, BF16), tok(MLA_HEADS * LANES, BF16),
            tok_t(MLA_HEADS * MLA_V_DIM, BF16),
            tok(FOX_HEADS * LANES, BF16), tok(FOX_HEADS * LANES, BF16),
            tok_t(FOX_HEADS * FOX_HEAD_DIM, BF16),
            planes(BF16), planes(BF16), planes(BF16)]
    w_list = [wts["wa"], wts["wfox"], wts["wfoxv_t"], wts["wdil"], wts["qg"], wts["kvg"], wts["wuq"],
              wts["wuk"], wts["wuv_t"], wts["bf"], wts["sel"]]
    return pl.pallas_call(
        functools.partial(_proj_kernel, tm=tm),
        out_shape=tuple(o[0] for o in outs),
        grid=grid,
        in_specs=[pl.BlockSpec((1, tm, D), lambda b, i: (b, i, 0))] + [full(w) for w in w_list]
        + [tab, tab, tab, tab],
        out_specs=tuple(o[1] for o in outs),
        scratch_shapes=[pltpu.VMEM((1, LANES), F32)] + [pltpu.VMEM((tm, LANES), F32)] * (3 * nd // LANES),
        compiler_params=_cparams(("parallel", "arbitrary")),
        name="proj",
    )(x, *w_list, cos_a, sin_a, cos_d, sin_d)


FLASH_CHUNK = 32


def _flash_kernel(q0_ref, q1_ref, k0_ref, k1_ref, vt_ref, o_ref,
                  s0_sc, s1_sc, p0_sc, p1_sc, m_sc, l_sc, acc0_sc, acc1_sc, *, tq):
    i = pl.program_id(2)
    q_refs = (q0_ref, q1_ref)
    k_refs = (k0_ref, k1_ref)
    s_scs = (s0_sc, s1_sc)
    p_scs = (p0_sc, p1_sc)
    acc_scs = (acc0_sc, acc1_sc)
    dv = HEAD_LANES
    n_chunks = tq // FLASH_CHUNK

    m_sc[...] = jnp.full_like(m_sc, NEG_INF)
    l_sc[...] = jnp.zeros_like(l_sc)
    acc0_sc[...] = jnp.zeros_like(acc0_sc)
    acc1_sc[...] = jnp.zeros_like(acc1_sc)

    def step(j, masked):
        ks = pl.multiple_of(j * tq, tq)
        m_news, alphas = [], []
        for h in range(2):
            k = k_refs[h][0, pl.ds(ks, tq), :]
            st = _nt_dot(k, q_refs[h][0])
            if masked:
                key = lax.broadcasted_iota(jnp.int32, (tq, tq), 0)
                qry = lax.broadcasted_iota(jnp.int32, (tq, tq), 1)
                st = jnp.where(key <= qry, st, NEG_INF)
            s_scs[h][...] = st
            m8 = jnp.max(st.reshape(tq // 8, 8, tq), axis=0)
            m_prev = m_sc[h]
            m_new = jnp.maximum(m_prev, jnp.max(m8, axis=0, keepdims=True))
            m_news.append(m_new)
            alphas.append(jnp.exp2(m_prev - m_new))
            m_sc[h] = m_new
        for h in range(2):
            l8 = jnp.zeros((8, tq), F32)
            for c in range(n_chunks):
                rows = slice(c * FLASH_CHUNK, (c + 1) * FLASH_CHUNK)
                p = jnp.exp2(s_scs[h][rows, :] - m_news[h])
                l8 = l8 + jnp.sum(p.reshape(FLASH_CHUNK // 8, 8, tq), axis=0)
                p_scs[h][rows, :] = p.astype(BF16)
            vt = vt_ref[0, h * dv:(h + 1) * dv, pl.ds(ks, tq)]
            pv = _dot(vt, p_scs[h][...])
            acc_scs[h][...] = acc_scs[h][...] * alphas[h] + pv
            l_sc[h] = l_sc[h] * alphas[h] + jnp.sum(l8, axis=0, keepdims=True)

    def body(j, carry):
        step(j, False)
        return carry

    lax.fori_loop(0, i, body, 0)
    step(i, True)
    for h in range(2):
        o_ref[0, h * dv:(h + 1) * dv, :] = (acc_scs[h][...] / l_sc[h]).astype(o_ref.dtype)


def _flash_attention(q, k, vt, n_heads, tq, name):
    B, S, _ = q.shape
    n_pairs = n_heads // 2
    dv2 = 2 * HEAD_LANES
    qspec0 = pl.BlockSpec((1, tq, LANES), lambda b, p, i: (b, i, 2 * p))
    qspec1 = pl.BlockSpec((1, tq, LANES), lambda b, p, i: (b, i, 2 * p + 1))
    kspec0 = pl.BlockSpec((1, S, LANES), lambda b, p, i: (b, 0, 2 * p))
    kspec1 = pl.BlockSpec((1, S, LANES), lambda b, p, i: (b, 0, 2 * p + 1))
    vspec = pl.BlockSpec((1, dv2, S), lambda b, p, i: (b, p, 0))
    return pl.pallas_call(
        functools.partial(_flash_kernel, tq=tq),
        out_shape=jax.ShapeDtypeStruct((B, n_heads * HEAD_LANES, S), BF16),
        grid=(B, n_pairs, S // tq),
        in_specs=[qspec0, qspec1, kspec0, kspec1, vspec],
        out_specs=pl.BlockSpec((1, dv2, tq), lambda b, p, i: (b, p, i)),
        scratch_shapes=[pltpu.VMEM((tq, tq), F32), pltpu.VMEM((tq, tq), F32),
                        pltpu.VMEM((tq, tq), BF16), pltpu.VMEM((tq, tq), BF16),
                        pltpu.VMEM((2, 1, tq), F32), pltpu.VMEM((2, 1, tq), F32),
                        pltpu.VMEM((HEAD_LANES, tq), F32), pltpu.VMEM((HEAD_LANES, tq), F32)],
        compiler_params=_cparams(("parallel", "parallel", "arbitrary")),
        name=name,
    )(q, q, k, k, vt)


DIL_PLANES = 16
DIL_TQ = 128
DIL_UNROLL = 4


def _plane_slot(r16):
    return 4 * (r16 % 4) + r16 // 4


def _dilated_geometry(dil):
    planes = DIL_PLANES // dil
    rows = DIL_TQ // planes
    if dil == 1:
        order = [(c // 4) + 4 * (c % 4) for c in range(planes)]
    else:
        order = list(range(planes))
    return planes, rows, order


def _dilated_bias(dil, first):
    planes, rows, order = _dilated_geometry(dil)
    krows = 2 * rows
    a = np.arange(planes * rows)
    b = np.arange(planes * krows)
    uq = np.asarray(order)[a // rows] + planes * (a % rows + (0 if first else rows))
    uk = np.asarray(order)[b // krows] + planes * (b % krows)
    diff = uq[:, None] - uk[None, :]
    valid = (diff >= 0) & (diff <= DIL_TQ)
    return jnp.asarray(np.where(valid, 0.0, NEG_INF), F32)


def _dilated_kernel(q_ref, k_ref, v_ref, b1f_ref, b1r_ref, b4f_ref, b4r_ref, b16f_ref, b16r_ref, o_ref,
                    q_st, k_st, v_st, m_sc, l_sc, acc_sc, *, rows_per_plane):
    pair = pl.program_id(1)
    n_pairs = DIL_HEADS // 2
    lane = lax.broadcasted_iota(jnp.int32, (1, LANES), 1)
    lo = lane < HEAD_LANES
    q_lo = (lane % HEAD_LANES) < (HEAD_LANES // 2)
    tq = DIL_TQ

    def plane_lanes(r16):
        return pl.ds(pl.multiple_of((r16 * n_pairs + pair) * LANES, LANES), LANES)

    for r16 in range(DIL_PLANES):
        slot = _plane_slot(r16)
        q_st[slot] = q_ref[0, :, plane_lanes(r16)].astype(F32)
        k_st[slot] = k_ref[0, :, plane_lanes(r16)].astype(F32)
        v_st[slot] = v_ref[0, :, plane_lanes(r16)].astype(F32)
    m_sc[...] = jnp.full_like(m_sc, NEG_INF)
    l_sc[...] = jnp.zeros_like(l_sc)
    acc_sc[...] = jnp.zeros_like(acc_sc)

    def block(planes, rows, slot0, row0, first, bias_ref):
        qsl = (pl.ds(slot0, planes), pl.ds(row0, rows), slice(None))
        k_row0 = row0 if first else row0 - rows
        ksl = (pl.ds(slot0, planes), pl.ds(k_row0, 2 * rows), slice(None))
        q = q_st[qsl].reshape(tq, LANES)
        k = k_st[ksl].reshape(-1, LANES).astype(BF16)
        v = v_st[ksl].reshape(-1, LANES).astype(BF16)
        zero = jnp.zeros_like(q)
        qs = (jnp.where(q_lo, q, zero).astype(BF16), jnp.where(q_lo, zero, q).astype(BF16))
        bias = bias_ref[...]
        v_ext = jnp.concatenate([v, jnp.ones_like(v)], axis=1)
        maxes, sums, pvs = [], [], []
        for h in range(2):
            s = _nt_dot(qs[h], k) + bias
            mh = jnp.max(s, axis=1, keepdims=True)
            p = jnp.exp2(s - mh)
            maxes.append(mh)
            pvl = _dot(p.astype(BF16), v_ext)
            pvs.append(pvl[:, :LANES])
            sums.append(pvl[:, LANES:])
        m_blk = jnp.where(lo, maxes[0], maxes[1])
        m_old = m_sc[qsl].reshape(tq, LANES)
        m_new = jnp.maximum(m_old, m_blk)
        a_old = jnp.exp2(m_old - m_new)
        a_blk = jnp.exp2(m_blk - m_new)
        l_new = a_old * l_sc[qsl].reshape(tq, LANES) + a_blk * jnp.where(lo, sums[0], sums[1])
        acc_new = a_old * acc_sc[qsl].reshape(tq, LANES) + a_blk * jnp.where(lo, pvs[0], pvs[1])
        m_sc[qsl] = m_new.reshape(planes, rows, LANES)
        l_sc[qsl] = l_new.reshape(planes, rows, LANES)
        acc_sc[qsl] = acc_new.reshape(planes, rows, LANES)

    bias_refs = {1: (b1f_ref, b1r_ref), 4: (b4f_ref, b4r_ref), 16: (b16f_ref, b16r_ref)}
    for window, dil in DILATED_PAIRS:
        assert window // dil == tq
        planes, rows, _ = _dilated_geometry(dil)
        nblk = rows_per_plane // rows
        bf_ref, br_ref = bias_refs[dil]

        def first_body(sub, carry, planes=planes, rows=rows, bf_ref=bf_ref):
            block(planes, rows, sub * planes, 0, True, bf_ref)
            return carry

        lax.fori_loop(0, dil, first_body, 0, unroll=min(dil, DIL_UNROLL))

        if nblk > 1:
            def rest_body(t, carry, planes=planes, rows=rows, nblk=nblk, br_ref=br_ref):
                sub = t // (nblk - 1)
                blk = t % (nblk - 1) + 1
                block(planes, rows, sub * planes, pl.multiple_of(blk * rows, rows), False, br_ref)
                return carry

            lax.fori_loop(0, dil * (nblk - 1), rest_body, 0, unroll=DIL_UNROLL)

    for r16 in range(DIL_PLANES):
        slot = _plane_slot(r16)
        o_ref[0, :, plane_lanes(r16)] = (acc_sc[slot] / l_sc[slot]).astype(o_ref.dtype)


def _dilated_attention(qc, kc, vc):
    B, rpp, width = qc.shape
    n_pairs = DIL_HEADS // 2
    assert rpp % DIL_TQ == 0 and rpp >= 2 * DIL_TQ

    spec = pl.BlockSpec((1, rpp, width), lambda b, p: (b, 0, 0))
    biases = [_dilated_bias(d, f) for _, d in DILATED_PAIRS for f in (True, False)]
    bspecs = [pl.BlockSpec(bb.shape, lambda b, p: (0, 0)) for bb in biases]
    plane = pltpu.VMEM((DIL_PLANES, rpp, LANES), F32)
    return pl.pallas_call(
        functools.partial(_dilated_kernel, rows_per_plane=rpp),
        out_shape=jax.ShapeDtypeStruct((B, rpp, width), BF16),
        grid=(B, n_pairs),
        in_specs=[spec, spec, spec] + bspecs,
        out_specs=spec,
        scratch_shapes=[plane] * 6,
        compiler_params=_cparams(("parallel", "arbitrary")),
        name="dilated",
    )(qc, kc, vc, *biases)


def _tn_dot(a_t, b):
    return lax.dot_general(a_t, b, (((0,), (0,)), ((), ())), preferred_element_type=F32)


def _mix_out_kernel(oat_ref, obt_ref, oc_ref, x_ref, wa_ref, wb_ref, wc_ref, g_ref, b_ref, o_ref, *oc_scs):
    nd = len(oc_scs) * LANES
    rows_per_plane = oc_scs[0].shape[0] // DIL_PLANES
    for r in range(DIL_PLANES):
        rows = pl.ds(r, rows_per_plane, stride=DIL_PLANES)
        for pblk, sc in enumerate(oc_scs):
            lanes = slice(r * nd + pblk * LANES, r * nd + (pblk + 1) * LANES)
            sc[rows, :] = oc_ref[0, :, lanes].astype(F32)
    oc = jnp.concatenate([sc[...] for sc in oc_scs], axis=1).astype(BF16)
    y = (_tn_dot(oat_ref[0], wa_ref[...]) + _tn_dot(obt_ref[0], wb_ref[...])
         + _dot(oc, wc_ref[...]))
    o_ref[0] = _layer_norm(DEEPNORM_ALPHA * x_ref[0] + y, g_ref[...], b_ref[...])


def _mix_out(oa_t, ob_t, oc, x, w_out, g, b, tm):
    B, S, D = x.shape
    na, nb = oa_t.shape[1], ob_t.shape[1]
    wa, wb, wc = w_out[:na], w_out[na:na + nb], w_out[na + nb:]

    def cols(a):
        return pl.BlockSpec((1, a.shape[1], tm), lambda bb, i: (bb, 0, i))

    def rows(a):
        return pl.BlockSpec((1, tm, a.shape[2]), lambda bb, i: (bb, i, 0))

    def full(a):
        return pl.BlockSpec(a.shape, lambda bb, i: (0, 0))

    return pl.pallas_call(
        _mix_out_kernel,
        out_shape=jax.ShapeDtypeStruct((B, S, D), F32),
        grid=(B, S // tm),
        in_specs=[cols(oa_t), cols(ob_t),
                  pl.BlockSpec((1, tm // DIL_PLANES, oc.shape[2]), lambda bb, i: (bb, i, 0)),
                  rows(x), full(wa), full(wb), full(wc), full(g), full(b)],
        out_specs=rows(x),
        scratch_shapes=[pltpu.VMEM((tm, LANES), F32)] * (oc.shape[2] // DIL_PLANES // LANES),
        compiler_params=_cparams(("parallel", "parallel")),
        name="mix_out",
    )(oa_t, ob_t, oc, x, wa, wb, wc, g, b)


def _mem_kv_kernel(mem_ref, w_ref, k_ref, v_ref):
    kv = _dot(mem_ref[0].astype(BF16), w_ref[...])
    d = k_ref.shape[-1]
    k_ref[0] = kv[:, :d].astype(BF16)
    v_ref[0] = kv[:, d:].astype(BF16)


def _mem_kv(mem, w_kv):
    B, M, D = mem.shape
    spec = pl.BlockSpec((1, M, D), lambda b: (b, 0, 0))
    return pl.pallas_call(
        _mem_kv_kernel,
        out_shape=(jax.ShapeDtypeStruct((B, M, D), BF16),) * 2,
        grid=(B,),
        in_specs=[spec, pl.BlockSpec(w_kv.shape, lambda b: (0, 0))],
        out_specs=(spec, spec),
        compiler_params=_cparams(("parallel",)),
        name="mem_kv",
    )(mem, w_kv)


def _xattn_kernel(x_ref, k_ref, v_ref, wq_ref, wo_ref, g_ref, b_ref, o_ref):
    x = x_ref[0]
    d = x.shape[-1]
    dh = d // XATTN_HEADS
    q = (_dot(x.astype(BF16), wq_ref[...]) * (dh ** -0.5)).astype(BF16)
    outs = []
    for h in range(XATTN_HEADS):
        sl = slice(h * dh, (h + 1) * dh)
        s = _nt_dot(q[:, sl], k_ref[0, :, sl])
        p = jnp.exp(s - jnp.max(s, axis=1, keepdims=True))
        o = _dot(p.astype(BF16), v_ref[0, :, sl])
        outs.append((o / jnp.sum(p, axis=1, keepdims=True)).astype(BF16))
    o_all = jnp.concatenate(outs, axis=1)
    y = _dot(o_all, wo_ref[...])
    o_ref[0] = _layer_norm(DEEPNORM_ALPHA * x + y, g_ref[...], b_ref[...])


def _xattn(x, k_mem, v_mem, w_q, w_o, g, b, tm):
    B, S, D = x.shape
    M = k_mem.shape[1]
    xspec = pl.BlockSpec((1, tm, D), lambda bb, i: (bb, i, 0))
    mspec = pl.BlockSpec((1, M, D), lambda bb, i: (bb, 0, 0))

    def full(a):
        return pl.BlockSpec(a.shape, lambda bb, i: (0, 0))

    return pl.pallas_call(
        _xattn_kernel,
        out_shape=jax.ShapeDtypeStruct((B, S, D), F32),
        grid=(B, S // tm),
        in_specs=[xspec, mspec, mspec, full(w_q), full(w_o), full(g), full(b)],
        out_specs=xspec,
        compiler_params=_cparams(("parallel", "parallel")),
        name="xattn",
    )(x, k_mem, v_mem, w_q, w_o, g, b)


def _router_gates_t(x, rw_t, rbias):
    logits_t = _nt_dot(rw_t, x, HIGHEST)
    scores = 1.0 / (1.0 + jnp.exp(-logits_t))
    biased = scores + rbias
    sc = [scores[e:e + 1, :] for e in range(N_EXPERTS)]
    bs = [biased[e:e + 1, :] for e in range(N_EXPERTS)]
    epg = EXPERTS_PER_GROUP

    def beats(a, b, a_first):
        return (a >= b) if a_first else (a > b)

    grp = []
    for gi in range(N_GROUPS):
        v = bs[gi * epg:(gi + 1) * epg]
        best = None
        for a in range(epg):
            for b in range(a + 1, epg):
                pair = v[a] + v[b]
                best = pair if best is None else jnp.maximum(best, pair)
        grp.append(best)
    gates = []
    for gi in range(N_GROUPS):
        g_sel = None
        for gj in range(N_GROUPS):
            if gj == gi:
                continue
            w = beats(grp[gi], grp[gj], gi < gj)
            g_sel = w if g_sel is None else (g_sel & w)
        in_top = []
        for a in range(epg):
            ea = gi * epg + a
            n_above = jnp.zeros(bs[ea].shape, jnp.int32)
            for b in range(epg):
                if b == a:
                    continue
                eb = gi * epg + b
                n_above = n_above + beats(bs[eb], bs[ea], b < a).astype(jnp.int32)
            in_top.append(g_sel & (n_above < 2))
        denom = None
        for a in range(epg):
            term = jnp.where(in_top[a], sc[gi * epg + a], 0.0)
            denom = term if denom is None else denom + term
        denom = jnp.where(g_sel, denom, 1.0)
        for a in range(epg):
            gates.append(jnp.where(in_top[a], sc[gi * epg + a] / denom, 0.0))
    return jnp.concatenate(gates, axis=0)


def _moe_kernel(x_ref, rwt_ref, rb_ref, wg_ref, wu_ref, wd_ref, g_ref, b_ref, o_ref,
                gates_sc, acc_sc, xb_sc, *, tm):
    e = pl.program_id(1)

    @pl.when(e == 0)
    def _():
        x = x_ref[...]
        gates_t = _router_gates_t(x, rwt_ref[...], rb_ref[...])
        eye = (lax.broadcasted_iota(jnp.int32, (tm, tm), 0)
               == lax.broadcasted_iota(jnp.int32, (tm, tm), 1)).astype(F32)
        gates_sc[...] = _nt_dot(eye, gates_t, HIGHEST)
        acc_sc[...] = jnp.zeros_like(acc_sc)
        xb_sc[...] = x.astype(BF16)

    xb = xb_sc[...]
    hg = _dot(xb, wg_ref[0])
    hu = _dot(xb, wu_ref[0])
    lane_e = lax.broadcasted_iota(jnp.int32, (1, N_EXPERTS), 1)
    gate = jnp.sum(jnp.where(lane_e == e, gates_sc[...], 0.0), axis=1, keepdims=True)
    a = (hg / (1.0 + jnp.exp(-hg))) * hu * gate
    acc_sc[...] += _dot(a.astype(BF16), wd_ref[0])

    @pl.when(e == N_EXPERTS - 1)
    def _():
        o_ref[...] = _layer_norm(DEEPNORM_ALPHA * x_ref[...] + acc_sc[...], g_ref[...], b_ref[...])


def _moe(x2d, rw_t, rbias, wg, wu, wd, g, b, tm):
    T, D = x2d.shape
    xspec = pl.BlockSpec((tm, D), lambda i, e: (i, 0))

    def full(a):
        return pl.BlockSpec(a.shape, lambda i, e: (0, 0))

    return pl.pallas_call(
        functools.partial(_moe_kernel, tm=tm),
        out_shape=jax.ShapeDtypeStruct((T, D), F32),
        grid=(T // tm, N_EXPERTS),
        in_specs=[xspec, full(rw_t), full(rbias),
                  pl.BlockSpec((1, D, D_EXPERT), lambda i, e: (e, 0, 0)),
                  pl.BlockSpec((1, D, D_EXPERT), lambda i, e: (e, 0, 0)),
                  pl.BlockSpec((1, D_EXPERT, D), lambda i, e: (e, 0, 0)),
                  full(g), full(b)],
        out_specs=xspec,
        scratch_shapes=[pltpu.VMEM((tm, N_EXPERTS), F32), pltpu.VMEM((tm, D), F32),
                        pltpu.VMEM((tm, D), BF16)],
        compiler_params=_cparams(("parallel", "arbitrary")),
        name="moe",
    )(x2d, rw_t, rbias, wg, wu, wd, g, b)


def _rope_constants():
    half_d = DIL_HEAD_DIM // 2
    inv_d = ROPE_THETA ** (-jnp.arange(half_d, dtype=F32) / half_d)
    half_a = MLA_ROPE_DIM // 2
    inv_a = ROPE_THETA ** (-jnp.arange(half_a, dtype=F32) / half_a)
    lane = np.arange(LANES)
    invf_d = inv_d[lane % half_d].reshape(1, LANES)
    sgn_d = jnp.asarray(np.where(lane < HEAD_LANES, -1.0, 1.0).reshape(1, LANES), F32)
    in_rope = (lane >= MLA_NOPE_DIM) & (lane < MLA_NOPE_DIM + MLA_ROPE_DIM)
    invf_a = jnp.where(jnp.asarray(in_rope), inv_a[(lane - MLA_NOPE_DIM) % half_a], 0.0).reshape(1, LANES)
    sgn_a = jnp.ones((1, LANES), F32)
    return invf_a, sgn_a, invf_d, sgn_d


def _dil_pair_perm():
    half = DIL_HEAD_DIM // 2
    cols = []
    for p in range(DIL_HEADS // 2):
        h0, h1 = 2 * p, 2 * p + 1
        for h, off in ((h0, 0), (h1, 0), (h0, half), (h1, half)):
            cols.extend(range(h * DIL_HEAD_DIM + off, h * DIL_HEAD_DIM + off + half))
    return np.asarray(cols)


def _layer_weights(w_in, b_forget, q_gain, kv_gain, w_uq, w_ukv):
    D = w_in.shape[0]
    cuts = np.cumsum([MLA_Q_RANK, MLA_KV_RANK, MLA_ROPE_DIM, 3 * FOX_HEADS * FOX_HEAD_DIM, FOX_HEADS])
    w_cq, w_ckv, w_kr, w_fox, w_ff, w_dil = jnp.split(w_in, cuts.tolist(), axis=1)
    half = MLA_ROPE_DIM // 2
    w_kr_rot = jnp.concatenate([-w_kr[:, half:], w_kr[:, :half]], axis=1)

    def place_rope(w):
        return jnp.pad(w, ((0, 0), (MLA_NOPE_DIM, LANES - MLA_NOPE_DIM - MLA_ROPE_DIM)))

    wa = jnp.concatenate([w_cq, w_ckv, place_rope(w_kr), place_rope(w_kr_rot),
                          jnp.pad(w_ff, ((0, 0), (0, LANES - FOX_HEADS)))], axis=1)

    nd = DIL_HEADS * DIL_HEAD_DIM
    perm = _dil_pair_perm()
    wdil = jnp.concatenate([w_dil[:, :nd][:, perm], w_dil[:, nd:2 * nd][:, perm], w_dil[:, 2 * nd:]], axis=1)

    dq = MLA_NOPE_DIM + MLA_ROPE_DIM
    uq = w_uq.reshape(MLA_Q_RANK, MLA_HEADS, dq)
    uq_rope = uq[:, :, MLA_NOPE_DIM:]
    uq_rot = jnp.concatenate([-uq_rope[:, :, half:], uq_rope[:, :, :half]], axis=2)
    pad_tail = LANES - dq
    uq_plain = jnp.pad(uq, ((0, 0), (0, 0), (0, pad_tail))).reshape(MLA_Q_RANK, MLA_HEADS * LANES)
    uq_rotp = jnp.pad(uq_rot, ((0, 0), (0, 0), (MLA_NOPE_DIM, pad_tail))).reshape(MLA_Q_RANK, MLA_HEADS * LANES)
    wuq = jnp.concatenate([uq_plain, uq_rotp], axis=1)

    ukv = w_ukv.reshape(MLA_KV_RANK, MLA_HEADS, MLA_NOPE_DIM + MLA_V_DIM)
    wuk = jnp.pad(ukv[:, :, :MLA_NOPE_DIM], ((0, 0), (0, 0), (0, LANES - MLA_NOPE_DIM)))
    wuk = wuk.reshape(MLA_KV_RANK, MLA_HEADS * LANES)
    wuv = ukv[:, :, MLA_NOPE_DIM:].reshape(MLA_KV_RANK, MLA_HEADS * MLA_V_DIM)

    bf = jnp.pad(b_forget.astype(F32), (0, LANES - FOX_HEADS)).reshape(1, LANES)
    nf = FOX_HEADS * FOX_HEAD_DIM
    return dict(wa=wa.astype(BF16), wfox=w_fox[:, :2 * nf].astype(BF16),
                wfoxv_t=w_fox[:, 2 * nf:].T.astype(BF16), wdil=wdil.astype(BF16),
                qg=q_gain.reshape(1, -1).astype(F32), kvg=kv_gain.reshape(1, -1).astype(F32),
                wuq=wuq.astype(BF16), wuk=wuk.astype(BF16), wuv_t=wuv.T.astype(BF16), bf=bf,
                sel=_fox_bias_placement())


def _fox_bias_placement():
    sel = np.zeros((3 * LANES, FOX_HEADS * LANES), np.float32)
    for j in range(3):
        for h in range(FOX_HEADS):
            sel[j * LANES + h, h * LANES + FOX_HEAD_DIM + j] = 1.0
    return jnp.asarray(sel, BF16)


def _tiles(S):
    tm = min(512, S)
    return dict(rope=min(512, S), proj=tm, flash=min(512, S), mix=tm, xattn=tm, moe=tm)


def kernel(x, mem, positions, w_in, b_forget, mla_q_gain, mla_kv_gain, mla_w_uq, mla_w_ukv, w_mix_out, ln_mix_g, ln_mix_b, xattn_w_q, xattn_w_kv, xattn_w_o, ln_mem_g, ln_mem_b, router_w, router_bias, expert_w_gate, expert_w_up, expert_w_down, ln_ffn_g, ln_ffn_b):
    B, S, D = x.shape
    depth = w_in.shape[0]
    t = _tiles(S)
    invf_a, sgn_a, invf_d, sgn_d = _rope_constants()
    cos_a, sin_a = _rope_tables(positions, invf_a, sgn_a, t["rope"])
    cos_d, sin_d = _rope_tables(positions, invf_d, sgn_d, t["rope"])
    tabs = (cos_a, sin_a, cos_d, sin_d)
    rw_t = router_w.T.astype(F32)
    rbias = router_bias.reshape(-1, 1).astype(F32)

    def row(v):
        return v.reshape(1, -1).astype(F32)

    for l in range(depth):
        wts = _layer_weights(w_in[l], b_forget[l], mla_q_gain[l], mla_kv_gain[l], mla_w_uq[l], mla_w_ukv[l])
        qa, ka, va_t, qb, kb, vb_t, qc, kc, vc = _proj(x, wts, tabs, t["proj"])
        oa_t = _flash_attention(qa, ka, va_t, MLA_HEADS, t["flash"], "mla_flash")
        ob_t = _flash_attention(qb, kb, vb_t, FOX_HEADS, t["flash"], "fox_flash")
        oc = _dilated_attention(qc, kc, vc)
        x = _mix_out(oa_t, ob_t, oc, x, w_mix_out[l].astype(BF16), row(ln_mix_g[l]), row(ln_mix_b[l]),
                     t["mix"])
        k_mem, v_mem = _mem_kv(mem, xattn_w_kv[l].astype(BF16))
        x = _xattn(x, k_mem, v_mem, xattn_w_q[l].astype(BF16), xattn_w_o[l].astype(BF16),
                   row(ln_mem_g[l]), row(ln_mem_b[l]), t["xattn"])
        x2d = _moe(x.reshape(B * S, D), rw_t, rbias, expert_w_gate[l].astype(BF16),
                   expert_w_up[l].astype(BF16), expert_w_down[l].astype(BF16),
                   row(ln_ffn_g[l]), row(ln_ffn_b[l]), t["moe"])
        x = x2d.reshape(B, S, D)
    return x
```

```python
import functools
import math

import numpy as np
import jax
import jax.numpy as jnp
from jax import lax
from jax.experimental import pallas as pl
from jax.experimental.pallas import tpu as pltpu

F32 = jnp.float32
BF16 = jnp.bfloat16
HIGHEST = lax.Precision.HIGHEST

D_MODEL = 1024
DEPTH = 4
MLA_HEADS = 4
MLA_Q_RANK = 256
MLA_KV_RANK = 128
MLA_NOPE_DIM = 64
MLA_ROPE_DIM = 32
MLA_V_DIM = 64
FOX_HEADS = 6
FOX_HEAD_DIM = 64
DIL_HEADS = 6
DIL_HEAD_DIM = 64
DILATED_PAIRS = ((128, 1), (512, 4), (2048, 16))
XATTN_HEADS = 4
N_EXPERTS = 16
N_GROUPS = 4
EXPERTS_PER_GROUP = N_EXPERTS // N_GROUPS
D_EXPERT = 256
ROPE_THETA = 10000.0
NORM_EPS = 1e-5
NEG_INF = -1e30
DEEPNORM_ALPHA = (2 * DEPTH) ** 0.25
LOG2E = math.log2(math.e)

LANES = 128
HEAD_LANES = 64
VMEM_LIMIT_BYTES = 52 * 1024 * 1024

A_CQ = 0
A_CKV = A_CQ + MLA_Q_RANK
A_KR = A_CKV + MLA_KV_RANK
A_KRR = A_KR + LANES
A_FF = A_KRR + LANES
A_COLS = A_FF + LANES


def _cparams(sem):
    return pltpu.CompilerParams(dimension_semantics=sem, vmem_limit_bytes=VMEM_LIMIT_BYTES)


def _nt_dot(a, b, precision=None):
    return lax.dot_general(a, b, (((1,), (1,)), ((), ())), precision=precision,
                           preferred_element_type=F32)


def _dot(a, b, precision=None):
    return jnp.dot(a, b, precision=precision, preferred_element_type=F32)


def _layer_norm(y, g, b):
    mu = jnp.mean(y, axis=-1, keepdims=True)
    yc = y - mu
    var = jnp.mean(yc * yc, axis=-1, keepdims=True)
    return yc * lax.rsqrt(var + NORM_EPS) * g + b


def _rms_norm(y, g):
    ms = jnp.mean(y * y, axis=-1, keepdims=True)
    return y * lax.rsqrt(ms + NORM_EPS) * g


def _rope_table_kernel(pos_ref, invf_ref, sgn_ref, cos_ref, sin_ref):
    ang = pos_ref[0].astype(F32) * invf_ref[...]
    cos_ref[0] = jnp.cos(ang)
    sin_ref[0] = jnp.sin(ang) * sgn_ref[...]


def _rope_tables(positions, invf, sgn, ts):
    B, S = positions.shape
    pos3 = positions.reshape(B, S, 1)
    spec = pl.BlockSpec((1, ts, LANES), lambda b, i: (b, i, 0))
    vec = pl.BlockSpec((1, LANES), lambda b, i: (0, 0))
    return pl.pallas_call(
        _rope_table_kernel,
        out_shape=(jax.ShapeDtypeStruct((B, S, LANES), F32),) * 2,
        grid=(B, S // ts),
        in_specs=[pl.BlockSpec((1, ts, 1), lambda b, i: (b, i, 0)), vec, vec],
        out_specs=(spec, spec),
        compiler_params=_cparams(("parallel", "parallel")),
        name="rope_tables",
    )(pos3, invf, sgn)


def _proj_kernel(x_ref, wa_ref, wfox_ref, wfoxv_ref, wdil_ref, qg_ref, kvg_ref, wuq_ref, wuk_ref,
                 wuv_ref, bf_ref, sel_ref, cosa_ref, sina_ref, cosd_ref, sind_ref,
                 qa_ref, ka_ref, va_ref, qb_ref, kb_ref, vb_ref,
                 qc_ref, kc_ref, vc_ref, carry_c, *dil_scs, tm):
    i = pl.program_id(1)
    xb = x_ref[0].astype(BF16)

    ha = _dot(xb, wa_ref[...])
    cqn = _rms_norm(ha[:, A_CQ:A_CQ + MLA_Q_RANK], qg_ref[...]).astype(BF16)
    ckvn = _rms_norm(ha[:, A_CKV:A_CKV + MLA_KV_RANK], kvg_ref[...]).astype(BF16)
    cos_a = cosa_ref[0]
    sin_a = sina_ref[0]
    q2 = _dot(cqn, wuq_ref[...])
    k_nope = _dot(ckvn, wuk_ref[...])
    k_rot = ha[:, A_KR:A_KR + LANES] * cos_a + ha[:, A_KRR:A_KRR + LANES] * sin_a
    q_scale = (MLA_NOPE_DIM + MLA_ROPE_DIM) ** -0.5 * LOG2E
    n_half = MLA_HEADS * LANES
    for h in range(MLA_HEADS):
        sl = slice(h * LANES, (h + 1) * LANES)
        qh = q2[:, sl] * cos_a + q2[:, n_half + h * LANES:n_half + (h + 1) * LANES] * sin_a
        qa_ref[0, :, sl] = (qh * q_scale).astype(BF16)
        ka_ref[0, :, sl] = (k_nope[:, sl] + k_rot).astype(BF16)
    va_ref[0] = _nt_dot(wuv_ref[...], ckvn).astype(BF16)

    z = ha[:, A_FF:A_FF + LANES] + bf_ref[...]
    logf = jnp.minimum(z, 0.0) - jnp.log(1.0 + jnp.exp(-jnp.abs(z)))
    lane = lax.broadcasted_iota(jnp.int32, (1, LANES), 1)
    logf = jnp.where(lane < FOX_HEADS, logf, 0.0)

    @pl.when(i == 0)
    def _():
        carry_c[...] = jnp.zeros_like(carry_c)

    r = lax.broadcasted_iota(jnp.int32, (tm, tm), 0)
    c = lax.broadcasted_iota(jnp.int32, (tm, tm), 1)
    lower = (c <= r).astype(F32)
    fcol = _dot(lower, logf, HIGHEST) + carry_c[...]
    carry_c[...] = fcol[tm - 1:tm, :]

    nb = fcol * (-LOG2E)
    hi = nb.astype(BF16)
    r1 = nb - hi.astype(F32)
    mid = r1.astype(BF16)
    lo3 = (r1 - mid.astype(F32)).astype(BF16)
    placed = _dot(jnp.concatenate([hi, mid, lo3], axis=1), sel_ref[...])

    hf = _dot(xb, wfox_ref[...])
    nf = FOX_HEADS * FOX_HEAD_DIM
    dims = lane < FOX_HEAD_DIM
    ones3 = jnp.where((lane >= FOX_HEAD_DIM) & (lane < FOX_HEAD_DIM + 3), 1.0, 0.0)
    fq_scale = FOX_HEAD_DIM ** -0.5 * LOG2E
    for h in range(FOX_HEADS):
        blk = slice((h // 2) * LANES, (h // 2 + 1) * LANES)
        out = slice(h * LANES, (h + 1) * LANES)
        qh = hf[:, blk]
        kh = hf[:, nf + (h // 2) * LANES:nf + (h // 2 + 1) * LANES]
        if h % 2:
            qh = pltpu.roll(qh, HEAD_LANES, 1)
            kh = pltpu.roll(kh, HEAD_LANES, 1)
        qb_ref[0, :, out] = jnp.where(dims, qh * fq_scale, ones3).astype(BF16)
        kb_ref[0, :, out] = jnp.where(dims, kh, placed[:, out]).astype(BF16)
    vb_ref[0] = _nt_dot(wfoxv_ref[...], xb).astype(BF16)

    hd = _dot(xb, wdil_ref[...])
    cos_d = cosd_ref[0]
    sin_d = sind_ref[0]
    nd = DIL_HEADS * DIL_HEAD_DIM
    for pblk in range(nd // LANES):
        sl = slice(pblk * LANES, (pblk + 1) * LANES)
        qh = hd[:, sl]
        kh = hd[:, nd + pblk * LANES:nd + (pblk + 1) * LANES]
        qh = qh * cos_d + pltpu.roll(qh, HEAD_LANES, 1) * sin_d
        kh = kh * cos_d + pltpu.roll(kh, HEAD_LANES, 1) * sin_d
        dil_scs[pblk][...] = qh * (DIL_HEAD_DIM ** -0.5 * LOG2E)
        dil_scs[3 + pblk][...] = kh
        dil_scs[6 + pblk][...] = hd[:, 2 * nd + pblk * LANES:2 * nd + (pblk + 1) * LANES]
    for a, ref in enumerate((qc_ref, kc_ref, vc_ref)):
        for r in range(DIL_PLANES):
            rows = pl.ds(r, tm // DIL_PLANES, stride=DIL_PLANES)
            for pblk in range(nd // LANES):
                lanes = slice(r * nd + pblk * LANES, r * nd + (pblk + 1) * LANES)
                ref[0, :, lanes] = dil_scs[3 * a + pblk][rows, :].astype(BF16)


def _proj(x, wts, tabs, tm):
    B, S, D = x.shape
    cos_a, sin_a, cos_d, sin_d = tabs
    grid = (B, S // tm)

    def full(a):
        return pl.BlockSpec(a.shape, lambda b, i: (0,) * a.ndim)

    def tok(width, dtype):
        return (jax.ShapeDtypeStruct((B, S, width), dtype),
                pl.BlockSpec((1, tm, width), lambda b, i: (b, i, 0)))

    tab = pl.BlockSpec((1, tm, LANES), lambda b, i: (b, i, 0))
    nd = DIL_HEADS * DIL_HEAD_DIM

    def planes(dtype):
        return (jax.ShapeDtypeStruct((B, S // DIL_PLANES, DIL_PLANES * nd), dtype),
                pl.BlockSpec((1, tm // DIL_PLANES, DIL_PLANES * nd), lambda b, i: (b, i, 0)))

    def tok_t(height, dtype):
        return (jax.ShapeDtypeStruct((B, height, S), dtype),
                pl.BlockSpec((1, height, tm), lambda b, i: (b, 0, i)))

    outs = [tok(MLA_HEADS * LANES, BF16), tok(MLA_HEADS * LANES, BF16),
            tok_t(MLA_HEADS * MLA_V_DIM, BF16),
            tok(FOX_HEADS * LANES, BF16), tok(FOX_HEADS * LANES, BF16),
            tok_t(FOX_HEADS * FOX_HEAD_DIM, BF16),
            planes(BF16), planes(BF16), planes(BF16)]
    w_list = [wts["wa"], wts["wfox"], wts["wfoxv_t"], wts["wdil"], wts["qg"], wts["kvg"], wts["wuq"],
              wts["wuk"], wts["wuv_t"], wts["bf"], wts["sel"]]
    return pl.pallas_call(
        functools.partial(_proj_kernel, tm=tm),
        out_shape=tuple(o[0] for o in outs),
        grid=grid,
        in_specs=[pl.BlockSpec((1, tm, D), lambda b, i: (b, i, 0))] + [full(w) for w in w_list]
        + [tab, tab, tab, tab],
        out_specs=tuple(o[1] for o in outs),
        scratch_shapes=[pltpu.VMEM((1, LANES), F32)] + [pltpu.VMEM((tm, LANES), F32)] * (3 * nd // LANES),
        compiler_params=_cparams(("parallel", "arbitrary")),
        name="proj",
    )(x, *w_list, cos_a, sin_a, cos_d, sin_d)


FLASH_CHUNK = 32


FLASH_KEY_SPLIT = 2
ONES_ROWS = 16


def _flash_kernel(q0_ref, q1_ref, k0_ref, k1_ref, vt_ref, o_ref, *scratch, tq):
    i = pl.program_id(2)
    q_refs = (q0_ref, q1_ref)
    k_refs = (k0_ref, k1_ref)
    n_chain = 2 * FLASH_KEY_SPLIT
    s_scs = scratch[:n_chain]
    p_scs = scratch[n_chain:2 * n_chain]
    m_sc, l_sc, acc0_sc, acc1_sc = scratch[2 * n_chain:]
    acc_scs = (acc0_sc, acc1_sc)
    dv = HEAD_LANES
    tk = tq // FLASH_KEY_SPLIT
    n_chunks = tk // FLASH_CHUNK

    m_sc[...] = jnp.full_like(m_sc, NEG_INF)
    l_sc[...] = jnp.zeros_like(l_sc)
    acc0_sc[...] = jnp.zeros_like(acc0_sc)
    acc1_sc[...] = jnp.zeros_like(acc1_sc)

    def step(j, masked):
        m_locs = []
        for h in range(2):
            for c in range(FLASH_KEY_SPLIT):
                ks = pl.multiple_of(j * tq + c * tk, tk)
                k = k_refs[h][0, pl.ds(ks, tk), :]
                st = _nt_dot(k, q_refs[h][0])
                if masked:
                    key = lax.broadcasted_iota(jnp.int32, (tk, tq), 0) + c * tk
                    qry = lax.broadcasted_iota(jnp.int32, (tk, tq), 1)
                    st = jnp.where(key <= qry, st, NEG_INF)
                s_scs[h * FLASH_KEY_SPLIT + c][...] = st
                m8 = jnp.max(st.reshape(tk // 8, 8, tq), axis=0)
                m_locs.append(jnp.max(m8, axis=0, keepdims=True))
        locs = []
        for h in range(2):
            for c in range(FLASH_KEY_SPLIT):
                ks = pl.multiple_of(j * tq + c * tk, tk)
                s_sc = s_scs[h * FLASH_KEY_SPLIT + c]
                p_sc = p_scs[h * FLASH_KEY_SPLIT + c]
                m_loc = m_locs[h * FLASH_KEY_SPLIT + c]
                for r in range(n_chunks):
                    rows = slice(r * FLASH_CHUNK, (r + 1) * FLASH_CHUNK)
                    p_sc[rows, :] = jnp.exp2(s_sc[rows, :] - m_loc).astype(BF16)
                vt = vt_ref[0, h * dv:(h + 1) * dv, pl.ds(ks, tk)]
                vt_ext = jnp.concatenate([vt, jnp.ones((ONES_ROWS, tk), BF16)], axis=0)
                pvl = _dot(vt_ext, p_sc[...])
                locs.append((m_loc, pvl))
        for h in range(2):
            mine = locs[h * FLASH_KEY_SPLIT:(h + 1) * FLASH_KEY_SPLIT]
            m_prev = m_sc[h]
            m_new = m_prev
            for m_loc, _ in mine:
                m_new = jnp.maximum(m_new, m_loc)
            a_prev = jnp.exp2(m_prev - m_new)
            acc = acc_scs[h][...] * a_prev
            l = l_sc[h] * a_prev
            for m_loc, pvl in mine:
                a_loc = jnp.exp2(m_loc - m_new)
                acc = acc + pvl[:dv] * a_loc
                l = l + pvl[dv:dv + 1] * a_loc
            acc_scs[h][...] = acc
            l_sc[h] = l
            m_sc[h] = m_new

    def body(j, carry):
        step(j, False)
        return carry

    lax.fori_loop(0, i, body, 0)
    step(i, True)
    for h in range(2):
        o_ref[0, h * dv:(h + 1) * dv, :] = (acc_scs[h][...] / l_sc[h]).astype(o_ref.dtype)


def _flash_attention(q, k, vt, n_heads, tq, name):
    B, S, _ = q.shape
    n_pairs = n_heads // 2
    dv2 = 2 * HEAD_LANES
    qspec0 = pl.BlockSpec((1, tq, LANES), lambda b, p, i: (b, i, 2 * p))
    qspec1 = pl.BlockSpec((1, tq, LANES), lambda b, p, i: (b, i, 2 * p + 1))
    kspec0 = pl.BlockSpec((1, S, LANES), lambda b, p, i: (b, 0, 2 * p))
    kspec1 = pl.BlockSpec((1, S, LANES), lambda b, p, i: (b, 0, 2 * p + 1))
    vspec = pl.BlockSpec((1, dv2, S), lambda b, p, i: (b, p, 0))
    return pl.pallas_call(
        functools.partial(_flash_kernel, tq=tq),
        out_shape=jax.ShapeDtypeStruct((B, n_heads * HEAD_LANES, S), BF16),
        grid=(B, n_pairs, S // tq),
        in_specs=[qspec0, qspec1, kspec0, kspec1, vspec],
        out_specs=pl.BlockSpec((1, dv2, tq), lambda b, p, i: (b, p, i)),
        scratch_shapes=[pltpu.VMEM((tq // FLASH_KEY_SPLIT, tq), F32)] * (2 * FLASH_KEY_SPLIT)
        + [pltpu.VMEM((tq // FLASH_KEY_SPLIT, tq), BF16)] * (2 * FLASH_KEY_SPLIT)
        + [pltpu.VMEM((2, 1, tq), F32), pltpu.VMEM((2, 1, tq), F32),
                        pltpu.VMEM((HEAD_LANES, tq), F32), pltpu.VMEM((HEAD_LANES, tq), F32)],
        compiler_params=_cparams(("parallel", "parallel", "arbitrary")),
        name=name,
    )(q, q, k, k, vt)


DIL_PLANES = 16
DIL_TQ = 128
DIL_UNROLL = 4


def _plane_slot(r16):
    return 4 * (r16 % 4) + r16 // 4


def _dilated_geometry(dil):
    planes = DIL_PLANES // dil
    rows = DIL_TQ // planes
    if dil == 1:
        order = [(c // 4) + 4 * (c % 4) for c in range(planes)]
    else:
        order = list(range(planes))
    return planes, rows, order


def _dilated_bias(dil, first):
    planes, rows, order = _dilated_geometry(dil)
    krows = 2 * rows
    a = np.arange(planes * rows)
    b = np.arange(planes * krows)
    uq = np.asarray(order)[a // rows] + planes * (a % rows + (0 if first else rows))
    uk = np.asarray(order)[b // krows] + planes * (b % krows)
    diff = uq[:, None] - uk[None, :]
    valid = (diff >= 0) & (diff <= DIL_TQ)
    return jnp.asarray(np.where(valid, 0.0, NEG_INF), F32)


def _dilated_kernel(q_ref, k_ref, v_ref, b1f_ref, b1r_ref, b4f_ref, b4r_ref, b16f_ref, b16r_ref, o_ref,
                    q_st, k_st, v_st, m_sc, l_sc, acc_sc, *, rows_per_plane):
    pair = pl.program_id(1)
    n_pairs = DIL_HEADS // 2
    lane = lax.broadcasted_iota(jnp.int32, (1, LANES), 1)
    lo = lane < HEAD_LANES
    q_lo = (lane % HEAD_LANES) < (HEAD_LANES // 2)
    tq = DIL_TQ

    def plane_lanes(r16):
        return pl.ds(pl.multiple_of((r16 * n_pairs + pair) * LANES, LANES), LANES)

    for r16 in range(DIL_PLANES):
        slot = _plane_slot(r16)
        q_st[slot] = q_ref[0, :, plane_lanes(r16)].astype(F32)
        k_st[slot] = k_ref[0, :, plane_lanes(r16)].astype(F32)
        v_st[slot] = v_ref[0, :, plane_lanes(r16)].astype(F32)
    m_sc[...] = jnp.full_like(m_sc, NEG_INF)
    l_sc[...] = jnp.zeros_like(l_sc)
    acc_sc[...] = jnp.zeros_like(acc_sc)

    def block(planes, rows, slot0, row0, first, bias_ref):
        qsl = (pl.ds(slot0, planes), pl.ds(row0, rows), slice(None))
        k_row0 = row0 if first else row0 - rows
        ksl = (pl.ds(slot0, planes), pl.ds(k_row0, 2 * rows), slice(None))
        q = q_st[qsl].reshape(tq, LANES)
        k = k_st[ksl].reshape(-1, LANES).astype(BF16)
        v = v_st[ksl].reshape(-1, LANES).astype(BF16)
        zero = jnp.zeros_like(q)
        qs = (jnp.where(q_lo, q, zero).astype(BF16), jnp.where(q_lo, zero, q).astype(BF16))
        bias = bias_ref[...]
        v_ext = jnp.concatenate([v, jnp.ones_like(v)], axis=1)
        maxes, sums, pvs = [], [], []
        for h in range(2):
            s = _nt_dot(qs[h], k) + bias
            mh = jnp.max(s, axis=1, keepdims=True)
            p = jnp.exp2(s - mh)
            maxes.append(mh)
            pvl = _dot(p.astype(BF16), v_ext)
            pvs.append(pvl[:, :LANES])
            sums.append(pvl[:, LANES:])
        m_blk = jnp.where(lo, maxes[0], maxes[1])
        m_old = m_sc[qsl].reshape(tq, LANES)
        m_new = jnp.maximum(m_old, m_blk)
        a_old = jnp.exp2(m_old - m_new)
        a_blk = jnp.exp2(m_blk - m_new)
        l_new = a_old * l_sc[qsl].reshape(tq, LANES) + a_blk * jnp.where(lo, sums[0], sums[1])
        acc_new = a_old * acc_sc[qsl].reshape(tq, LANES) + a_blk * jnp.where(lo, pvs[0], pvs[1])
        m_sc[qsl] = m_new.reshape(planes, rows, LANES)
        l_sc[qsl] = l_new.reshape(planes, rows, LANES)
        acc_sc[qsl] = acc_new.reshape(planes, rows, LANES)

    bias_refs = {1: (b1f_ref, b1r_ref), 4: (b4f_ref, b4r_ref), 16: (b16f_ref, b16r_ref)}
    for window, dil in DILATED_PAIRS:
        assert window // dil == tq
        planes, rows, _ = _dilated_geometry(dil)
        nblk = rows_per_plane // rows
        bf_ref, br_ref = bias_refs[dil]

        def first_body(sub, carry, planes=planes, rows=rows, bf_ref=bf_ref):
            block(planes, rows, sub * planes, 0, True, bf_ref)
            return carry

        lax.fori_loop(0, dil, first_body, 0, unroll=min(dil, DIL_UNROLL))

        if nblk > 1:
            def rest_body(t, carry, planes=planes, rows=rows, nblk=nblk, br_ref=br_ref):
                sub = t // (nblk - 1)
                blk = t % (nblk - 1) + 1
                block(planes, rows, sub * planes, pl.multiple_of(blk * rows, rows), False, br_ref)
                return carry

            lax.fori_loop(0, dil * (nblk - 1), rest_body, 0, unroll=DIL_UNROLL)

    for r16 in range(DIL_PLANES):
        slot = _plane_slot(r16)
        o_ref[0, :, plane_lanes(r16)] = (acc_sc[slot] / l_sc[slot]).astype(o_ref.dtype)


def _dilated_attention(qc, kc, vc):
    B, rpp, width = qc.shape
    n_pairs = DIL_HEADS // 2
    assert rpp % DIL_TQ == 0 and rpp >= 2 * DIL_TQ

    spec = pl.BlockSpec((1, rpp, width), lambda b, p: (b, 0, 0))
    biases = [_dilated_bias(d, f) for _, d in DILATED_PAIRS for f in (True, False)]
    bspecs = [pl.BlockSpec(bb.shape, lambda b, p: (0, 0)) for bb in biases]
    plane = pltpu.VMEM((DIL_PLANES, rpp, LANES), F32)
    return pl.pallas_call(
        functools.partial(_dilated_kernel, rows_per_plane=rpp),
        out_shape=jax.ShapeDtypeStruct((B, rpp, width), BF16),
        grid=(B, n_pairs),
        in_specs=[spec, spec, spec] + bspecs,
        out_specs=spec,
        scratch_shapes=[plane] * 6,
        compiler_params=_cparams(("parallel", "arbitrary")),
        name="dilated",
    )(qc, kc, vc, *biases)


def _tn_dot(a_t, b):
    return lax.dot_general(a_t, b, (((0,), (0,)), ((), ())), preferred_element_type=F32)


def _mix_out_kernel(oat_ref, obt_ref, oc_ref, x_ref, wa_ref, wb_ref, wc_ref, g_ref, b_ref, o_ref, *oc_scs):
    nd = len(oc_scs) * LANES
    rows_per_plane = oc_scs[0].shape[0] // DIL_PLANES
    for r in range(DIL_PLANES):
        rows = pl.ds(r, rows_per_plane, stride=DIL_PLANES)
        for pblk, sc in enumerate(oc_scs):
            lanes = slice(r * nd + pblk * LANES, r * nd + (pblk + 1) * LANES)
            sc[rows, :] = oc_ref[0, :, lanes].astype(F32)
    oc = jnp.concatenate([sc[...] for sc in oc_scs], axis=1).astype(BF16)
    y = (_tn_dot(oat_ref[0], wa_ref[...]) + _tn_dot(obt_ref[0], wb_ref[...])
         + _dot(oc, wc_ref[...]))
    o_ref[0] = _layer_norm(DEEPNORM_ALPHA * x_ref[0] + y, g_ref[...], b_ref[...])


def _mix_out(oa_t, ob_t, oc, x, w_out, g, b, tm):
    B, S, D = x.shape
    na, nb = oa_t.shape[1], ob_t.shape[1]
    wa, wb, wc = w_out[:na], w_out[na:na + nb], w_out[na + nb:]

    def cols(a):
        return pl.BlockSpec((1, a.shape[1], tm), lambda bb, i: (bb, 0, i))

    def rows(a):
        return pl.BlockSpec((1, tm, a.shape[2]), lambda bb, i: (bb, i, 0))

    def full(a):
        return pl.BlockSpec(a.shape, lambda bb, i: (0, 0))

    return pl.pallas_call(
        _mix_out_kernel,
        out_shape=jax.ShapeDtypeStruct((B, S, D), F32),
        grid=(B, S // tm),
        in_specs=[cols(oa_t), cols(ob_t),
                  pl.BlockSpec((1, tm // DIL_PLANES, oc.shape[2]), lambda bb, i: (bb, i, 0)),
                  rows(x), full(wa), full(wb), full(wc), full(g), full(b)],
        out_specs=rows(x),
        scratch_shapes=[pltpu.VMEM((tm, LANES), F32)] * (oc.shape[2] // DIL_PLANES // LANES),
        compiler_params=_cparams(("parallel", "parallel")),
        name="mix_out",
    )(oa_t, ob_t, oc, x, wa, wb, wc, g, b)


def _mem_kv_kernel(mem_ref, w_ref, k_ref, v_ref):
    kv = _dot(mem_ref[0].astype(BF16), w_ref[...])
    d = k_ref.shape[-1]
    k_ref[0] = kv[:, :d].astype(BF16)
    v_ref[0] = kv[:, d:].astype(BF16)


def _mem_kv(mem, w_kv):
    B, M, D = mem.shape
    spec = pl.BlockSpec((1, M, D), lambda b: (b, 0, 0))
    return pl.pallas_call(
        _mem_kv_kernel,
        out_shape=(jax.ShapeDtypeStruct((B, M, D), BF16),) * 2,
        grid=(B,),
        in_specs=[spec, pl.BlockSpec(w_kv.shape, lambda b: (0, 0))],
        out_specs=(spec, spec),
        compiler_params=_cparams(("parallel",)),
        name="mem_kv",
    )(mem, w_kv)


def _xattn_kernel(x_ref, k_ref, v_ref, wq_ref, wo_ref, g_ref, b_ref, o_ref):
    x = x_ref[0]
    d = x.shape[-1]
    dh = d // XATTN_HEADS
    q = (_dot(x.astype(BF16), wq_ref[...]) * (dh ** -0.5)).astype(BF16)
    outs = []
    for h in range(XATTN_HEADS):
        sl = slice(h * dh, (h + 1) * dh)
        s = _nt_dot(q[:, sl], k_ref[0, :, sl])
        p = jnp.exp(s - jnp.max(s, axis=1, keepdims=True))
        o = _dot(p.astype(BF16), v_ref[0, :, sl])
        outs.append((o / jnp.sum(p, axis=1, keepdims=True)).astype(BF16))
    o_all = jnp.concatenate(outs, axis=1)
    y = _dot(o_all, wo_ref[...])
    o_ref[0] = _layer_norm(DEEPNORM_ALPHA * x + y, g_ref[...], b_ref[...])


def _xattn(x, k_mem, v_mem, w_q, w_o, g, b, tm):
    B, S, D = x.shape
    M = k_mem.shape[1]
    xspec = pl.BlockSpec((1, tm, D), lambda bb, i: (bb, i, 0))
    mspec = pl.BlockSpec((1, M, D), lambda bb, i: (bb, 0, 0))

    def full(a):
        return pl.BlockSpec(a.shape, lambda bb, i: (0, 0))

    return pl.pallas_call(
        _xattn_kernel,
        out_shape=jax.ShapeDtypeStruct((B, S, D), F32),
        grid=(B, S // tm),
        in_specs=[xspec, mspec, mspec, full(w_q), full(w_o), full(g), full(b)],
        out_specs=xspec,
        compiler_params=_cparams(("parallel", "parallel")),
        name="xattn",
    )(x, k_mem, v_mem, w_q, w_o, g, b)


def _router_gates_t(x, rw_t, rbias):
    logits_t = _nt_dot(rw_t, x, HIGHEST)
    scores = 1.0 / (1.0 + jnp.exp(-logits_t))
    biased = scores + rbias
    sc = [scores[e:e + 1, :] for e in range(N_EXPERTS)]
    bs = [biased[e:e + 1, :] for e in range(N_EXPERTS)]
    epg = EXPERTS_PER_GROUP

    def beats(a, b, a_first):
        return (a >= b) if a_first else (a > b)

    grp = []
    for gi in range(N_GROUPS):
        v = bs[gi * epg:(gi + 1) * epg]
        best = None
        for a in range(epg):
            for b in range(a + 1, epg):
                pair = v[a] + v[b]
                best = pair if best is None else jnp.maximum(best, pair)
        grp.append(best)
    gates = []
    for gi in range(N_GROUPS):
        g_sel = None
        for gj in range(N_GROUPS):
            if gj == gi:
                continue
            w = beats(grp[gi], grp[gj], gi < gj)
            g_sel = w if g_sel is None else (g_sel & w)
        in_top = []
        for a in range(epg):
            ea = gi * epg + a
            n_above = jnp.zeros(bs[ea].shape, jnp.int32)
            for b in range(epg):
                if b == a:
                    continue
                eb = gi * epg + b
                n_above = n_above + beats(bs[eb], bs[ea], b < a).astype(jnp.int32)
            in_top.append(g_sel & (n_above < 2))
        denom = None
        for a in range(epg):
            term = jnp.where(in_top[a], sc[gi * epg + a], 0.0)
            denom = term if denom is None else denom + term
        denom = jnp.where(g_sel, denom, 1.0)
        for a in range(epg):
            gates.append(jnp.where(in_top[a], sc[gi * epg + a] / denom, 0.0))
    return jnp.concatenate(gates, axis=0)


def _moe_kernel(x_ref, rwt_ref, rb_ref, wg_ref, wu_ref, wd_ref, g_ref, b_ref, o_ref,
                gates_sc, acc_sc, xb_sc, *, tm):
    grp = pl.program_id(1)
    epg = EXPERTS_PER_GROUP

    @pl.when(grp == 0)
    def _():
        ts = min(tm, ROUTER_ROWS)
        eye = (lax.broadcasted_iota(jnp.int32, (ts, ts), 0)
               == lax.broadcasted_iota(jnp.int32, (ts, ts), 1)).astype(F32)
        for r in range(tm // ts):
            rows = slice(r * ts, (r + 1) * ts)
            x = x_ref[rows, :]
            gates_t = _router_gates_t(x, rwt_ref[...], rb_ref[...])
            gates_sc[rows, :] = _nt_dot(eye, gates_t, HIGHEST)
            xb_sc[rows, :] = x.astype(BF16)

    xb = xb_sc[...]
    lane_e = lax.broadcasted_iota(jnp.int32, (1, N_EXPERTS), 1)
    gates = gates_sc[...]
    acts = []
    for e in range(epg):
        hg = _dot(xb, wg_ref[e])
        hu = _dot(xb, wu_ref[e])
        gate = jnp.sum(jnp.where(lane_e == grp * epg + e, gates, 0.0), axis=1, keepdims=True)
        acts.append(((hg / (1.0 + jnp.exp(-hg))) * hu * gate).astype(BF16))
    part = _dot(jnp.concatenate(acts, axis=1), wd_ref[...].reshape(epg * D_EXPERT, -1))

    @pl.when(grp == 0)
    def _():
        acc_sc[...] = part

    @pl.when(grp > 0)
    def _():
        acc_sc[...] += part

    @pl.when(grp == N_GROUPS - 1)
    def _():
        o_ref[...] = _layer_norm(DEEPNORM_ALPHA * x_ref[...] + acc_sc[...], g_ref[...], b_ref[...])


ROUTER_ROWS = 512


def _moe(x2d, rw_t, rbias, wg, wu, wd, g, b, tm):
    T, D = x2d.shape
    xspec = pl.BlockSpec((tm, D), lambda i, e: (i, 0))
    epg = EXPERTS_PER_GROUP

    def full(a):
        return pl.BlockSpec(a.shape, lambda i, e: (0, 0))

    return pl.pallas_call(
        functools.partial(_moe_kernel, tm=tm),
        out_shape=jax.ShapeDtypeStruct((T, D), F32),
        grid=(T // tm, N_GROUPS),
        in_specs=[xspec, full(rw_t), full(rbias),
                  pl.BlockSpec((epg, D, D_EXPERT), lambda i, e: (e, 0, 0)),
                  pl.BlockSpec((epg, D, D_EXPERT), lambda i, e: (e, 0, 0)),
                  pl.BlockSpec((epg, D_EXPERT, D), lambda i, e: (e, 0, 0)),
                  full(g), full(b)],
        out_specs=xspec,
        scratch_shapes=[pltpu.VMEM((tm, N_EXPERTS), F32), pltpu.VMEM((tm, D), F32),
                        pltpu.VMEM((tm, D), BF16)],
        compiler_params=_cparams(("parallel", "arbitrary")),
        name="moe",
    )(x2d, rw_t, rbias, wg, wu, wd, g, b)


def _rope_constants():
    half_d = DIL_HEAD_DIM // 2
    inv_d = ROPE_THETA ** (-jnp.arange(half_d, dtype=F32) / half_d)
    half_a = MLA_ROPE_DIM // 2
    inv_a = ROPE_THETA ** (-jnp.arange(half_a, dtype=F32) / half_a)
    lane = np.arange(LANES)
    invf_d = inv_d[lane % half_d].reshape(1, LANES)
    sgn_d = jnp.asarray(np.where(lane < HEAD_LANES, -1.0, 1.0).reshape(1, LANES), F32)
    in_rope = (lane >= MLA_NOPE_DIM) & (lane < MLA_NOPE_DIM + MLA_ROPE_DIM)
    invf_a = jnp.where(jnp.asarray(in_rope), inv_a[(lane - MLA_NOPE_DIM) % half_a], 0.0).reshape(1, LANES)
    sgn_a = jnp.ones((1, LANES), F32)
    return invf_a, sgn_a, invf_d, sgn_d


def _dil_pair_perm():
    half = DIL_HEAD_DIM // 2
    cols = []
    for p in range(DIL_HEADS // 2):
        h0, h1 = 2 * p, 2 * p + 1
        for h, off in ((h0, 0), (h1, 0), (h0, half), (h1, half)):
            cols.extend(range(h * DIL_HEAD_DIM + off, h * DIL_HEAD_DIM + off + half))
    return np.asarray(cols)


def _layer_weights(w_in, b_forget, q_gain, kv_gain, w_uq, w_ukv):
    D = w_in.shape[0]
    cuts = np.cumsum([MLA_Q_RANK, MLA_KV_RANK, MLA_ROPE_DIM, 3 * FOX_HEADS * FOX_HEAD_DIM, FOX_HEADS])
    w_cq, w_ckv, w_kr, w_fox, w_ff, w_dil = jnp.split(w_in, cuts.tolist(), axis=1)
    half = MLA_ROPE_DIM // 2
    w_kr_rot = jnp.concatenate([-w_kr[:, half:], w_kr[:, :half]], axis=1)

    def place_rope(w):
        return jnp.pad(w, ((0, 0), (MLA_NOPE_DIM, LANES - MLA_NOPE_DIM - MLA_ROPE_DIM)))

    wa = jnp.concatenate([w_cq, w_ckv, place_rope(w_kr), place_rope(w_kr_rot),
                          jnp.pad(w_ff, ((0, 0), (0, LANES - FOX_HEADS)))], axis=1)

    nd = DIL_HEADS * DIL_HEAD_DIM
    perm = _dil_pair_perm()
    wdil = jnp.concatenate([w_dil[:, :nd][:, perm], w_dil[:, nd:2 * nd][:, perm], w_dil[:, 2 * nd:]], axis=1)

    dq = MLA_NOPE_DIM + MLA_ROPE_DIM
    uq = w_uq.reshape(MLA_Q_RANK, MLA_HEADS, dq)
    uq_rope = uq[:, :, MLA_NOPE_DIM:]
    uq_rot = jnp.concatenate([-uq_rope[:, :, half:], uq_rope[:, :, :half]], axis=2)
    pad_tail = LANES - dq
    uq_plain = jnp.pad(uq, ((0, 0), (0, 0), (0, pad_tail))).reshape(MLA_Q_RANK, MLA_HEADS * LANES)
    uq_rotp = jnp.pad(uq_rot, ((0, 0), (0, 0), (MLA_NOPE_DIM, pad_tail))).reshape(MLA_Q_RANK, MLA_HEADS * LANES)
    wuq = jnp.concatenate([uq_plain, uq_rotp], axis=1)

    ukv = w_ukv.reshape(MLA_KV_RANK, MLA_HEADS, MLA_NOPE_DIM + MLA_V_DIM)
    wuk = jnp.pad(ukv[:, :, :MLA_NOPE_DIM], ((0, 0), (0, 0), (0, LANES - MLA_NOPE_DIM)))
    wuk = wuk.reshape(MLA_KV_RANK, MLA_HEADS * LANES)
    wuv = ukv[:, :, MLA_NOPE_DIM:].reshape(MLA_KV_RANK, MLA_HEADS * MLA_V_DIM)

    bf = jnp.pad(b_forget.astype(F32), (0, LANES - FOX_HEADS)).reshape(1, LANES)
    nf = FOX_HEADS * FOX_HEAD_DIM
    return dict(wa=wa.astype(BF16), wfox=w_fox[:, :2 * nf].astype(BF16),
                wfoxv_t=w_fox[:, 2 * nf:].T.astype(BF16), wdil=wdil.astype(BF16),
                qg=q_gain.reshape(1, -1).astype(F32), kvg=kv_gain.reshape(1, -1).astype(F32),
                wuq=wuq.astype(BF16), wuk=wuk.astype(BF16), wuv_t=wuv.T.astype(BF16), bf=bf,
                sel=_fox_bias_placement())


def _fox_bias_placement():
    sel = np.zeros((3 * LANES, FOX_HEADS * LANES), np.float32)
    for j in range(3):
        for h in range(FOX_HEADS):
            sel[j * LANES + h, h * LANES + FOX_HEAD_DIM + j] = 1.0
    return jnp.asarray(sel, BF16)


def _tiles(S):
    tm = min(512, S)
    return dict(rope=min(512, S), proj=tm, flash=min(512, S), mix=tm, xattn=tm, moe=min(1024, S))


def kernel(x, mem, positions, w_in, b_forget, mla_q_gain, mla_kv_gain, mla_w_uq, mla_w_ukv, w_mix_out, ln_mix_g, ln_mix_b, xattn_w_q, xattn_w_kv, xattn_w_o, ln_mem_g, ln_mem_b, router_w, router_bias, expert_w_gate, expert_w_up, expert_w_down, ln_ffn_g, ln_ffn_b):
    B, S, D = x.shape
    depth = w_in.shape[0]
    t = _tiles(S)
    invf_a, sgn_a, invf_d, sgn_d = _rope_constants()
    cos_a, sin_a = _rope_tables(positions, invf_a, sgn_a, t["rope"])
    cos_d, sin_d = _rope_tables(positions, invf_d, sgn_d, t["rope"])
    tabs = (cos_a, sin_a, cos_d, sin_d)
    rw_t = router_w.T.astype(F32)
    rbias = router_bias.reshape(-1, 1).astype(F32)

    def row(v):
        return v.reshape(1, -1).astype(F32)

    for l in range(depth):
        wts = _layer_weights(w_in[l], b_forget[l], mla_q_gain[l], mla_kv_gain[l], mla_w_uq[l], mla_w_ukv[l])
        qa, ka, va_t, qb, kb, vb_t, qc, kc, vc = _proj(x, wts, tabs, t["proj"])
        oa_t = _flash_attention(qa, ka, va_t, MLA_HEADS, t["flash"], "mla_flash")
        ob_t = _flash_attention(qb, kb, vb_t, FOX_HEADS, t["flash"], "fox_flash")
        oc = _dilated_attention(qc, kc, vc)
        x = _mix_out(oa_t, ob_t, oc, x, w_mix_out[l].astype(BF16), row(ln_mix_g[l]), row(ln_mix_b[l]),
                     t["mix"])
        k_mem, v_mem = _mem_kv(mem, xattn_w_kv[l].astype(BF16))
        x = _xattn(x, k_mem, v_mem, xattn_w_q[l].astype(BF16), xattn_w_o[l].astype(BF16),
                   row(ln_mem_g[l]), row(ln_mem_b[l]), t["xattn"])
        x2d = _moe(x.reshape(B * S, D), rw_t, rbias, expert_w_gate[l].astype(BF16),
                   expert_w_up[l].astype(BF16), expert_w_down[l].astype(BF16),
                   row(ln_ffn_g[l]), row(ln_ffn_b[l]), t["moe"])
        x = x2d.reshape(B, S, D)
    return x
```

```python
import functools
import math

import numpy as np
import jax
import jax.numpy as jnp
from jax import lax
from jax.experimental import pallas as pl
from jax.experimental.pallas import tpu as pltpu

F32 = jnp.float32
BF16 = jnp.bfloat16
HIGHEST = lax.Precision.HIGHEST

D_MODEL = 1024
DEPTH = 4
MLA_HEADS = 4
MLA_Q_RANK = 256
MLA_KV_RANK = 128
MLA_NOPE_DIM = 64
MLA_ROPE_DIM = 32
MLA_V_DIM = 64
FOX_HEADS = 6
FOX_HEAD_DIM = 64
DIL_HEADS = 6
DIL_HEAD_DIM = 64
DILATED_PAIRS = ((128, 1), (512, 4), (2048, 16))
XATTN_HEADS = 4
N_EXPERTS = 16
N_GROUPS = 4
EXPERTS_PER_GROUP = N_EXPERTS // N_GROUPS
D_EXPERT = 256
ROPE_THETA = 10000.0
NORM_EPS = 1e-5
NEG_INF = -1e30
DEEPNORM_ALPHA = (2 * DEPTH) ** 0.25
LOG2E = math.log2(math.e)

LANES = 128
HEAD_LANES = 64
VMEM_LIMIT_BYTES = 52 * 1024 * 1024

A_CQ = 0
A_CKV = A_CQ + MLA_Q_RANK
A_KR = A_CKV + MLA_KV_RANK
A_KRR = A_KR + LANES
A_FF = A_KRR + LANES
A_COLS = A_FF + LANES


def _cparams(sem):
    return pltpu.CompilerParams(dimension_semantics=sem, vmem_limit_bytes=VMEM_LIMIT_BYTES)


def _nt_dot(a, b, precision=None):
    return lax.dot_general(a, b, (((1,), (1,)), ((), ())), precision=precision,
                           preferred_element_type=F32)


def _dot(a, b, precision=None):
    return jnp.dot(a, b, precision=precision, preferred_element_type=F32)


def _layer_norm(y, g, b):
    mu = jnp.mean(y, axis=-1, keepdims=True)
    yc = y - mu
    var = jnp.mean(yc * yc, axis=-1, keepdims=True)
    return yc * lax.rsqrt(var + NORM_EPS) * g + b


def _rms_norm(y, g):
    ms = jnp.mean(y * y, axis=-1, keepdims=True)
    return y * lax.rsqrt(ms + NORM_EPS) * g


def _rope_table_kernel(pos_ref, invf_ref, sgn_ref, cos_ref, sin_ref):
    ang = pos_ref[0].astype(F32) * invf_ref[...]
    cos_ref[0] = jnp.cos(ang)
    sin_ref[0] = jnp.sin(ang) * sgn_ref[...]


def _rope_tables(positions, invf, sgn, ts):
    B, S = positions.shape
    pos3 = positions.reshape(B, S, 1)
    spec = pl.BlockSpec((1, ts, LANES), lambda b, i: (b, i, 0))
    vec = pl.BlockSpec((1, LANES), lambda b, i: (0, 0))
    return pl.pallas_call(
        _rope_table_kernel,
        out_shape=(jax.ShapeDtypeStruct((B, S, LANES), F32),) * 2,
        grid=(B, S // ts),
        in_specs=[pl.BlockSpec((1, ts, 1), lambda b, i: (b, i, 0)), vec, vec],
        out_specs=(spec, spec),
        compiler_params=_cparams(("parallel", "parallel")),
        name="rope_tables",
    )(pos3, invf, sgn)


def _proj_kernel(x_ref, wa_ref, wfox_ref, wfoxv_ref, wdil_ref, qg_ref, kvg_ref, wuq_ref, wuk_ref,
                 wuv_ref, bf_ref, cosa_ref, sina_ref, cosd_ref, sind_ref,
                 qa_ref, ka_ref, va_ref, qb_ref, kb_ref, vb_ref,
                 qc_ref, kc_ref, vc_ref, carry_c, *dil_scs, tm):
    i = pl.program_id(1)
    xb = x_ref[0].astype(BF16)

    ha = _dot(xb, wa_ref[...])
    cqn = _rms_norm(ha[:, A_CQ:A_CQ + MLA_Q_RANK], qg_ref[...]).astype(BF16)
    ckvn = _rms_norm(ha[:, A_CKV:A_CKV + MLA_KV_RANK], kvg_ref[...]).astype(BF16)
    cos_a = cosa_ref[0]
    sin_a = sina_ref[0]
    q2 = _dot(cqn, wuq_ref[...])
    k_nope = _dot(ckvn, wuk_ref[...])
    k_rot = ha[:, A_KR:A_KR + LANES] * cos_a + ha[:, A_KRR:A_KRR + LANES] * sin_a
    q_scale = (MLA_NOPE_DIM + MLA_ROPE_DIM) ** -0.5 * LOG2E
    n_half = MLA_HEADS * LANES
    for h in range(MLA_HEADS):
        sl = slice(h * LANES, (h + 1) * LANES)
        qh = q2[:, sl] * cos_a + q2[:, n_half + h * LANES:n_half + (h + 1) * LANES] * sin_a
        qa_ref[0, :, sl] = (qh * q_scale).astype(BF16)
        ka_ref[0, :, sl] = (k_nope[:, sl] + k_rot).astype(BF16)
    va_ref[0] = _nt_dot(wuv_ref[...], ckvn).astype(BF16)

    lane = lax.broadcasted_iota(jnp.int32, (1, LANES), 1)
    bias_lanes = (lane >= FOX_HEAD_DIM) & (lane < FOX_HEAD_DIM + 3 * FOX_HEADS)
    z = ha[:, A_FF:A_FF + LANES] + bf_ref[...]
    logf = jnp.minimum(z, 0.0) - jnp.log(1.0 + jnp.exp(-jnp.abs(z)))
    logf = jnp.where(bias_lanes, logf, 0.0)

    @pl.when(i == 0)
    def _():
        carry_c[...] = jnp.zeros_like(carry_c)

    p0 = logf.astype(BF16).astype(F32)
    r0 = logf - p0
    p1 = r0.astype(BF16).astype(F32)
    p2 = (r0 - p1).astype(BF16)
    packed = jnp.concatenate([(p0 + pltpu.roll(p1, HEAD_LANES, 1)).astype(BF16), p2], axis=1)
    r = lax.broadcasted_iota(jnp.int32, (tm, tm), 0)
    c = lax.broadcasted_iota(jnp.int32, (tm, tm), 1)
    lower = (c <= r).astype(BF16)
    cum = _dot(lower, packed)
    cum_a = cum[:, :LANES]
    fcol = jnp.where(bias_lanes, cum_a + pltpu.roll(cum_a, HEAD_LANES, 1) + cum[:, LANES:], 0.0) + carry_c[...]
    carry_c[...] = fcol[tm - 1:tm, :]

    nb = fcol * (-LOG2E)
    hi = nb.astype(BF16).astype(F32)
    r1 = nb - hi
    mid = r1.astype(BF16).astype(F32)
    lo3 = (r1 - mid).astype(BF16).astype(F32)
    piece = (lane - FOX_HEAD_DIM) % 3
    k_bias = jnp.where(piece == 0, hi, jnp.where(piece == 1, mid, lo3))

    hf = _dot(xb, wfox_ref[...])
    nf = FOX_HEADS * FOX_HEAD_DIM
    dims = lane < FOX_HEAD_DIM
    fq_scale = FOX_HEAD_DIM ** -0.5 * LOG2E
    for h in range(FOX_HEADS):
        blk = slice((h // 2) * LANES, (h // 2 + 1) * LANES)
        out = slice(h * LANES, (h + 1) * LANES)
        qh = hf[:, blk]
        kh = hf[:, nf + (h // 2) * LANES:nf + (h // 2 + 1) * LANES]
        if h % 2:
            qh = pltpu.roll(qh, HEAD_LANES, 1)
            kh = pltpu.roll(kh, HEAD_LANES, 1)
        own = (lane >= FOX_HEAD_DIM + 3 * h) & (lane < FOX_HEAD_DIM + 3 * h + 3)
        qb_ref[0, :, out] = jnp.where(dims, qh * fq_scale, jnp.where(own, 1.0, 0.0)).astype(BF16)
        kb_ref[0, :, out] = jnp.where(dims, kh, k_bias).astype(BF16)
    vb_ref[0] = _nt_dot(wfoxv_ref[...], xb).astype(BF16)

    hd = _dot(xb, wdil_ref[...])
    cos_d = cosd_ref[0]
    sin_d = sind_ref[0]
    nd = DIL_HEADS * DIL_HEAD_DIM
    for pblk in range(nd // LANES):
        sl = slice(pblk * LANES, (pblk + 1) * LANES)
        qh = hd[:, sl]
        kh = hd[:, nd + pblk * LANES:nd + (pblk + 1) * LANES]
        qh = qh * cos_d + pltpu.roll(qh, HEAD_LANES, 1) * sin_d
        kh = kh * cos_d + pltpu.roll(kh, HEAD_LANES, 1) * sin_d
        dil_scs[pblk][...] = qh * (DIL_HEAD_DIM ** -0.5 * LOG2E)
        dil_scs[3 + pblk][...] = kh
        dil_scs[6 + pblk][...] = hd[:, 2 * nd + pblk * LANES:2 * nd + (pblk + 1) * LANES]
    for a, ref in enumerate((qc_ref, kc_ref, vc_ref)):
        for r in range(DIL_PLANES):
            rows = pl.ds(r, tm // DIL_PLANES, stride=DIL_PLANES)
            for pblk in range(nd // LANES):
                lanes = slice(r * nd + pblk * LANES, r * nd + (pblk + 1) * LANES)
                ref[0, :, lanes] = dil_scs[3 * a + pblk][rows, :].astype(BF16)


def _proj(x, wts, tabs, tm):
    B, S, D = x.shape
    cos_a, sin_a, cos_d, sin_d = tabs
    grid = (B, S // tm)

    def full(a):
        return pl.BlockSpec(a.shape, lambda b, i: (0,) * a.ndim)

    def tok(width, dtype):
        return (jax.ShapeDtypeStruct((B, S, width), dtype),
                pl.BlockSpec((1, tm, width), lambda b, i: (b, i, 0)))

    tab = pl.BlockSpec((1, tm, LANES), lambda b, i: (b, i, 0))
    nd = DIL_HEADS * DIL_HEAD_DIM

    def planes(dtype):
        return (jax.ShapeDtypeStruct((B, S // DIL_PLANES, DIL_PLANES * nd), dtype),
                pl.BlockSpec((1, tm // DIL_PLANES, DIL_PLANES * nd), lambda b, i: (b, i, 0)))

    def tok_t(height, dtype):
        return (jax.ShapeDtypeStruct((B, height, S), dtype),
                pl.BlockSpec((1, height, tm), lambda b, i: (b, 0, i)))

    outs = [tok(MLA_HEADS * LANES, BF16), tok(MLA_HEADS * LANES, BF16),
            tok_t(MLA_HEADS * MLA_V_DIM, BF16),
            tok(FOX_HEADS * LANES, BF16), tok(FOX_HEADS * LANES, BF16),
            tok_t(FOX_HEADS * FOX_HEAD_DIM, BF16),
            planes(BF16), planes(BF16), planes(BF16)]
    w_list = [wts["wa"], wts["wfox"], wts["wfoxv_t"], wts["wdil"], wts["qg"], wts["kvg"], wts["wuq"],
              wts["wuk"], wts["wuv_t"], wts["bf"]]
    return pl.pallas_call(
        functools.partial(_proj_kernel, tm=tm),
        out_shape=tuple(o[0] for o in outs),
        grid=grid,
        in_specs=[pl.BlockSpec((1, tm, D), lambda b, i: (b, i, 0))] + [full(w) for w in w_list]
        + [tab, tab, tab, tab],
        out_specs=tuple(o[1] for o in outs),
        scratch_shapes=[pltpu.VMEM((1, LANES), F32)] + [pltpu.VMEM((tm, LANES), F32)] * (3 * nd // LANES),
        compiler_params=_cparams(("parallel", "arbitrary")),
        name="proj",
    )(x, *w_list, cos_a, sin_a, cos_d, sin_d)


FLASH_CHUNK = 32


FLASH_KEY_SPLIT = 2
ONES_ROWS = 16


def _flash_kernel(q0_ref, q1_ref, k0_ref, k1_ref, vt_ref, o_ref, *scratch, tq):
    i = pl.program_id(2)
    q_refs = (q0_ref, q1_ref)
    k_refs = (k0_ref, k1_ref)
    n_chain = 2 * FLASH_KEY_SPLIT
    s_scs = scratch[:n_chain]
    p_scs = scratch[n_chain:2 * n_chain]
    m_sc, l_sc, acc0_sc, acc1_sc = scratch[2 * n_chain:]
    acc_scs = (acc0_sc, acc1_sc)
    dv = HEAD_LANES
    tk = tq // FLASH_KEY_SPLIT
    n_chunks = tk // FLASH_CHUNK

    m_sc[...] = jnp.full_like(m_sc, NEG_INF)
    l_sc[...] = jnp.zeros_like(l_sc)
    acc0_sc[...] = jnp.zeros_like(acc0_sc)
    acc1_sc[...] = jnp.zeros_like(acc1_sc)

    def step(j, masked):
        m_locs = []
        for h in range(2):
            for c in range(FLASH_KEY_SPLIT):
                ks = pl.multiple_of(j * tq + c * tk, tk)
                k = k_refs[h][0, pl.ds(ks, tk), :]
                st = _nt_dot(k, q_refs[h][0])
                if masked:
                    key = lax.broadcasted_iota(jnp.int32, (tk, tq), 0) + c * tk
                    qry = lax.broadcasted_iota(jnp.int32, (tk, tq), 1)
                    st = jnp.where(key <= qry, st, NEG_INF)
                s_scs[h * FLASH_KEY_SPLIT + c][...] = st
                m8 = jnp.max(st.reshape(tk // 8, 8, tq), axis=0)
                m_locs.append(jnp.max(m8, axis=0, keepdims=True))
        locs = []
        for h in range(2):
            for c in range(FLASH_KEY_SPLIT):
                ks = pl.multiple_of(j * tq + c * tk, tk)
                s_sc = s_scs[h * FLASH_KEY_SPLIT + c]
                p_sc = p_scs[h * FLASH_KEY_SPLIT + c]
                m_loc = m_locs[h * FLASH_KEY_SPLIT + c]
                for r in range(n_chunks):
                    rows = slice(r * FLASH_CHUNK, (r + 1) * FLASH_CHUNK)
                    p_sc[rows, :] = jnp.exp2(s_sc[rows, :] - m_loc).astype(BF16)
                vt = vt_ref[0, h * dv:(h + 1) * dv, pl.ds(ks, tk)]
                vt_ext = jnp.concatenate([vt, jnp.ones((ONES_ROWS, tk), BF16)], axis=0)
                pvl = _dot(vt_ext, p_sc[...])
                locs.append((m_loc, pvl))
        for h in range(2):
            mine = locs[h * FLASH_KEY_SPLIT:(h + 1) * FLASH_KEY_SPLIT]
            m_prev = m_sc[h]
            m_new = m_prev
            for m_loc, _ in mine:
                m_new = jnp.maximum(m_new, m_loc)
            a_prev = jnp.exp2(m_prev - m_new)
            acc = acc_scs[h][...] * a_prev
            l = l_sc[h] * a_prev
            for m_loc, pvl in mine:
                a_loc = jnp.exp2(m_loc - m_new)
                acc = acc + pvl[:dv] * a_loc
                l = l + pvl[dv:dv + 1] * a_loc
            acc_scs[h][...] = acc
            l_sc[h] = l
            m_sc[h] = m_new

    def body(jj, carry):
        step(2 * jj, False)
        step(2 * jj + 1, False)
        return carry

    lax.fori_loop(0, i // 2, body, 0)

    @pl.when(i % 2 == 1)
    def _():
        step(i - 1, False)

    step(i, True)
    for h in range(2):
        o_ref[0, h * dv:(h + 1) * dv, :] = (acc_scs[h][...] / l_sc[h]).astype(o_ref.dtype)


def _flash_attention(q, k, vt, n_heads, tq, name):
    B, S, _ = q.shape
    n_pairs = n_heads // 2
    dv2 = 2 * HEAD_LANES
    qspec0 = pl.BlockSpec((1, tq, LANES), lambda b, p, i: (b, i, 2 * p))
    qspec1 = pl.BlockSpec((1, tq, LANES), lambda b, p, i: (b, i, 2 * p + 1))
    kspec0 = pl.BlockSpec((1, S, LANES), lambda b, p, i: (b, 0, 2 * p))
    kspec1 = pl.BlockSpec((1, S, LANES), lambda b, p, i: (b, 0, 2 * p + 1))
    vspec = pl.BlockSpec((1, dv2, S), lambda b, p, i: (b, p, 0))
    return pl.pallas_call(
        functools.partial(_flash_kernel, tq=tq),
        out_shape=jax.ShapeDtypeStruct((B, n_heads * HEAD_LANES, S), BF16),
        grid=(B, n_pairs, S // tq),
        in_specs=[qspec0, qspec1, kspec0, kspec1, vspec],
        out_specs=pl.BlockSpec((1, dv2, tq), lambda b, p, i: (b, p, i)),
        scratch_shapes=[pltpu.VMEM((tq // FLASH_KEY_SPLIT, tq), F32)] * (2 * FLASH_KEY_SPLIT)
        + [pltpu.VMEM((tq // FLASH_KEY_SPLIT, tq), BF16)] * (2 * FLASH_KEY_SPLIT)
        + [pltpu.VMEM((2, 1, tq), F32), pltpu.VMEM((2, 1, tq), F32),
                        pltpu.VMEM((HEAD_LANES, tq), F32), pltpu.VMEM((HEAD_LANES, tq), F32)],
        compiler_params=_cparams(("parallel", "parallel", "arbitrary")),
        name=name,
    )(q, q, k, k, vt)


DIL_PLANES = 16
DIL_TQ = 128
DIL_UNROLL = 4


def _plane_slot(r16):
    return 4 * (r16 % 4) + r16 // 4


def _dilated_geometry(dil):
    planes = DIL_PLANES // dil
    rows = DIL_TQ // planes
    if dil == 1:
        order = [(c // 4) + 4 * (c % 4) for c in range(planes)]
    else:
        order = list(range(planes))
    return planes, rows, order


def _dilated_bias(dil, first):
    planes, rows, order = _dilated_geometry(dil)
    krows = 2 * rows
    a = np.arange(planes * rows)
    b = np.arange(planes * krows)
    uq = np.asarray(order)[a // rows] + planes * (a % rows + (0 if first else rows))
    uk = np.asarray(order)[b // krows] + planes * (b % krows)
    diff = uq[:, None] - uk[None, :]
    valid = (diff >= 0) & (diff <= DIL_TQ)
    return jnp.asarray(np.where(valid, 0.0, NEG_INF), F32)


def _dilated_kernel(q_ref, k_ref, v_ref, b1f_ref, b1r_ref, b4f_ref, b4r_ref, b16f_ref, b16r_ref, o_ref,
                    q_st, k_st, v_st, m_sc, l_sc, acc_sc, *, rows_per_plane):
    pair = pl.program_id(1)
    n_pairs = DIL_HEADS // 2
    lane = lax.broadcasted_iota(jnp.int32, (1, LANES), 1)
    lo = lane < HEAD_LANES
    q_lo = (lane % HEAD_LANES) < (HEAD_LANES // 2)
    tq = DIL_TQ

    def plane_lanes(r16):
        return pl.ds(pl.multiple_of((r16 * n_pairs + pair) * LANES, LANES), LANES)

    for r16 in range(DIL_PLANES):
        slot = _plane_slot(r16)
        q_st[slot] = q_ref[0, :, plane_lanes(r16)].astype(F32)
        k_st[slot] = k_ref[0, :, plane_lanes(r16)].astype(F32)
        v_st[slot] = v_ref[0, :, plane_lanes(r16)].astype(F32)
    m_sc[...] = jnp.full_like(m_sc, NEG_INF)
    l_sc[...] = jnp.zeros_like(l_sc)
    acc_sc[...] = jnp.zeros_like(acc_sc)

    def block(planes, rows, slot0, row0, first, bias_ref):
        qsl = (pl.ds(slot0, planes), pl.ds(row0, rows), slice(None))
        k_row0 = row0 if first else row0 - rows
        ksl = (pl.ds(slot0, planes), pl.ds(k_row0, 2 * rows), slice(None))
        q = q_st[qsl].reshape(tq, LANES)
        k = k_st[ksl].reshape(-1, LANES).astype(BF16)
        v = v_st[ksl].reshape(-1, LANES).astype(BF16)
        zero = jnp.zeros_like(q)
        qs = (jnp.where(q_lo, q, zero).astype(BF16), jnp.where(q_lo, zero, q).astype(BF16))
        bias = bias_ref[...]
        v_ext = jnp.concatenate([v, jnp.ones_like(v)], axis=1)
        maxes, sums, pvs = [], [], []
        for h in range(2):
            s = _nt_dot(qs[h], k) + bias
            mh = jnp.max(s, axis=1, keepdims=True)
            p = jnp.exp2(s - mh)
            maxes.append(mh)
            pvl = _dot(p.astype(BF16), v_ext)
            pvs.append(pvl[:, :LANES])
            sums.append(pvl[:, LANES:])
        m_blk = jnp.where(lo, maxes[0], maxes[1])
        m_old = m_sc[qsl].reshape(tq, LANES)
        m_new = jnp.maximum(m_old, m_blk)
        a_old = jnp.exp2(m_old - m_new)
        a_blk = jnp.exp2(m_blk - m_new)
        l_new = a_old * l_sc[qsl].reshape(tq, LANES) + a_blk * jnp.where(lo, sums[0], sums[1])
        acc_new = a_old * acc_sc[qsl].reshape(tq, LANES) + a_blk * jnp.where(lo, pvs[0], pvs[1])
        m_sc[qsl] = m_new.reshape(planes, rows, LANES)
        l_sc[qsl] = l_new.reshape(planes, rows, LANES)
        acc_sc[qsl] = acc_new.reshape(planes, rows, LANES)

    bias_refs = {1: (b1f_ref, b1r_ref), 4: (b4f_ref, b4r_ref), 16: (b16f_ref, b16r_ref)}
    for window, dil in DILATED_PAIRS:
        assert window // dil == tq
        planes, rows, _ = _dilated_geometry(dil)
        nblk = rows_per_plane // rows
        bf_ref, br_ref = bias_refs[dil]

        def first_body(sub, carry, planes=planes, rows=rows, bf_ref=bf_ref):
            block(planes, rows, sub * planes, 0, True, bf_ref)
            return carry

        lax.fori_loop(0, dil, first_body, 0, unroll=min(dil, DIL_UNROLL))

        if nblk > 1:
            def rest_body(t, carry, planes=planes, rows=rows, nblk=nblk, br_ref=br_ref):
                sub = t // (nblk - 1)
                blk = t % (nblk - 1) + 1
                block(planes, rows, sub * planes, pl.multiple_of(blk * rows, rows), False, br_ref)
                return carry

            lax.fori_loop(0, dil * (nblk - 1), rest_body, 0, unroll=DIL_UNROLL)

    for r16 in range(DIL_PLANES):
        slot = _plane_slot(r16)
        o_ref[0, :, plane_lanes(r16)] = (acc_sc[slot] / l_sc[slot]).astype(o_ref.dtype)


def _dilated_attention(qc, kc, vc):
    B, rpp, width = qc.shape
    n_pairs = DIL_HEADS // 2
    assert rpp % DIL_TQ == 0 and rpp >= 2 * DIL_TQ

    spec = pl.BlockSpec((1, rpp, width), lambda b, p: (b, 0, 0))
    biases = [_dilated_bias(d, f) for _, d in DILATED_PAIRS for f in (True, False)]
    bspecs = [pl.BlockSpec(bb.shape, lambda b, p: (0, 0)) for bb in biases]
    plane = pltpu.VMEM((DIL_PLANES, rpp, LANES), F32)
    return pl.pallas_call(
        functools.partial(_dilated_kernel, rows_per_plane=rpp),
        out_shape=jax.ShapeDtypeStruct((B, rpp, width), BF16),
        grid=(B, n_pairs),
        in_specs=[spec, spec, spec] + bspecs,
        out_specs=spec,
        scratch_shapes=[plane] * 6,
        compiler_params=_cparams(("parallel", "arbitrary")),
        name="dilated",
    )(qc, kc, vc, *biases)


def _tn_dot(a_t, b):
    return lax.dot_general(a_t, b, (((0,), (0,)), ((), ())), preferred_element_type=F32)


def _mix_out_kernel(oat_ref, obt_ref, oc_ref, x_ref, wa_ref, wb_ref, wc_ref, g_ref, b_ref, o_ref, *oc_scs):
    nd = len(oc_scs) * LANES
    rows_per_plane = oc_scs[0].shape[0] // DIL_PLANES
    for r in range(DIL_PLANES):
        rows = pl.ds(r, rows_per_plane, stride=DIL_PLANES)
        for pblk, sc in enumerate(oc_scs):
            lanes = slice(r * nd + pblk * LANES, r * nd + (pblk + 1) * LANES)
            sc[rows, :] = oc_ref[0, :, lanes].astype(F32)
    oc = jnp.concatenate([sc[...] for sc in oc_scs], axis=1).astype(BF16)
    y = (_tn_dot(oat_ref[0], wa_ref[...]) + _tn_dot(obt_ref[0], wb_ref[...])
         + _dot(oc, wc_ref[...]))
    o_ref[0] = _layer_norm(DEEPNORM_ALPHA * x_ref[0] + y, g_ref[...], b_ref[...])


def _mix_out(oa_t, ob_t, oc, x, w_out, g, b, tm):
    B, S, D = x.shape
    na, nb = oa_t.shape[1], ob_t.shape[1]
    wa, wb, wc = w_out[:na], w_out[na:na + nb], w_out[na + nb:]

    def cols(a):
        return pl.BlockSpec((1, a.shape[1], tm), lambda bb, i: (bb, 0, i))

    def rows(a):
        return pl.BlockSpec((1, tm, a.shape[2]), lambda bb, i: (bb, i, 0))

    def full(a):
        return pl.BlockSpec(a.shape, lambda bb, i: (0, 0))

    return pl.pallas_call(
        _mix_out_kernel,
        out_shape=jax.ShapeDtypeStruct((B, S, D), F32),
        grid=(B, S // tm),
        in_specs=[cols(oa_t), cols(ob_t),
                  pl.BlockSpec((1, tm // DIL_PLANES, oc.shape[2]), lambda bb, i: (bb, i, 0)),
                  rows(x), full(wa), full(wb), full(wc), full(g), full(b)],
        out_specs=rows(x),
        scratch_shapes=[pltpu.VMEM((tm, LANES), F32)] * (oc.shape[2] // DIL_PLANES // LANES),
        compiler_params=_cparams(("parallel", "parallel")),
        name="mix_out",
    )(oa_t, ob_t, oc, x, wa, wb, wc, g, b)


def _mem_kv_kernel(mem_ref, w_ref, k_ref, v_ref):
    kv = _dot(mem_ref[0].astype(BF16), w_ref[...])
    d = k_ref.shape[-1]
    k_ref[0] = kv[:, :d].astype(BF16)
    v_ref[0] = kv[:, d:].astype(BF16)


def _mem_kv(mem, w_kv):
    B, M, D = mem.shape
    spec = pl.BlockSpec((1, M, D), lambda b: (b, 0, 0))
    return pl.pallas_call(
        _mem_kv_kernel,
        out_shape=(jax.ShapeDtypeStruct((B, M, D), BF16),) * 2,
        grid=(B,),
        in_specs=[spec, pl.BlockSpec(w_kv.shape, lambda b: (0, 0))],
        out_specs=(spec, spec),
        compiler_params=_cparams(("parallel",)),
        name="mem_kv",
    )(mem, w_kv)


def _xattn_kernel(x_ref, k_ref, v_ref, wq_ref, wo_ref, g_ref, b_ref, o_ref):
    x = x_ref[0]
    d = x.shape[-1]
    dh = d // XATTN_HEADS
    q = (_dot(x.astype(BF16), wq_ref[...]) * (dh ** -0.5)).astype(BF16)
    outs = []
    for h in range(XATTN_HEADS):
        sl = slice(h * dh, (h + 1) * dh)
        s = _nt_dot(q[:, sl], k_ref[0, :, sl])
        p = jnp.exp(s - jnp.max(s, axis=1, keepdims=True))
        o = _dot(p.astype(BF16), v_ref[0, :, sl])
        outs.append((o / jnp.sum(p, axis=1, keepdims=True)).astype(BF16))
    o_all = jnp.concatenate(outs, axis=1)
    y = _dot(o_all, wo_ref[...])
    o_ref[0] = _layer_norm(DEEPNORM_ALPHA * x + y, g_ref[...], b_ref[...])


def _xattn(x, k_mem, v_mem, w_q, w_o, g, b, tm):
    B, S, D = x.shape
    M = k_mem.shape[1]
    xspec = pl.BlockSpec((1, tm, D), lambda bb, i: (bb, i, 0))
    mspec = pl.BlockSpec((1, M, D), lambda bb, i: (bb, 0, 0))

    def full(a):
        return pl.BlockSpec(a.shape, lambda bb, i: (0, 0))

    return pl.pallas_call(
        _xattn_kernel,
        out_shape=jax.ShapeDtypeStruct((B, S, D), F32),
        grid=(B, S // tm),
        in_specs=[xspec, mspec, mspec, full(w_q), full(w_o), full(g), full(b)],
        out_specs=xspec,
        compiler_params=_cparams(("parallel", "parallel")),
        name="xattn",
    )(x, k_mem, v_mem, w_q, w_o, g, b)


def _split_bf16(a):
    hi = a.astype(BF16)
    return hi, (a - hi.astype(F32)).astype(BF16)


def _router_gates_t(x, rw_t, rbias):
    x_hi, x_lo = _split_bf16(x)
    w_hi, w_lo = _split_bf16(rw_t)
    t1 = _nt_dot(jnp.concatenate([w_hi, w_lo], axis=0), x_hi)
    logits_t = t1[:N_EXPERTS] + (t1[N_EXPERTS:] + _nt_dot(w_hi, x_lo))
    scores = 1.0 / (1.0 + jnp.exp(-logits_t))
    biased = scores + rbias
    sc = [scores[e:e + 1, :] for e in range(N_EXPERTS)]
    bs = [biased[e:e + 1, :] for e in range(N_EXPERTS)]
    epg = EXPERTS_PER_GROUP

    def beats(a, b, a_first):
        return (a >= b) if a_first else (a > b)

    grp = []
    for gi in range(N_GROUPS):
        v = bs[gi * epg:(gi + 1) * epg]
        best = None
        for a in range(epg):
            for b in range(a + 1, epg):
                pair = v[a] + v[b]
                best = pair if best is None else jnp.maximum(best, pair)
        grp.append(best)
    gates = []
    for gi in range(N_GROUPS):
        g_sel = None
        for gj in range(N_GROUPS):
            if gj == gi:
                continue
            w = beats(grp[gi], grp[gj], gi < gj)
            g_sel = w if g_sel is None else (g_sel & w)
        in_top = []
        for a in range(epg):
            ea = gi * epg + a
            n_above = jnp.zeros(bs[ea].shape, jnp.int32)
            for b in range(epg):
                if b == a:
                    continue
                eb = gi * epg + b
                n_above = n_above + beats(bs[eb], bs[ea], b < a).astype(jnp.int32)
            in_top.append(g_sel & (n_above < 2))
        denom = None
        for a in range(epg):
            term = jnp.where(in_top[a], sc[gi * epg + a], 0.0)
            denom = term if denom is None else denom + term
        denom = jnp.where(g_sel, denom, 1.0)
        for a in range(epg):
            gates.append(jnp.where(in_top[a], sc[gi * epg + a] / denom, 0.0))
    return jnp.concatenate(gates, axis=0)


def _moe_kernel(x_ref, rwt_ref, rb_ref, wg_ref, wu_ref, wd_ref, g_ref, b_ref, o_ref,
                gates_sc, acc_sc, xb_sc, *, tm):
    grp = pl.program_id(1)
    epg = EXPERTS_PER_GROUP

    @pl.when(grp == 0)
    def _():
        ts = min(tm, ROUTER_ROWS)
        eye = (lax.broadcasted_iota(jnp.int32, (ts, ts), 0)
               == lax.broadcasted_iota(jnp.int32, (ts, ts), 1)).astype(BF16)
        ne = N_EXPERTS
        for r in range(tm // ts):
            rows = slice(r * ts, (r + 1) * ts)
            x = x_ref[rows, :]
            gates_t = _router_gates_t(x, rwt_ref[...], rb_ref[...])
            hi = gates_t.astype(BF16)
            rest = gates_t - hi.astype(F32)
            mid, lo = _split_bf16(rest)
            g3 = _nt_dot(eye, jnp.concatenate([hi, mid, lo], axis=0))
            gates_sc[rows, :] = g3[:, :ne] + (g3[:, ne:2 * ne] + g3[:, 2 * ne:])
            xb_sc[rows, :] = x.astype(BF16)

    xb = xb_sc[...]
    lane_e = lax.broadcasted_iota(jnp.int32, (1, N_EXPERTS), 1)
    gates = gates_sc[...]
    acts = []
    for e in range(epg):
        hg = _dot(xb, wg_ref[e])
        hu = _dot(xb, wu_ref[e])
        gate = jnp.sum(jnp.where(lane_e == grp * epg + e, gates, 0.0), axis=1, keepdims=True)
        acts.append(((hg / (1.0 + jnp.exp(-hg))) * hu * gate).astype(BF16))
    part = _dot(jnp.concatenate(acts, axis=1), wd_ref[...].reshape(epg * D_EXPERT, -1))

    @pl.when(grp == 0)
    def _():
        acc_sc[...] = part

    @pl.when(grp > 0)
    def _():
        acc_sc[...] += part

    @pl.when(grp == N_GROUPS - 1)
    def _():
        o_ref[...] = _layer_norm(DEEPNORM_ALPHA * x_ref[...] + acc_sc[...], g_ref[...], b_ref[...])


ROUTER_ROWS = 512


def _moe(x2d, rw_t, rbias, wg, wu, wd, g, b, tm):
    T, D = x2d.shape
    xspec = pl.BlockSpec((tm, D), lambda i, e: (i, 0))
    epg = EXPERTS_PER_GROUP

    def full(a):
        return pl.BlockSpec(a.shape, lambda i, e: (0, 0))

    return pl.pallas_call(
        functools.partial(_moe_kernel, tm=tm),
        out_shape=jax.ShapeDtypeStruct((T, D), F32),
        grid=(T // tm, N_GROUPS),
        in_specs=[xspec, full(rw_t), full(rbias),
                  pl.BlockSpec((epg, D, D_EXPERT), lambda i, e: (e, 0, 0)),
                  pl.BlockSpec((epg, D, D_EXPERT), lambda i, e: (e, 0, 0)),
                  pl.BlockSpec((epg, D_EXPERT, D), lambda i, e: (e, 0, 0)),
                  full(g), full(b)],
        out_specs=xspec,
        scratch_shapes=[pltpu.VMEM((tm, N_EXPERTS), F32), pltpu.VMEM((tm, D), F32),
                        pltpu.VMEM((tm, D), BF16)],
        compiler_params=_cparams(("parallel", "arbitrary")),
        name="moe",
    )(x2d, rw_t, rbias, wg, wu, wd, g, b)


def _rope_constants():
    half_d = DIL_HEAD_DIM // 2
    inv_d = ROPE_THETA ** (-jnp.arange(half_d, dtype=F32) / half_d)
    half_a = MLA_ROPE_DIM // 2
    inv_a = ROPE_THETA ** (-jnp.arange(half_a, dtype=F32) / half_a)
    lane = np.arange(LANES)
    invf_d = inv_d[lane % half_d].reshape(1, LANES)
    sgn_d = jnp.asarray(np.where(lane < HEAD_LANES, -1.0, 1.0).reshape(1, LANES), F32)
    in_rope = (lane >= MLA_NOPE_DIM) & (lane < MLA_NOPE_DIM + MLA_ROPE_DIM)
    invf_a = jnp.where(jnp.asarray(in_rope), inv_a[(lane - MLA_NOPE_DIM) % half_a], 0.0).reshape(1, LANES)
    sgn_a = jnp.ones((1, LANES), F32)
    return invf_a, sgn_a, invf_d, sgn_d


def _layer_weights(w_in, b_forget, q_gain, kv_gain, w_uq, w_ukv):
    D = w_in.shape[0]
    cuts = np.cumsum([MLA_Q_RANK, MLA_KV_RANK, MLA_ROPE_DIM, 3 * FOX_HEADS * FOX_HEAD_DIM, FOX_HEADS])
    w_cq, w_ckv, w_kr, w_fox, w_ff, w_dil = jnp.split(w_in, cuts.tolist(), axis=1)
    half = MLA_ROPE_DIM // 2
    w_kr_rot = jnp.concatenate([-w_kr[:, half:], w_kr[:, :half]], axis=1)

    def place_rope(w):
        return jnp.pad(w, ((0, 0), (MLA_NOPE_DIM, LANES - MLA_NOPE_DIM - MLA_ROPE_DIM)))

    def place_gate(w):
        rep = jnp.repeat(w, 3, axis=1)
        return jnp.pad(rep, ((0, 0), (FOX_HEAD_DIM, LANES - FOX_HEAD_DIM - 3 * FOX_HEADS)))

    wa = jnp.concatenate([w_cq, w_ckv, place_rope(w_kr), place_rope(w_kr_rot),
                          place_gate(w_ff)], axis=1)

    nd = DIL_HEADS * DIL_HEAD_DIM

    def pair_layout(w):
        half = DIL_HEAD_DIM // 2
        w5 = w.reshape(D, DIL_HEADS // 2, 2, 2, half)
        return w5.transpose(0, 1, 3, 2, 4).reshape(D, nd)

    wdil = jnp.concatenate([pair_layout(w_dil[:, :nd]), pair_layout(w_dil[:, nd:2 * nd]), w_dil[:, 2 * nd:]],
                           axis=1)

    dq = MLA_NOPE_DIM + MLA_ROPE_DIM
    uq = w_uq.reshape(MLA_Q_RANK, MLA_HEADS, dq)
    uq_rope = uq[:, :, MLA_NOPE_DIM:]
    uq_rot = jnp.concatenate([-uq_rope[:, :, half:], uq_rope[:, :, :half]], axis=2)
    pad_tail = LANES - dq
    uq_plain = jnp.pad(uq, ((0, 0), (0, 0), (0, pad_tail))).reshape(MLA_Q_RANK, MLA_HEADS * LANES)
    uq_rotp = jnp.pad(uq_rot, ((0, 0), (0, 0), (MLA_NOPE_DIM, pad_tail))).reshape(MLA_Q_RANK, MLA_HEADS * LANES)
    wuq = jnp.concatenate([uq_plain, uq_rotp], axis=1)

    ukv = w_ukv.reshape(MLA_KV_RANK, MLA_HEADS, MLA_NOPE_DIM + MLA_V_DIM)
    wuk = jnp.pad(ukv[:, :, :MLA_NOPE_DIM], ((0, 0), (0, 0), (0, LANES - MLA_NOPE_DIM)))
    wuk = wuk.reshape(MLA_KV_RANK, MLA_HEADS * LANES)
    wuv = ukv[:, :, MLA_NOPE_DIM:].reshape(MLA_KV_RANK, MLA_HEADS * MLA_V_DIM)

    bf = place_gate(b_forget.astype(F32).reshape(1, -1))
    nf = FOX_HEADS * FOX_HEAD_DIM
    return dict(wa=wa.astype(BF16), wfox=w_fox[:, :2 * nf].astype(BF16),
                wfoxv_t=w_fox[:, 2 * nf:].T.astype(BF16), wdil=wdil.astype(BF16),
                qg=q_gain.reshape(1, -1).astype(F32), kvg=kv_gain.reshape(1, -1).astype(F32),
                wuq=wuq.astype(BF16), wuk=wuk.astype(BF16), wuv_t=wuv.T.astype(BF16), bf=bf)


def _tiles(S):
    tm = min(512, S)
    return dict(rope=min(512, S), proj=tm, flash=min(512, S), mix=tm, xattn=tm, moe=min(1024, S))


def kernel(x, mem, positions, w_in, b_forget, mla_q_gain, mla_kv_gain, mla_w_uq, mla_w_ukv, w_mix_out, ln_mix_g, ln_mix_b, xattn_w_q, xattn_w_kv, xattn_w_o, ln_mem_g, ln_mem_b, router_w, router_bias, expert_w_gate, expert_w_up, expert_w_down, ln_ffn_g, ln_ffn_b):
    B, S, D = x.shape
    depth = w_in.shape[0]
    t = _tiles(S)
    invf_a, sgn_a, invf_d, sgn_d = _rope_constants()
    cos_a, sin_a = _rope_tables(positions, invf_a, sgn_a, t["rope"])
    cos_d, sin_d = _rope_tables(positions, invf_d, sgn_d, t["rope"])
    tabs = (cos_a, sin_a, cos_d, sin_d)
    rw_t = router_w.T.astype(F32)
    rbias = router_bias.reshape(-1, 1).astype(F32)

    def row(v):
        return v.reshape(1, -1).astype(F32)

    for l in range(depth):
        wts = _layer_weights(w_in[l], b_forget[l], mla_q_gain[l], mla_kv_gain[l], mla_w_uq[l], mla_w_ukv[l])
        qa, ka, va_t, qb, kb, vb_t, qc, kc, vc = _proj(x, wts, tabs, t["proj"])
        oa_t = _flash_attention(qa, ka, va_t, MLA_HEADS, t["flash"], "mla_flash")
        ob_t = _flash_attention(qb, kb, vb_t, FOX_HEADS, t["flash"], "fox_flash")
        oc = _dilated_attention(qc, kc, vc)
        x = _mix_out(oa_t, ob_t, oc, x, w_mix_out[l].astype(BF16), row(ln_mix_g[l]), row(ln_mix_b[l]),
                     t["mix"])
        k_mem, v_mem = _mem_kv(mem, xattn_w_kv[l].astype(BF16))
        x = _xattn(x, k_mem, v_mem, xattn_w_q[l].astype(BF16), xattn_w_o[l].astype(BF16),
                   row(ln_mem_g[l]), row(ln_mem_b[l]), t["xattn"])
        x2d = _moe(x.reshape(B * S, D), rw_t, rbias, expert_w_gate[l].astype(BF16),
                   expert_w_up[l].astype(BF16), expert_w_down[l].astype(BF16),
                   row(ln_ffn_g[l]), row(ln_ffn_b[l]), t["moe"])
        x = x2d.reshape(B, S, D)
    return x
```

```python
import functools
import math

import numpy as np
import jax
import jax.numpy as jnp
from jax import lax
from jax.experimental import pallas as pl
from jax.experimental.pallas import tpu as pltpu

F32 = jnp.float32
BF16 = jnp.bfloat16
HIGHEST = lax.Precision.HIGHEST

D_MODEL = 1024
DEPTH = 4
MLA_HEADS = 4
MLA_Q_RANK = 256
MLA_KV_RANK = 128
MLA_NOPE_DIM = 64
MLA_ROPE_DIM = 32
MLA_V_DIM = 64
FOX_HEADS = 6
FOX_HEAD_DIM = 64
DIL_HEADS = 6
DIL_HEAD_DIM = 64
DILATED_PAIRS = ((128, 1), (512, 4), (2048, 16))
XATTN_HEADS = 4
N_EXPERTS = 16
N_GROUPS = 4
EXPERTS_PER_GROUP = N_EXPERTS // N_GROUPS
D_EXPERT = 256
ROPE_THETA = 10000.0
NORM_EPS = 1e-5
NEG_INF = -1e30
DEEPNORM_ALPHA = (2 * DEPTH) ** 0.25
LOG2E = math.log2(math.e)

LANES = 128
HEAD_LANES = 64
VMEM_LIMIT_BYTES = 52 * 1024 * 1024

A_CQ = 0
A_CKV = A_CQ + MLA_Q_RANK
A_KR = A_CKV + MLA_KV_RANK
A_KRR = A_KR + LANES
A_FF = A_KRR + LANES
A_COLS = A_FF + LANES


def _cparams(sem):
    return pltpu.CompilerParams(dimension_semantics=sem, vmem_limit_bytes=VMEM_LIMIT_BYTES)


def _nt_dot(a, b, precision=None):
    return lax.dot_general(a, b, (((1,), (1,)), ((), ())), precision=precision,
                           preferred_element_type=F32)


def _dot(a, b, precision=None):
    return jnp.dot(a, b, precision=precision, preferred_element_type=F32)


def _layer_norm(y, g, b):
    mu = jnp.mean(y, axis=-1, keepdims=True)
    yc = y - mu
    var = jnp.mean(yc * yc, axis=-1, keepdims=True)
    return yc * lax.rsqrt(var + NORM_EPS) * g + b


def _rms_norm(y, g):
    ms = jnp.mean(y * y, axis=-1, keepdims=True)
    return y * lax.rsqrt(ms + NORM_EPS) * g


def _rope_table_kernel(pos_ref, invf_ref, sgn_ref, cos_ref, sin_ref):
    ang = pos_ref[0].astype(F32) * invf_ref[...]
    cos_ref[0] = jnp.cos(ang)
    sin_ref[0] = jnp.sin(ang) * sgn_ref[...]


def _rope_tables(positions, invf, sgn, ts):
    B, S = positions.shape
    pos3 = positions.reshape(B, S, 1)
    spec = pl.BlockSpec((1, ts, LANES), lambda b, i: (b, i, 0))
    vec = pl.BlockSpec((1, LANES), lambda b, i: (0, 0))
    return pl.pallas_call(
        _rope_table_kernel,
        out_shape=(jax.ShapeDtypeStruct((B, S, LANES), F32),) * 2,
        grid=(B, S // ts),
        in_specs=[pl.BlockSpec((1, ts, 1), lambda b, i: (b, i, 0)), vec, vec],
        out_specs=(spec, spec),
        compiler_params=_cparams(("parallel", "parallel")),
        name="rope_tables",
    )(pos3, invf, sgn)


def _proj_kernel(x_ref, wa_ref, wfox_ref, wfoxv_ref, wdil_ref, qg_ref, kvg_ref, wuq_ref, wuk_ref,
                 wuv_ref, bf_ref, cosa_ref, sina_ref, cosd_ref, sind_ref,
                 qa_ref, ka_ref, va_ref, qb_ref, kb_ref, vb_ref,
                 qc_ref, kc_ref, vc_ref, carry_c, *dil_scs, tm):
    i = pl.program_id(1)
    xb = x_ref[0].astype(BF16)

    ha = _dot(xb, wa_ref[...])
    cqn = _rms_norm(ha[:, A_CQ:A_CQ + MLA_Q_RANK], qg_ref[...]).astype(BF16)
    ckvn = _rms_norm(ha[:, A_CKV:A_CKV + MLA_KV_RANK], kvg_ref[...]).astype(BF16)
    cos_a = cosa_ref[0]
    sin_a = sina_ref[0]
    q2 = _dot(cqn, wuq_ref[...])
    k_nope = _dot(ckvn, wuk_ref[...])
    k_rot = ha[:, A_KR:A_KR + LANES] * cos_a + ha[:, A_KRR:A_KRR + LANES] * sin_a
    q_scale = (MLA_NOPE_DIM + MLA_ROPE_DIM) ** -0.5 * LOG2E
    n_half = MLA_HEADS * LANES
    for h in range(MLA_HEADS):
        sl = slice(h * LANES, (h + 1) * LANES)
        qh = q2[:, sl] * cos_a + q2[:, n_half + h * LANES:n_half + (h + 1) * LANES] * sin_a
        qa_ref[0, :, sl] = (qh * q_scale).astype(BF16)
        ka_ref[0, :, sl] = (k_nope[:, sl] + k_rot).astype(BF16)
    va_ref[0] = _nt_dot(wuv_ref[...], ckvn).astype(BF16)

    lane = lax.broadcasted_iota(jnp.int32, (1, LANES), 1)
    bias_lanes = (lane >= FOX_HEAD_DIM) & (lane < FOX_HEAD_DIM + 3 * FOX_HEADS)
    z = ha[:, A_FF:A_FF + LANES] + bf_ref[...]
    logf = jnp.minimum(z, 0.0) - jnp.log(1.0 + jnp.exp(-jnp.abs(z)))
    logf = jnp.where(bias_lanes, logf, 0.0)

    @pl.when(i == 0)
    def _():
        carry_c[...] = jnp.zeros_like(carry_c)

    p0 = logf.astype(BF16).astype(F32)
    r0 = logf - p0
    p1 = r0.astype(BF16).astype(F32)
    p2 = (r0 - p1).astype(BF16)
    packed = jnp.concatenate([(p0 + pltpu.roll(p1, HEAD_LANES, 1)).astype(BF16), p2], axis=1)
    r = lax.broadcasted_iota(jnp.int32, (tm, tm), 0)
    c = lax.broadcasted_iota(jnp.int32, (tm, tm), 1)
    lower = (c <= r).astype(BF16)
    cum = _dot(lower, packed)
    cum_a = cum[:, :LANES]
    fcol = jnp.where(bias_lanes, cum_a + pltpu.roll(cum_a, HEAD_LANES, 1) + cum[:, LANES:], 0.0) + carry_c[...]
    carry_c[...] = fcol[tm - 1:tm, :]

    nb = fcol * (-LOG2E)
    hi = nb.astype(BF16).astype(F32)
    r1 = nb - hi
    mid = r1.astype(BF16).astype(F32)
    lo3 = (r1 - mid).astype(BF16).astype(F32)
    piece = (lane - FOX_HEAD_DIM) % 3
    k_bias = jnp.where(piece == 0, hi, jnp.where(piece == 1, mid, lo3))

    hf = _dot(xb, wfox_ref[...])
    nf = FOX_HEADS * FOX_HEAD_DIM
    dims = lane < FOX_HEAD_DIM
    fq_scale = FOX_HEAD_DIM ** -0.5 * LOG2E
    for h in range(FOX_HEADS):
        blk = slice((h // 2) * LANES, (h // 2 + 1) * LANES)
        out = slice(h * LANES, (h + 1) * LANES)
        qh = hf[:, blk]
        kh = hf[:, nf + (h // 2) * LANES:nf + (h // 2 + 1) * LANES]
        if h % 2:
            qh = pltpu.roll(qh, HEAD_LANES, 1)
            kh = pltpu.roll(kh, HEAD_LANES, 1)
        own = (lane >= FOX_HEAD_DIM + 3 * h) & (lane < FOX_HEAD_DIM + 3 * h + 3)
        qb_ref[0, :, out] = jnp.where(dims, qh * fq_scale, jnp.where(own, 1.0, 0.0)).astype(BF16)
        kb_ref[0, :, out] = jnp.where(dims, kh, k_bias).astype(BF16)
    vb_ref[0] = _nt_dot(wfoxv_ref[...], xb).astype(BF16)

    hd = _dot(xb, wdil_ref[...])
    cos_d = cosd_ref[0]
    sin_d = sind_ref[0]
    nd = DIL_HEADS * DIL_HEAD_DIM
    for pblk in range(nd // LANES):
        sl = slice(pblk * LANES, (pblk + 1) * LANES)
        qh = hd[:, sl]
        kh = hd[:, nd + pblk * LANES:nd + (pblk + 1) * LANES]
        qh = qh * cos_d + pltpu.roll(qh, HEAD_LANES, 1) * sin_d
        kh = kh * cos_d + pltpu.roll(kh, HEAD_LANES, 1) * sin_d
        dil_scs[pblk][...] = qh * (DIL_HEAD_DIM ** -0.5 * LOG2E)
        dil_scs[3 + pblk][...] = kh
        dil_scs[6 + pblk][...] = hd[:, 2 * nd + pblk * LANES:2 * nd + (pblk + 1) * LANES]
    for a, ref in enumerate((qc_ref, kc_ref, vc_ref)):
        for r in range(DIL_PLANES):
            rows = pl.ds(r, tm // DIL_PLANES, stride=DIL_PLANES)
            for pblk in range(nd // LANES):
                lanes = slice(r * nd + pblk * LANES, r * nd + (pblk + 1) * LANES)
                ref[0, :, lanes] = dil_scs[3 * a + pblk][rows, :].astype(BF16)


def _proj(x, wts, tabs, tm):
    B, S, D = x.shape
    cos_a, sin_a, cos_d, sin_d = tabs
    grid = (B, S // tm)

    def full(a):
        return pl.BlockSpec(a.shape, lambda b, i: (0,) * a.ndim)

    def tok(width, dtype):
        return (jax.ShapeDtypeStruct((B, S, width), dtype),
                pl.BlockSpec((1, tm, width), lambda b, i: (b, i, 0)))

    tab = pl.BlockSpec((1, tm, LANES), lambda b, i: (b, i, 0))
    nd = DIL_HEADS * DIL_HEAD_DIM

    def planes(dtype):
        return (jax.ShapeDtypeStruct((B, S // DIL_PLANES, DIL_PLANES * nd), dtype),
                pl.BlockSpec((1, tm // DIL_PLANES, DIL_PLANES * nd), lambda b, i: (b, i, 0)))

    def tok_t(height, dtype):
        return (jax.ShapeDtypeStruct((B, height, S), dtype),
                pl.BlockSpec((1, height, tm), lambda b, i: (b, 0, i)))

    outs = [tok(MLA_HEADS * LANES, BF16), tok(MLA_HEADS * LANES, BF16),
            tok_t(MLA_HEADS * MLA_V_DIM, BF16),
            tok(FOX_HEADS * LANES, BF16), tok(FOX_HEADS * LANES, BF16),
            tok_t(FOX_HEADS * FOX_HEAD_DIM, BF16),
            planes(BF16), planes(BF16), planes(BF16)]
    w_list = [wts["wa"], wts["wfox"], wts["wfoxv_t"], wts["wdil"], wts["qg"], wts["kvg"], wts["wuq"],
              wts["wuk"], wts["wuv_t"], wts["bf"]]
    return pl.pallas_call(
        functools.partial(_proj_kernel, tm=tm),
        out_shape=tuple(o[0] for o in outs),
        grid=grid,
        in_specs=[pl.BlockSpec((1, tm, D), lambda b, i: (b, i, 0))] + [full(w) for w in w_list]
        + [tab, tab, tab, tab],
        out_specs=tuple(o[1] for o in outs),
        scratch_shapes=[pltpu.VMEM((1, LANES), F32)] + [pltpu.VMEM((tm, LANES), F32)] * (3 * nd // LANES),
        compiler_params=_cparams(("parallel", "arbitrary")),
        name="proj",
    )(x, *w_list, cos_a, sin_a, cos_d, sin_d)


FLASH_CHUNK = 32


FLASH_KEY_SPLIT = 2
ONES_ROWS = 16


def _flash_kernel(q0_ref, q1_ref, k0_ref, k1_ref, vt_ref, o_ref, *scratch, tq):
    i = pl.program_id(2)
    q_refs = (q0_ref, q1_ref)
    k_refs = (k0_ref, k1_ref)
    n_chain = 2 * FLASH_KEY_SPLIT
    s_scs = scratch[:n_chain]
    p_scs = scratch[n_chain:2 * n_chain]
    m_sc, l_sc, acc0_sc, acc1_sc = scratch[2 * n_chain:]
    acc_scs = (acc0_sc, acc1_sc)
    dv = HEAD_LANES
    tk = tq // FLASH_KEY_SPLIT
    n_chunks = tk // FLASH_CHUNK

    m_sc[...] = jnp.full_like(m_sc, NEG_INF)
    l_sc[...] = jnp.zeros_like(l_sc)
    acc0_sc[...] = jnp.zeros_like(acc0_sc)
    acc1_sc[...] = jnp.zeros_like(acc1_sc)

    def step(j, masked):
        m_locs = []
        for h in range(2):
            for c in range(FLASH_KEY_SPLIT):
                ks = pl.multiple_of(j * tq + c * tk, tk)
                k = k_refs[h][0, pl.ds(ks, tk), :]
                st = _nt_dot(k, q_refs[h][0])
                if masked:
                    key = lax.broadcasted_iota(jnp.int32, (tk, tq), 0) + c * tk
                    qry = lax.broadcasted_iota(jnp.int32, (tk, tq), 1)
                    st = jnp.where(key <= qry, st, NEG_INF)
                s_scs[h * FLASH_KEY_SPLIT + c][...] = st
                m8 = jnp.max(st.reshape(tk // 8, 8, tq), axis=0)
                m_locs.append(jnp.max(m8, axis=0, keepdims=True))
        locs = []
        for h in range(2):
            for c in range(FLASH_KEY_SPLIT):
                ks = pl.multiple_of(j * tq + c * tk, tk)
                s_sc = s_scs[h * FLASH_KEY_SPLIT + c]
                p_sc = p_scs[h * FLASH_KEY_SPLIT + c]
                m_loc = m_locs[h * FLASH_KEY_SPLIT + c]
                for r in range(n_chunks):
                    rows = slice(r * FLASH_CHUNK, (r + 1) * FLASH_CHUNK)
                    p_sc[rows, :] = jnp.exp2(s_sc[rows, :] - m_loc).astype(BF16)
                vt = vt_ref[0, h * dv:(h + 1) * dv, pl.ds(ks, tk)]
                vt_ext = jnp.concatenate([vt, jnp.ones((ONES_ROWS, tk), BF16)], axis=0)
                pvl = _dot(vt_ext, p_sc[...])
                locs.append((m_loc, pvl))
        for h in range(2):
            mine = locs[h * FLASH_KEY_SPLIT:(h + 1) * FLASH_KEY_SPLIT]
            m_prev = m_sc[h]
            m_new = m_prev
            for m_loc, _ in mine:
                m_new = jnp.maximum(m_new, m_loc)
            a_prev = jnp.exp2(m_prev - m_new)
            acc = acc_scs[h][...] * a_prev
            l = l_sc[h] * a_prev
            for m_loc, pvl in mine:
                a_loc = jnp.exp2(m_loc - m_new)
                acc = acc + pvl[:dv] * a_loc
                l = l + pvl[dv:dv + 1] * a_loc
            acc_scs[h][...] = acc
            l_sc[h] = l
            m_sc[h] = m_new

    def body(jj, carry):
        step(2 * jj, False)
        step(2 * jj + 1, False)
        return carry

    lax.fori_loop(0, i // 2, body, 0)

    @pl.when(i % 2 == 1)
    def _():
        step(i - 1, False)

    step(i, True)
    for h in range(2):
        o_ref[0, h * dv:(h + 1) * dv, :] = (acc_scs[h][...] / l_sc[h]).astype(o_ref.dtype)


def _flash_attention(q, k, vt, n_heads, tq, name):
    B, S, _ = q.shape
    n_pairs = n_heads // 2
    dv2 = 2 * HEAD_LANES
    qspec0 = pl.BlockSpec((1, tq, LANES), lambda b, p, i: (b, i, 2 * p))
    qspec1 = pl.BlockSpec((1, tq, LANES), lambda b, p, i: (b, i, 2 * p + 1))
    kspec0 = pl.BlockSpec((1, S, LANES), lambda b, p, i: (b, 0, 2 * p))
    kspec1 = pl.BlockSpec((1, S, LANES), lambda b, p, i: (b, 0, 2 * p + 1))
    vspec = pl.BlockSpec((1, dv2, S), lambda b, p, i: (b, p, 0))
    return pl.pallas_call(
        functools.partial(_flash_kernel, tq=tq),
        out_shape=jax.ShapeDtypeStruct((B, n_heads * HEAD_LANES, S), BF16),
        grid=(B, n_pairs, S // tq),
        in_specs=[qspec0, qspec1, kspec0, kspec1, vspec],
        out_specs=pl.BlockSpec((1, dv2, tq), lambda b, p, i: (b, p, i)),
        scratch_shapes=[pltpu.VMEM((tq // FLASH_KEY_SPLIT, tq), F32)] * (2 * FLASH_KEY_SPLIT)
        + [pltpu.VMEM((tq // FLASH_KEY_SPLIT, tq), BF16)] * (2 * FLASH_KEY_SPLIT)
        + [pltpu.VMEM((2, 1, tq), F32), pltpu.VMEM((2, 1, tq), F32),
                        pltpu.VMEM((HEAD_LANES, tq), F32), pltpu.VMEM((HEAD_LANES, tq), F32)],
        compiler_params=_cparams(("parallel", "parallel", "arbitrary")),
        name=name,
    )(q, q, k, k, vt)


DIL_PLANES = 16
DIL_TQ = 128
DIL_UNROLL = 4


def _plane_slot(r16):
    return 4 * (r16 % 4) + r16 // 4


def _dilated_geometry(dil):
    planes = DIL_PLANES // dil
    rows = DIL_TQ // planes
    if dil == 1:
        order = [(c // 4) + 4 * (c % 4) for c in range(planes)]
    else:
        order = list(range(planes))
    return planes, rows, order


def _dilated_bias(dil, first):
    planes, rows, order = _dilated_geometry(dil)
    krows = 2 * rows
    a = np.arange(planes * rows)
    b = np.arange(planes * krows)
    uq = np.asarray(order)[a // rows] + planes * (a % rows + (0 if first else rows))
    uk = np.asarray(order)[b // krows] + planes * (b % krows)
    diff = uq[:, None] - uk[None, :]
    valid = (diff >= 0) & (diff <= DIL_TQ)
    return jnp.asarray(np.where(valid, 0.0, NEG_INF), F32)


def _dilated_kernel(q_ref, k_ref, v_ref, b1f_ref, b1r_ref, b4f_ref, b4r_ref, b16f_ref, b16r_ref, o_ref,
                    q_st, k_st, v_st, m_sc, l_sc, acc_sc, *, rows_per_plane):
    pair = pl.program_id(1)
    n_pairs = DIL_HEADS // 2
    lane = lax.broadcasted_iota(jnp.int32, (1, LANES), 1)
    lo = lane < HEAD_LANES
    q_lo = (lane % HEAD_LANES) < (HEAD_LANES // 2)
    tq = DIL_TQ

    def plane_lanes(r16):
        return pl.ds(pl.multiple_of((r16 * n_pairs + pair) * LANES, LANES), LANES)

    for r16 in range(DIL_PLANES):
        slot = _plane_slot(r16)
        q_st[slot] = q_ref[0, :, plane_lanes(r16)].astype(F32)
        k_st[slot] = k_ref[0, :, plane_lanes(r16)].astype(F32)
        v_st[slot] = v_ref[0, :, plane_lanes(r16)].astype(F32)
    m_sc[...] = jnp.full_like(m_sc, NEG_INF)
    l_sc[...] = jnp.zeros_like(l_sc)
    acc_sc[...] = jnp.zeros_like(acc_sc)

    def block(planes, rows, slot0, row0, first, bias_ref):
        qsl = (pl.ds(slot0, planes), pl.ds(row0, rows), slice(None))
        k_row0 = row0 if first else row0 - rows
        ksl = (pl.ds(slot0, planes), pl.ds(k_row0, 2 * rows), slice(None))
        q = q_st[qsl].reshape(tq, LANES)
        k = k_st[ksl].reshape(-1, LANES).astype(BF16)
        v = v_st[ksl].reshape(-1, LANES).astype(BF16)
        zero = jnp.zeros_like(q)
        qs = (jnp.where(q_lo, q, zero).astype(BF16), jnp.where(q_lo, zero, q).astype(BF16))
        bias = bias_ref[...]
        v_ext = jnp.concatenate([v, jnp.ones_like(v)], axis=1)
        maxes, sums, pvs = [], [], []
        for h in range(2):
            s = _nt_dot(qs[h], k) + bias
            mh = jnp.max(s, axis=1, keepdims=True)
            p = jnp.exp2(s - mh)
            maxes.append(mh)
            pvl = _dot(p.astype(BF16), v_ext)
            pvs.append(pvl[:, :LANES])
            sums.append(pvl[:, LANES:])
        m_blk = jnp.where(lo, maxes[0], maxes[1])
        m_old = m_sc[qsl].reshape(tq, LANES)
        m_new = jnp.maximum(m_old, m_blk)
        a_old = jnp.exp2(m_old - m_new)
        a_blk = jnp.exp2(m_blk - m_new)
        l_new = a_old * l_sc[qsl].reshape(tq, LANES) + a_blk * jnp.where(lo, sums[0], sums[1])
        acc_new = a_old * acc_sc[qsl].reshape(tq, LANES) + a_blk * jnp.where(lo, pvs[0], pvs[1])
        m_sc[qsl] = m_new.reshape(planes, rows, LANES)
        l_sc[qsl] = l_new.reshape(planes, rows, LANES)
        acc_sc[qsl] = acc_new.reshape(planes, rows, LANES)

    bias_refs = {1: (b1f_ref, b1r_ref), 4: (b4f_ref, b4r_ref), 16: (b16f_ref, b16r_ref)}
    for window, dil in DILATED_PAIRS:
        assert window // dil == tq
        planes, rows, _ = _dilated_geometry(dil)
        nblk = rows_per_plane // rows
        bf_ref, br_ref = bias_refs[dil]

        def first_body(sub, carry, planes=planes, rows=rows, bf_ref=bf_ref):
            block(planes, rows, sub * planes, 0, True, bf_ref)
            return carry

        lax.fori_loop(0, dil, first_body, 0, unroll=min(dil, DIL_UNROLL))

        if nblk > 1:
            def rest_body(t, carry, planes=planes, rows=rows, nblk=nblk, br_ref=br_ref):
                sub = t // (nblk - 1)
                blk = t % (nblk - 1) + 1
                block(planes, rows, sub * planes, pl.multiple_of(blk * rows, rows), False, br_ref)
                return carry

            lax.fori_loop(0, dil * (nblk - 1), rest_body, 0, unroll=DIL_UNROLL)

    for r16 in range(DIL_PLANES):
        slot = _plane_slot(r16)
        o_ref[0, :, plane_lanes(r16)] = (acc_sc[slot] / l_sc[slot]).astype(o_ref.dtype)


def _dilated_attention(qc, kc, vc):
    B, rpp, width = qc.shape
    n_pairs = DIL_HEADS // 2
    assert rpp % DIL_TQ == 0 and rpp >= 2 * DIL_TQ

    spec = pl.BlockSpec((1, rpp, width), lambda b, p: (b, 0, 0))
    biases = [_dilated_bias(d, f) for _, d in DILATED_PAIRS for f in (True, False)]
    bspecs = [pl.BlockSpec(bb.shape, lambda b, p: (0, 0)) for bb in biases]
    plane = pltpu.VMEM((DIL_PLANES, rpp, LANES), F32)
    return pl.pallas_call(
        functools.partial(_dilated_kernel, rows_per_plane=rpp),
        out_shape=jax.ShapeDtypeStruct((B, rpp, width), BF16),
        grid=(B, n_pairs),
        in_specs=[spec, spec, spec] + bspecs,
        out_specs=spec,
        scratch_shapes=[plane] * 6,
        compiler_params=_cparams(("parallel", "arbitrary")),
        name="dilated",
    )(qc, kc, vc, *biases)


def _tn_dot(a_t, b):
    return lax.dot_general(a_t, b, (((0,), (0,)), ((), ())), preferred_element_type=F32)


def _mix_out_kernel(oat_ref, obt_ref, oc_ref, x_ref, wa_ref, wb_ref, wc_ref, g_ref, b_ref, o_ref, *oc_scs):
    nd = len(oc_scs) * LANES
    rows_per_plane = oc_scs[0].shape[0] // DIL_PLANES
    for r in range(DIL_PLANES):
        rows = pl.ds(r, rows_per_plane, stride=DIL_PLANES)
        for pblk, sc in enumerate(oc_scs):
            lanes = slice(r * nd + pblk * LANES, r * nd + (pblk + 1) * LANES)
            sc[rows, :] = oc_ref[0, :, lanes].astype(F32)
    oc = jnp.concatenate([sc[...] for sc in oc_scs], axis=1).astype(BF16)
    y = (_tn_dot(oat_ref[0], wa_ref[...]) + _tn_dot(obt_ref[0], wb_ref[...])
         + _dot(oc, wc_ref[...]))
    o_ref[0] = _layer_norm(DEEPNORM_ALPHA * x_ref[0] + y, g_ref[...], b_ref[...])


def _mix_out(oa_t, ob_t, oc, x, w_out, g, b, tm):
    B, S, D = x.shape
    na, nb = oa_t.shape[1], ob_t.shape[1]
    wa, wb, wc = w_out[:na], w_out[na:na + nb], w_out[na + nb:]

    def cols(a):
        return pl.BlockSpec((1, a.shape[1], tm), lambda bb, i: (bb, 0, i))

    def rows(a):
        return pl.BlockSpec((1, tm, a.shape[2]), lambda bb, i: (bb, i, 0))

    def full(a):
        return pl.BlockSpec(a.shape, lambda bb, i: (0, 0))

    return pl.pallas_call(
        _mix_out_kernel,
        out_shape=jax.ShapeDtypeStruct((B, S, D), F32),
        grid=(B, S // tm),
        in_specs=[cols(oa_t), cols(ob_t),
                  pl.BlockSpec((1, tm // DIL_PLANES, oc.shape[2]), lambda bb, i: (bb, i, 0)),
                  rows(x), full(wa), full(wb), full(wc), full(g), full(b)],
        out_specs=rows(x),
        scratch_shapes=[pltpu.VMEM((tm, LANES), F32)] * (oc.shape[2] // DIL_PLANES // LANES),
        compiler_params=_cparams(("parallel", "parallel")),
        name="mix_out",
    )(oa_t, ob_t, oc, x, wa, wb, wc, g, b)


def _mem_kv_kernel(mem_ref, w_ref, k_ref, v_ref):
    kv = _dot(mem_ref[0].astype(BF16), w_ref[...])
    d = k_ref.shape[-1]
    k_ref[0] = kv[:, :d].astype(BF16)
    v_ref[0] = kv[:, d:].astype(BF16)


def _mem_kv(mem, w_kv):
    B, M, D = mem.shape
    spec = pl.BlockSpec((1, M, D), lambda b: (b, 0, 0))
    return pl.pallas_call(
        _mem_kv_kernel,
        out_shape=(jax.ShapeDtypeStruct((B, M, D), BF16),) * 2,
        grid=(B,),
        in_specs=[spec, pl.BlockSpec(w_kv.shape, lambda b: (0, 0))],
        out_specs=(spec, spec),
        compiler_params=_cparams(("parallel",)),
        name="mem_kv",
    )(mem, w_kv)


def _xattn_kernel(x_ref, k_ref, v_ref, wq_ref, wo_ref, g_ref, b_ref, o_ref):
    x = x_ref[0]
    d = x.shape[-1]
    dh = d // XATTN_HEADS
    q = (_dot(x.astype(BF16), wq_ref[...]) * (dh ** -0.5)).astype(BF16)
    outs = []
    for h in range(XATTN_HEADS):
        sl = slice(h * dh, (h + 1) * dh)
        s = _nt_dot(q[:, sl], k_ref[0, :, sl])
        p = jnp.exp(s - jnp.max(s, axis=1, keepdims=True))
        o = _dot(p.astype(BF16), v_ref[0, :, sl])
        outs.append((o / jnp.sum(p, axis=1, keepdims=True)).astype(BF16))
    o_all = jnp.concatenate(outs, axis=1)
    y = _dot(o_all, wo_ref[...])
    o_ref[0] = _layer_norm(DEEPNORM_ALPHA * x + y, g_ref[...], b_ref[...])


def _xattn(x, k_mem, v_mem, w_q, w_o, g, b, tm):
    B, S, D = x.shape
    M = k_mem.shape[1]
    xspec = pl.BlockSpec((1, tm, D), lambda bb, i: (bb, i, 0))
    mspec = pl.BlockSpec((1, M, D), lambda bb, i: (bb, 0, 0))

    def full(a):
        return pl.BlockSpec(a.shape, lambda bb, i: (0, 0))

    return pl.pallas_call(
        _xattn_kernel,
        out_shape=jax.ShapeDtypeStruct((B, S, D), F32),
        grid=(B, S // tm),
        in_specs=[xspec, mspec, mspec, full(w_q), full(w_o), full(g), full(b)],
        out_specs=xspec,
        compiler_params=_cparams(("parallel", "parallel")),
        name="xattn",
    )(x, k_mem, v_mem, w_q, w_o, g, b)


def _split_bf16(a):
    hi = a.astype(BF16)
    return hi, (a - hi.astype(F32)).astype(BF16)


def _router_gates_t(x, rw_t, rbias):
    x_hi, x_lo = _split_bf16(x)
    w_hi, w_lo = _split_bf16(rw_t)
    t1 = _nt_dot(jnp.concatenate([w_hi, w_lo], axis=0), x_hi)
    logits_t = t1[:N_EXPERTS] + (t1[N_EXPERTS:] + _nt_dot(w_hi, x_lo))
    scores = 1.0 / (1.0 + jnp.exp(-logits_t))
    biased = scores + rbias
    sc = [scores[e:e + 1, :] for e in range(N_EXPERTS)]
    bs = [biased[e:e + 1, :] for e in range(N_EXPERTS)]
    epg = EXPERTS_PER_GROUP

    def beats(a, b, a_first):
        return (a >= b) if a_first else (a > b)

    grp = []
    for gi in range(N_GROUPS):
        v = bs[gi * epg:(gi + 1) * epg]
        best = None
        for a in range(epg):
            for b in range(a + 1, epg):
                pair = v[a] + v[b]
                best = pair if best is None else jnp.maximum(best, pair)
        grp.append(best)
    gates, g_sels = [], []
    for gi in range(N_GROUPS):
        g_sel = None
        for gj in range(N_GROUPS):
            if gj == gi:
                continue
            w = beats(grp[gi], grp[gj], gi < gj)
            g_sel = w if g_sel is None else (g_sel & w)
        g_sels.append(g_sel)
        in_top = []
        for a in range(epg):
            ea = gi * epg + a
            n_above = jnp.zeros(bs[ea].shape, jnp.int32)
            for b in range(epg):
                if b == a:
                    continue
                eb = gi * epg + b
                n_above = n_above + beats(bs[eb], bs[ea], b < a).astype(jnp.int32)
            in_top.append(g_sel & (n_above < 2))
        denom = None
        for a in range(epg):
            term = jnp.where(in_top[a], sc[gi * epg + a], 0.0)
            denom = term if denom is None else denom + term
        denom = jnp.where(g_sel, denom, 1.0)
        for a in range(epg):
            gates.append(jnp.where(in_top[a], sc[gi * epg + a] / denom, 0.0))
    return gates, g_sels


def _moe_kernel(x_ref, rwt_ref, rb_ref, wg_ref, wu_ref, wd_ref, g_ref, b_ref, o_ref,
                gt_sc, sel_sc, rank_sc, acc_sc, xb_sc, *, tm):
    grp = pl.program_id(1)
    epg = EXPERTS_PER_GROUP

    @pl.when(grp == 0)
    def _():
        ts = min(tm, ROUTER_ROWS)
        before = (lax.broadcasted_iota(jnp.int32, (ts, ts), 0)
                  < lax.broadcasted_iota(jnp.int32, (ts, ts), 1)).astype(BF16)
        seen = jnp.zeros((8, 1), F32)
        for r in range(tm // ts):
            cols = slice(r * ts, (r + 1) * ts)
            x = x_ref[cols, :]
            gates, g_sels = _router_gates_t(x, rwt_ref[...], rb_ref[...])
            zeros4 = [jnp.zeros((1, ts), F32)] * (8 - epg)
            for gi in range(N_GROUPS):
                gt_sc[gi, :, cols] = jnp.concatenate(gates[gi * epg:(gi + 1) * epg] + zeros4, axis=0)
            sel8 = jnp.concatenate([s.astype(F32) for s in g_sels] + [jnp.zeros((1, ts), F32)] * (8 - N_GROUPS),
                                   axis=0)
            sel_sc[:, cols] = sel8
            rank_sc[:, cols] = _dot(sel8.astype(BF16), before) + seen
            seen = seen + jnp.sum(sel8, axis=1, keepdims=True)
            xb_sc[cols, :] = x.astype(BF16)
        acc_sc[...] = jnp.zeros_like(acc_sc)

    sel = sel_sc[pl.ds(grp, 1), :] > 0.5
    rank = rank_sc[pl.ds(grp, 1), :].astype(jnp.int32)
    n_tok = jnp.sum(sel_sc[pl.ds(grp, 1), :]).astype(jnp.int32)
    gt = gt_sc[grp]
    g_hi = gt.astype(BF16).astype(F32)
    g_r = gt - g_hi
    g_mid = g_r.astype(BF16).astype(F32)
    g_lo = (g_r - g_mid).astype(BF16).astype(F32)
    g3 = jnp.concatenate([g_hi, g_mid, g_lo, jnp.zeros_like(g_hi)], axis=0).astype(BF16)
    wd_all = wd_ref[...].reshape(epg * D_EXPERT, -1)

    def chunk(first_slot, ch):
        slot = lax.broadcasted_iota(jnp.int32, (ch, tm), 0) + first_slot
        onehot = jnp.where((slot == rank) & sel, 1.0, 0.0).astype(BF16)
        xg = _dot(onehot, xb_sc[...]).astype(BF16)
        gs3 = _nt_dot(onehot, g3)
        gs = gs3[:, 0:8] + (gs3[:, 8:16] + gs3[:, 16:24])
        acts = []
        for e in range(epg):
            hg = _dot(xg, wg_ref[e])
            hu = _dot(xg, wu_ref[e])
            acts.append(((hg / (1.0 + jnp.exp(-hg))) * hu * gs[:, e:e + 1]).astype(BF16))
        y = _dot(jnp.concatenate(acts, axis=1), wd_all).astype(BF16)
        acc_sc[...] += _tn_dot(onehot, y)

    n_wide = (n_tok + (MOE_CHUNK - MOE_TAIL_CHUNK - 1)) // MOE_CHUNK

    def wide(c, carry):
        chunk(c * MOE_CHUNK, MOE_CHUNK)
        return carry

    lax.fori_loop(0, n_wide, wide, 0)

    @pl.when(n_tok > n_wide * MOE_CHUNK)
    def _():
        chunk(n_wide * MOE_CHUNK, MOE_TAIL_CHUNK)

    @pl.when(grp == N_GROUPS - 1)
    def _():
        o_ref[...] = _layer_norm(DEEPNORM_ALPHA * x_ref[...] + acc_sc[...], g_ref[...], b_ref[...])


ROUTER_ROWS = 512
MOE_CHUNK = 256
MOE_TAIL_CHUNK = 128


def _moe(x2d, rw_t, rbias, wg, wu, wd, g, b, tm):
    T, D = x2d.shape
    xspec = pl.BlockSpec((tm, D), lambda i, e: (i, 0))
    epg = EXPERTS_PER_GROUP

    def full(a):
        return pl.BlockSpec(a.shape, lambda i, e: (0, 0))

    return pl.pallas_call(
        functools.partial(_moe_kernel, tm=tm),
        out_shape=jax.ShapeDtypeStruct((T, D), F32),
        grid=(T // tm, N_GROUPS),
        in_specs=[xspec, full(rw_t), full(rbias),
                  pl.BlockSpec((epg, D, D_EXPERT), lambda i, e: (e, 0, 0)),
                  pl.BlockSpec((epg, D, D_EXPERT), lambda i, e: (e, 0, 0)),
                  pl.BlockSpec((epg, D_EXPERT, D), lambda i, e: (e, 0, 0)),
                  full(g), full(b)],
        out_specs=xspec,
        scratch_shapes=[pltpu.VMEM((N_GROUPS, 8, tm), F32), pltpu.VMEM((8, tm), F32), pltpu.VMEM((8, tm), F32),
                        pltpu.VMEM((tm, D), F32), pltpu.VMEM((tm, D), BF16)],
        compiler_params=_cparams(("parallel", "arbitrary")),
        name="moe",
    )(x2d, rw_t, rbias, wg, wu, wd, g, b)


def _rope_constants():
    half_d = DIL_HEAD_DIM // 2
    inv_d = ROPE_THETA ** (-jnp.arange(half_d, dtype=F32) / half_d)
    half_a = MLA_ROPE_DIM // 2
    inv_a = ROPE_THETA ** (-jnp.arange(half_a, dtype=F32) / half_a)
    lane = np.arange(LANES)
    invf_d = inv_d[lane % half_d].reshape(1, LANES)
    sgn_d = jnp.asarray(np.where(lane < HEAD_LANES, -1.0, 1.0).reshape(1, LANES), F32)
    in_rope = (lane >= MLA_NOPE_DIM) & (lane < MLA_NOPE_DIM + MLA_ROPE_DIM)
    invf_a = jnp.where(jnp.asarray(in_rope), inv_a[(lane - MLA_NOPE_DIM) % half_a], 0.0).reshape(1, LANES)
    sgn_a = jnp.ones((1, LANES), F32)
    return invf_a, sgn_a, invf_d, sgn_d


def _layer_weights(w_in, b_forget, q_gain, kv_gain, w_uq, w_ukv):
    D = w_in.shape[0]
    cuts = np.cumsum([MLA_Q_RANK, MLA_KV_RANK, MLA_ROPE_DIM, 3 * FOX_HEADS * FOX_HEAD_DIM, FOX_HEADS])
    w_cq, w_ckv, w_kr, w_fox, w_ff, w_dil = jnp.split(w_in, cuts.tolist(), axis=1)
    half = MLA_ROPE_DIM // 2
    w_kr_rot = jnp.concatenate([-w_kr[:, half:], w_kr[:, :half]], axis=1)

    def place_rope(w):
        return jnp.pad(w, ((0, 0), (MLA_NOPE_DIM, LANES - MLA_NOPE_DIM - MLA_ROPE_DIM)))

    def place_gate(w):
        rep = jnp.repeat(w, 3, axis=1)
        return jnp.pad(rep, ((0, 0), (FOX_HEAD_DIM, LANES - FOX_HEAD_DIM - 3 * FOX_HEADS)))

    wa = jnp.concatenate([w_cq, w_ckv, place_rope(w_kr), place_rope(w_kr_rot),
                          place_gate(w_ff)], axis=1)

    nd = DIL_HEADS * DIL_HEAD_DIM

    def pair_layout(w):
        half = DIL_HEAD_DIM // 2
        w5 = w.reshape(D, DIL_HEADS // 2, 2, 2, half)
        return w5.transpose(0, 1, 3, 2, 4).reshape(D, nd)

    wdil = jnp.concatenate([pair_layout(w_dil[:, :nd]), pair_layout(w_dil[:, nd:2 * nd]), w_dil[:, 2 * nd:]],
                           axis=1)

    dq = MLA_NOPE_DIM + MLA_ROPE_DIM
    uq = w_uq.reshape(MLA_Q_RANK, MLA_HEADS, dq)
    uq_rope = uq[:, :, MLA_NOPE_DIM:]
    uq_rot = jnp.concatenate([-uq_rope[:, :, half:], uq_rope[:, :, :half]], axis=2)
    pad_tail = LANES - dq
    uq_plain = jnp.pad(uq, ((0, 0), (0, 0), (0, pad_tail))).reshape(MLA_Q_RANK, MLA_HEADS * LANES)
    uq_rotp = jnp.pad(uq_rot, ((0, 0), (0, 0), (MLA_NOPE_DIM, pad_tail))).reshape(MLA_Q_RANK, MLA_HEADS * LANES)
    wuq = jnp.concatenate([uq_plain, uq_rotp], axis=1)

    ukv = w_ukv.reshape(MLA_KV_RANK, MLA_HEADS, MLA_NOPE_DIM + MLA_V_DIM)
    wuk = jnp.pad(ukv[:, :, :MLA_NOPE_DIM], ((0, 0), (0, 0), (0, LANES - MLA_NOPE_DIM)))
    wuk = wuk.reshape(MLA_KV_RANK, MLA_HEADS * LANES)
    wuv = ukv[:, :, MLA_NOPE_DIM:].reshape(MLA_KV_RANK, MLA_HEADS * MLA_V_DIM)

    bf = place_gate(b_forget.astype(F32).reshape(1, -1))
    nf = FOX_HEADS * FOX_HEAD_DIM
    return dict(wa=wa.astype(BF16), wfox=w_fox[:, :2 * nf].astype(BF16),
                wfoxv_t=w_fox[:, 2 * nf:].T.astype(BF16), wdil=wdil.astype(BF16),
                qg=q_gain.reshape(1, -1).astype(F32), kvg=kv_gain.reshape(1, -1).astype(F32),
                wuq=wuq.astype(BF16), wuk=wuk.astype(BF16), wuv_t=wuv.T.astype(BF16), bf=bf)


def _tiles(S):
    tm = min(512, S)
    return dict(rope=min(512, S), proj=tm, flash=min(512, S), mix=tm, xattn=tm, moe=min(1024, S))


def kernel(x, mem, positions, w_in, b_forget, mla_q_gain, mla_kv_gain, mla_w_uq, mla_w_ukv, w_mix_out, ln_mix_g, ln_mix_b, xattn_w_q, xattn_w_kv, xattn_w_o, ln_mem_g, ln_mem_b, router_w, router_bias, expert_w_gate, expert_w_up, expert_w_down, ln_ffn_g, ln_ffn_b):
    B, S, D = x.shape
    depth = w_in.shape[0]
    t = _tiles(S)
    invf_a, sgn_a, invf_d, sgn_d = _rope_constants()
    cos_a, sin_a = _rope_tables(positions, invf_a, sgn_a, t["rope"])
    cos_d, sin_d = _rope_tables(positions, invf_d, sgn_d, t["rope"])
    tabs = (cos_a, sin_a, cos_d, sin_d)
    rw_t = router_w.T.astype(F32)
    rbias = router_bias.reshape(-1, 1).astype(F32)

    def row(v):
        return v.reshape(1, -1).astype(F32)

    for l in range(depth):
        wts = _layer_weights(w_in[l], b_forget[l], mla_q_gain[l], mla_kv_gain[l], mla_w_uq[l], mla_w_ukv[l])
        qa, ka, va_t, qb, kb, vb_t, qc, kc, vc = _proj(x, wts, tabs, t["proj"])
        oa_t = _flash_attention(qa, ka, va_t, MLA_HEADS, t["flash"], "mla_flash")
        ob_t = _flash_attention(qb, kb, vb_t, FOX_HEADS, t["flash"], "fox_flash")
        oc = _dilated_attention(qc, kc, vc)
        x = _mix_out(oa_t, ob_t, oc, x, w_mix_out[l].astype(BF16), row(ln_mix_g[l]), row(ln_mix_b[l]),
                     t["mix"])
        k_mem, v_mem = _mem_kv(mem, xattn_w_kv[l].astype(BF16))
        x = _xattn(x, k_mem, v_mem, xattn_w_q[l].astype(BF16), xattn_w_o[l].astype(BF16),
                   row(ln_mem_g[l]), row(ln_mem_b[l]), t["xattn"])
        x2d = _moe(x.reshape(B * S, D), rw_t, rbias, expert_w_gate[l].astype(BF16),
                   expert_w_up[l].astype(BF16), expert_w_down[l].astype(BF16),
                   row(ln_ffn_g[l]), row(ln_ffn_b[l]), t["moe"])
        x = x2d.reshape(B, S, D)
    return x
```

```python
import functools
import math

import numpy as np
import jax
import jax.numpy as jnp
from jax import lax
from jax.experimental import pallas as pl
from jax.experimental.pallas import tpu as pltpu

F32 = jnp.float32
BF16 = jnp.bfloat16
HIGHEST = lax.Precision.HIGHEST

D_MODEL = 1024
DEPTH = 4
MLA_HEADS = 4
MLA_Q_RANK = 256
MLA_KV_RANK = 128
MLA_NOPE_DIM = 64
MLA_ROPE_DIM = 32
MLA_V_DIM = 64
FOX_HEADS = 6
FOX_HEAD_DIM = 64
DIL_HEADS = 6
DIL_HEAD_DIM = 64
DILATED_PAIRS = ((128, 1), (512, 4), (2048, 16))
XATTN_HEADS = 4
N_EXPERTS = 16
N_GROUPS = 4
EXPERTS_PER_GROUP = N_EXPERTS // N_GROUPS
D_EXPERT = 256
ROPE_THETA = 10000.0
NORM_EPS = 1e-5
NEG_INF = -1e30
DEEPNORM_ALPHA = (2 * DEPTH) ** 0.25
LOG2E = math.log2(math.e)

LANES = 128
HEAD_LANES = 64
VMEM_LIMIT_BYTES = 52 * 1024 * 1024

A_CQ = 0
A_CKV = A_CQ + MLA_Q_RANK
A_KR = A_CKV + MLA_KV_RANK
A_KRR = A_KR + LANES
A_FF = A_KRR + LANES
A_COLS = A_FF + LANES


def _cparams(sem):
    return pltpu.CompilerParams(dimension_semantics=sem, vmem_limit_bytes=VMEM_LIMIT_BYTES)


def _layer_slab(stacked, l):
    zeros = (0,) * (stacked.ndim - 1)
    return pl.BlockSpec((None,) + stacked.shape[1:], lambda *_: (l,) + zeros)


def _nt_dot(a, b, precision=None):
    return lax.dot_general(a, b, (((1,), (1,)), ((), ())), precision=precision,
                           preferred_element_type=F32)


def _dot(a, b, precision=None):
    return jnp.dot(a, b, precision=precision, preferred_element_type=F32)


def _layer_norm(y, g, b):
    mu = jnp.mean(y, axis=-1, keepdims=True)
    yc = y - mu
    var = jnp.mean(yc * yc, axis=-1, keepdims=True)
    return yc * lax.rsqrt(var + NORM_EPS) * g + b


def _rms_norm(y, g):
    ms = jnp.mean(y * y, axis=-1, keepdims=True)
    return y * lax.rsqrt(ms + NORM_EPS) * g


def _rope_table_kernel(pos_ref, invf_ref, sgn_ref, cos_ref, sin_ref):
    ang = pos_ref[0].astype(F32) * invf_ref[...]
    cos_ref[0] = jnp.cos(ang)
    sin_ref[0] = jnp.sin(ang) * sgn_ref[...]


def _rope_tables(positions, invf, sgn, ts):
    B, S = positions.shape
    pos3 = positions.reshape(B, S, 1)
    spec = pl.BlockSpec((1, ts, LANES), lambda b, i: (b, i, 0))
    vec = pl.BlockSpec((1, LANES), lambda b, i: (0, 0))
    return pl.pallas_call(
        _rope_table_kernel,
        out_shape=(jax.ShapeDtypeStruct((B, S, LANES), F32),) * 2,
        grid=(B, S // ts),
        in_specs=[pl.BlockSpec((1, ts, 1), lambda b, i: (b, i, 0)), vec, vec],
        out_specs=(spec, spec),
        compiler_params=_cparams(("parallel", "parallel")),
        name="rope_tables",
    )(pos3, invf, sgn)


def _proj_kernel(x_ref, wa_ref, wfox_ref, wfoxv_ref, wdil_ref, qg_ref, kvg_ref, wuq_ref, wuk_ref,
                 wuv_ref, bf_ref, cosa_ref, sina_ref, cosd_ref, sind_ref,
                 qa_ref, ka_ref, va_ref, qb_ref, kb_ref, vb_ref,
                 qc_ref, kc_ref, vc_ref, carry_c, *dil_scs, tm):
    i = pl.program_id(1)
    xb = x_ref[0].astype(BF16)

    ha = _dot(xb, wa_ref[...])
    cqn = _rms_norm(ha[:, A_CQ:A_CQ + MLA_Q_RANK], qg_ref[...]).astype(BF16)
    ckvn = _rms_norm(ha[:, A_CKV:A_CKV + MLA_KV_RANK], kvg_ref[...]).astype(BF16)
    cos_a = cosa_ref[0]
    sin_a = sina_ref[0]
    q2 = _dot(cqn, wuq_ref[...])
    k_nope = _dot(ckvn, wuk_ref[...])
    k_rot = ha[:, A_KR:A_KR + LANES] * cos_a + ha[:, A_KRR:A_KRR + LANES] * sin_a
    q_scale = (MLA_NOPE_DIM + MLA_ROPE_DIM) ** -0.5 * LOG2E
    n_half = MLA_HEADS * LANES
    for h in range(MLA_HEADS):
        sl = slice(h * LANES, (h + 1) * LANES)
        qh = q2[:, sl] * cos_a + q2[:, n_half + h * LANES:n_half + (h + 1) * LANES] * sin_a
        qa_ref[0, :, sl] = (qh * q_scale).astype(BF16)
        ka_ref[0, :, sl] = (k_nope[:, sl] + k_rot).astype(BF16)
    va_ref[0] = _nt_dot(wuv_ref[...], ckvn).astype(BF16)

    lane = lax.broadcasted_iota(jnp.int32, (1, LANES), 1)
    bias_lanes = (lane >= FOX_HEAD_DIM) & (lane < FOX_HEAD_DIM + 3 * FOX_HEADS)
    z = ha[:, A_FF:A_FF + LANES] + bf_ref[...]
    logf = jnp.minimum(z, 0.0) - jnp.log(1.0 + jnp.exp(-jnp.abs(z)))
    logf = jnp.where(bias_lanes, logf, 0.0)

    @pl.when(i == 0)
    def _():
        carry_c[...] = jnp.zeros_like(carry_c)

    p0 = logf.astype(BF16).astype(F32)
    r0 = logf - p0
    p1 = r0.astype(BF16).astype(F32)
    p2 = (r0 - p1).astype(BF16)
    packed = jnp.concatenate([(p0 + pltpu.roll(p1, HEAD_LANES, 1)).astype(BF16), p2], axis=1)
    r = lax.broadcasted_iota(jnp.int32, (tm, tm), 0)
    c = lax.broadcasted_iota(jnp.int32, (tm, tm), 1)
    lower = (c <= r).astype(BF16)
    cum = _dot(lower, packed)
    cum_a = cum[:, :LANES]
    fcol = jnp.where(bias_lanes, cum_a + pltpu.roll(cum_a, HEAD_LANES, 1) + cum[:, LANES:], 0.0) + carry_c[...]
    carry_c[...] = fcol[tm - 1:tm, :]

    nb = fcol * (-LOG2E)
    hi = nb.astype(BF16).astype(F32)
    r1 = nb - hi
    mid = r1.astype(BF16).astype(F32)
    lo3 = (r1 - mid).astype(BF16).astype(F32)
    piece = (lane - FOX_HEAD_DIM) % 3
    k_bias = jnp.where(piece == 0, hi, jnp.where(piece == 1, mid, lo3))

    hf = _dot(xb, wfox_ref[...])
    nf = FOX_HEADS * FOX_HEAD_DIM
    dims = lane < FOX_HEAD_DIM
    fq_scale = FOX_HEAD_DIM ** -0.5 * LOG2E
    for h in range(FOX_HEADS):
        blk = slice((h // 2) * LANES, (h // 2 + 1) * LANES)
        out = slice(h * LANES, (h + 1) * LANES)
        qh = hf[:, blk]
        kh = hf[:, nf + (h // 2) * LANES:nf + (h // 2 + 1) * LANES]
        if h % 2:
            qh = pltpu.roll(qh, HEAD_LANES, 1)
            kh = pltpu.roll(kh, HEAD_LANES, 1)
        own = (lane >= FOX_HEAD_DIM + 3 * h) & (lane < FOX_HEAD_DIM + 3 * h + 3)
        qb_ref[0, :, out] = jnp.where(dims, qh * fq_scale, jnp.where(own, 1.0, 0.0)).astype(BF16)
        kb_ref[0, :, out] = jnp.where(dims, kh, k_bias).astype(BF16)
    vb_ref[0] = _nt_dot(wfoxv_ref[...], xb).astype(BF16)

    hd = _dot(xb, wdil_ref[...])
    cos_d = cosd_ref[0]
    sin_d = sind_ref[0]
    nd = DIL_HEADS * DIL_HEAD_DIM
    for pblk in range(nd // LANES):
        sl = slice(pblk * LANES, (pblk + 1) * LANES)
        qh = hd[:, sl]
        kh = hd[:, nd + pblk * LANES:nd + (pblk + 1) * LANES]
        qh = qh * cos_d + pltpu.roll(qh, HEAD_LANES, 1) * sin_d
        kh = kh * cos_d + pltpu.roll(kh, HEAD_LANES, 1) * sin_d
        dil_scs[pblk][...] = qh * (DIL_HEAD_DIM ** -0.5 * LOG2E)
        dil_scs[3 + pblk][...] = kh
        dil_scs[6 + pblk][...] = hd[:, 2 * nd + pblk * LANES:2 * nd + (pblk + 1) * LANES]
    for a, ref in enumerate((qc_ref, kc_ref, vc_ref)):
        for r in range(DIL_PLANES):
            rows = pl.ds(r, tm // DIL_PLANES, stride=DIL_PLANES)
            for pblk in range(nd // LANES):
                lanes = slice(r * nd + pblk * LANES, r * nd + (pblk + 1) * LANES)
                ref[0, :, lanes] = dil_scs[3 * a + pblk][rows, :].astype(BF16)


def _proj(x, wts, l, tabs, tm):
    B, S, D = x.shape
    cos_a, sin_a, cos_d, sin_d = tabs
    grid = (B, S // tm)

    def tok(width, dtype):
        return (jax.ShapeDtypeStruct((B, S, width), dtype),
                pl.BlockSpec((1, tm, width), lambda b, i: (b, i, 0)))

    tab = pl.BlockSpec((1, tm, LANES), lambda b, i: (b, i, 0))
    nd = DIL_HEADS * DIL_HEAD_DIM

    def planes(dtype):
        return (jax.ShapeDtypeStruct((B, S // DIL_PLANES, DIL_PLANES * nd), dtype),
                pl.BlockSpec((1, tm // DIL_PLANES, DIL_PLANES * nd), lambda b, i: (b, i, 0)))

    def tok_t(height, dtype):
        return (jax.ShapeDtypeStruct((B, height, S), dtype),
                pl.BlockSpec((1, height, tm), lambda b, i: (b, 0, i)))

    outs = [tok(MLA_HEADS * LANES, BF16), tok(MLA_HEADS * LANES, BF16),
            tok_t(MLA_HEADS * MLA_V_DIM, BF16),
            tok(FOX_HEADS * LANES, BF16), tok(FOX_HEADS * LANES, BF16),
            tok_t(FOX_HEADS * FOX_HEAD_DIM, BF16),
            planes(BF16), planes(BF16), planes(BF16)]
    w_list = [wts["wa"], wts["wfox"], wts["wfoxv_t"], wts["wdil"], wts["qg"], wts["kvg"], wts["wuq"],
              wts["wuk"], wts["wuv_t"], wts["bf"]]
    return pl.pallas_call(
        functools.partial(_proj_kernel, tm=tm),
        out_shape=tuple(o[0] for o in outs),
        grid=grid,
        in_specs=[pl.BlockSpec((1, tm, D), lambda b, i: (b, i, 0))] + [_layer_slab(w, l) for w in w_list]
        + [tab, tab, tab, tab],
        out_specs=tuple(o[1] for o in outs),
        scratch_shapes=[pltpu.VMEM((1, LANES), F32)] + [pltpu.VMEM((tm, LANES), F32)] * (3 * nd // LANES),
        compiler_params=_cparams(("parallel", "arbitrary")),
        name="proj",
    )(x, *w_list, cos_a, sin_a, cos_d, sin_d)


FLASH_CHUNK = 32


FLASH_BLOCKS_PER_TRIP = 2
FLASH_KEY_SPLIT = 2
ONES_ROWS = 16


def _flash_kernel(q0_ref, q1_ref, k0_ref, k1_ref, vt_ref, o_ref, *scratch, tq):
    i = pl.program_id(2)
    q_refs = (q0_ref, q1_ref)
    k_refs = (k0_ref, k1_ref)
    n_chain = 2 * FLASH_KEY_SPLIT
    s_scs = scratch[:n_chain]
    p_scs = scratch[n_chain:2 * n_chain]
    m_sc, l_sc, acc0_sc, acc1_sc = scratch[2 * n_chain:]
    acc_scs = (acc0_sc, acc1_sc)
    dv = HEAD_LANES
    tk = tq // FLASH_KEY_SPLIT
    n_chunks = tk // FLASH_CHUNK

    m_sc[...] = jnp.full_like(m_sc, NEG_INF)
    l_sc[...] = jnp.zeros_like(l_sc)
    acc0_sc[...] = jnp.zeros_like(acc0_sc)
    acc1_sc[...] = jnp.zeros_like(acc1_sc)

    def step(j, masked):
        m_locs = []
        for h in range(2):
            for c in range(FLASH_KEY_SPLIT):
                ks = pl.multiple_of(j * tq + c * tk, tk)
                q0 = c * tk if masked else 0
                k = k_refs[h][0, pl.ds(ks, tk), :]
                st = _nt_dot(k, q_refs[h][0, q0:, :])
                if masked:
                    key = lax.broadcasted_iota(jnp.int32, (tk, tq - q0), 0)
                    qry = lax.broadcasted_iota(jnp.int32, (tk, tq - q0), 1)
                    st = jnp.where(key <= qry, st, NEG_INF)
                s_scs[h * FLASH_KEY_SPLIT + c][:, q0:] = st
                m8 = jnp.max(st.reshape(tk // 8, 8, tq - q0), axis=0)
                m_locs.append(jnp.max(m8, axis=0, keepdims=True))
        locs = []
        for h in range(2):
            for c in range(FLASH_KEY_SPLIT):
                ks = pl.multiple_of(j * tq + c * tk, tk)
                q0 = c * tk if masked else 0
                s_sc = s_scs[h * FLASH_KEY_SPLIT + c]
                p_sc = p_scs[h * FLASH_KEY_SPLIT + c]
                m_loc = m_locs[h * FLASH_KEY_SPLIT + c]
                for r in range(n_chunks):
                    rows = slice(r * FLASH_CHUNK, (r + 1) * FLASH_CHUNK)
                    p_sc[rows, q0:] = jnp.exp2(s_sc[rows, q0:] - m_loc).astype(BF16)
                vt = vt_ref[0, h * dv:(h + 1) * dv, pl.ds(ks, tk)]
                vt_ext = jnp.concatenate([vt, jnp.ones((ONES_ROWS, tk), BF16)], axis=0)
                pvl = _dot(vt_ext, p_sc[:, q0:])
                if q0:
                    m_loc = jnp.concatenate([jnp.full((1, q0), NEG_INF, F32), m_loc], axis=1)
                    pvl = jnp.concatenate([jnp.zeros((dv + ONES_ROWS, q0), F32), pvl], axis=1)
                locs.append((m_loc, pvl))
        for h in range(2):
            mine = locs[h * FLASH_KEY_SPLIT:(h + 1) * FLASH_KEY_SPLIT]
            m_prev = m_sc[h]
            m_new = m_prev
            for m_loc, _ in mine:
                m_new = jnp.maximum(m_new, m_loc)
            a_prev = jnp.exp2(m_prev - m_new)
            acc = acc_scs[h][...] * a_prev
            l = l_sc[h] * a_prev
            for m_loc, pvl in mine:
                a_loc = jnp.exp2(m_loc - m_new)
                acc = acc + pvl[:dv] * a_loc
                l = l + pvl[dv:dv + 1] * a_loc
            acc_scs[h][...] = acc
            l_sc[h] = l
            m_sc[h] = m_new

    per_trip = FLASH_BLOCKS_PER_TRIP

    def body(jj, carry):
        for u in range(per_trip):
            step(per_trip * jj + u, False)
        return carry

    n_trips = i // per_trip
    lax.fori_loop(0, n_trips, body, 0)

    def single(j, carry):
        step(j, False)
        return carry

    lax.fori_loop(n_trips * per_trip, i, single, 0)
    step(i, True)
    for h in range(2):
        o_ref[0, h * dv:(h + 1) * dv, :] = (acc_scs[h][...] / l_sc[h]).astype(o_ref.dtype)


def _flash_attention(q, k, vt, n_heads, tq, name):
    B, S, _ = q.shape
    n_pairs = n_heads // 2
    dv2 = 2 * HEAD_LANES
    qspec0 = pl.BlockSpec((1, tq, LANES), lambda b, p, i: (b, i, 2 * p))
    qspec1 = pl.BlockSpec((1, tq, LANES), lambda b, p, i: (b, i, 2 * p + 1))
    kspec0 = pl.BlockSpec((1, S, LANES), lambda b, p, i: (b, 0, 2 * p))
    kspec1 = pl.BlockSpec((1, S, LANES), lambda b, p, i: (b, 0, 2 * p + 1))
    vspec = pl.BlockSpec((1, dv2, S), lambda b, p, i: (b, p, 0))
    return pl.pallas_call(
        functools.partial(_flash_kernel, tq=tq),
        out_shape=jax.ShapeDtypeStruct((B, n_heads * HEAD_LANES, S), BF16),
        grid=(B, n_pairs, S // tq),
        in_specs=[qspec0, qspec1, kspec0, kspec1, vspec],
        out_specs=pl.BlockSpec((1, dv2, tq), lambda b, p, i: (b, p, i)),
        scratch_shapes=[pltpu.VMEM((tq // FLASH_KEY_SPLIT, tq), F32)] * (2 * FLASH_KEY_SPLIT)
        + [pltpu.VMEM((tq // FLASH_KEY_SPLIT, tq), BF16)] * (2 * FLASH_KEY_SPLIT)
        + [pltpu.VMEM((2, 1, tq), F32), pltpu.VMEM((2, 1, tq), F32),
                        pltpu.VMEM((HEAD_LANES, tq), F32), pltpu.VMEM((HEAD_LANES, tq), F32)],
        compiler_params=_cparams(("parallel", "parallel", "arbitrary")),
        name=name,
    )(q, q, k, k, vt)


DIL_PLANES = 16
DIL_TQ = 128
DIL_UNROLL = 8


def _plane_slot(r16):
    return 4 * (r16 % 4) + r16 // 4


def _dilated_geometry(dil):
    planes = DIL_PLANES // dil
    rows = DIL_TQ // planes
    if dil == 1:
        order = [(c // 4) + 4 * (c % 4) for c in range(planes)]
    else:
        order = list(range(planes))
    return planes, rows, order


def _dilated_bias(dil, first):
    planes, rows, order = _dilated_geometry(dil)
    krows = 2 * rows
    a = np.arange(planes * rows)
    b = np.arange(planes * krows)
    uq = np.asarray(order)[a // rows] + planes * (a % rows + (0 if first else rows))
    uk = np.asarray(order)[b // krows] + planes * (b % krows)
    diff = uq[:, None] - uk[None, :]
    valid = (diff >= 0) & (diff <= DIL_TQ)
    return jnp.asarray(np.where(valid, 0.0, NEG_INF), F32)


def _dilated_kernel(q_ref, k_ref, v_ref, b1f_ref, b1r_ref, b4f_ref, b4r_ref, b16f_ref, b16r_ref, o_ref,
                    q_st, k_st, v_st, m_sc, l_sc, acc_sc, *, rows_per_plane):
    pair = pl.program_id(1)
    n_pairs = DIL_HEADS // 2
    lane = lax.broadcasted_iota(jnp.int32, (1, LANES), 1)
    lo = lane < HEAD_LANES
    q_lo = (lane % HEAD_LANES) < (HEAD_LANES // 2)
    tq = DIL_TQ

    def plane_lanes(r16):
        return pl.ds(pl.multiple_of((r16 * n_pairs + pair) * LANES, LANES), LANES)

    for r16 in range(DIL_PLANES):
        slot = _plane_slot(r16)
        q_st[slot] = q_ref[0, :, plane_lanes(r16)].astype(F32)
        k_st[slot] = k_ref[0, :, plane_lanes(r16)].astype(F32)
        v_st[slot] = v_ref[0, :, plane_lanes(r16)].astype(F32)
    m_sc[...] = jnp.full_like(m_sc, NEG_INF)
    l_sc[...] = jnp.zeros_like(l_sc)
    acc_sc[...] = jnp.zeros_like(acc_sc)

    def block(planes, rows, slot0, row0, first, bias_ref):
        qsl = (pl.ds(slot0, planes), pl.ds(row0, rows), slice(None))
        k_row0 = row0 if first else row0 - rows
        ksl = (pl.ds(slot0, planes), pl.ds(k_row0, 2 * rows), slice(None))
        q = q_st[qsl].reshape(tq, LANES)
        k = k_st[ksl].reshape(-1, LANES).astype(BF16)
        v = v_st[ksl].reshape(-1, LANES).astype(BF16)
        zero = jnp.zeros_like(q)
        qs = (jnp.where(q_lo, q, zero).astype(BF16), jnp.where(q_lo, zero, q).astype(BF16))
        bias = bias_ref[...]
        v_ext = jnp.concatenate([v, jnp.ones_like(v)], axis=1)
        maxes, sums, pvs = [], [], []
        for h in range(2):
            s = _nt_dot(qs[h], k) + bias
            mh = jnp.max(s, axis=1, keepdims=True)
            p = jnp.exp2(s - mh)
            maxes.append(mh)
            pvl = _dot(p.astype(BF16), v_ext)
            pvs.append(pvl[:, :LANES])
            sums.append(pvl[:, LANES:])
        m_blk = jnp.where(lo, maxes[0], maxes[1])
        m_old = m_sc[qsl].reshape(tq, LANES)
        m_new = jnp.maximum(m_old, m_blk)
        a_old = jnp.exp2(m_old - m_new)
        a_blk = jnp.exp2(m_blk - m_new)
        l_new = a_old * l_sc[qsl].reshape(tq, LANES) + a_blk * jnp.where(lo, sums[0], sums[1])
        acc_new = a_old * acc_sc[qsl].reshape(tq, LANES) + a_blk * jnp.where(lo, pvs[0], pvs[1])
        m_sc[qsl] = m_new.reshape(planes, rows, LANES)
        l_sc[qsl] = l_new.reshape(planes, rows, LANES)
        acc_sc[qsl] = acc_new.reshape(planes, rows, LANES)

    bias_refs = {1: (b1f_ref, b1r_ref), 4: (b4f_ref, b4r_ref), 16: (b16f_ref, b16r_ref)}
    for window, dil in DILATED_PAIRS:
        assert window // dil == tq
        planes, rows, _ = _dilated_geometry(dil)
        nblk = rows_per_plane // rows
        bf_ref, br_ref = bias_refs[dil]

        def first_body(sub, carry, planes=planes, rows=rows, bf_ref=bf_ref):
            block(planes, rows, sub * planes, 0, True, bf_ref)
            return carry

        lax.fori_loop(0, dil, first_body, 0, unroll=min(dil, DIL_UNROLL))

        if nblk > 1:
            def rest_body(t, carry, planes=planes, rows=rows, nblk=nblk, br_ref=br_ref):
                sub = t // (nblk - 1)
                blk = t % (nblk - 1) + 1
                block(planes, rows, sub * planes, pl.multiple_of(blk * rows, rows), False, br_ref)
                return carry

            lax.fori_loop(0, dil * (nblk - 1), rest_body, 0, unroll=DIL_UNROLL)

    for r16 in range(DIL_PLANES):
        slot = _plane_slot(r16)
        o_ref[0, :, plane_lanes(r16)] = (acc_sc[slot] / l_sc[slot]).astype(o_ref.dtype)


def _dilated_attention(qc, kc, vc):
    B, rpp, width = qc.shape
    n_pairs = DIL_HEADS // 2
    assert rpp % DIL_TQ == 0 and rpp >= 2 * DIL_TQ

    spec = pl.BlockSpec((1, rpp, width), lambda b, p: (b, 0, 0))
    biases = [_dilated_bias(d, f) for _, d in DILATED_PAIRS for f in (True, False)]
    bspecs = [pl.BlockSpec(bb.shape, lambda b, p: (0, 0)) for bb in biases]
    plane = pltpu.VMEM((DIL_PLANES, rpp, LANES), F32)
    return pl.pallas_call(
        functools.partial(_dilated_kernel, rows_per_plane=rpp),
        out_shape=jax.ShapeDtypeStruct((B, rpp, width), BF16),
        grid=(B, n_pairs),
        in_specs=[spec, spec, spec] + bspecs,
        out_specs=spec,
        scratch_shapes=[plane] * 6,
        compiler_params=_cparams(("parallel", "arbitrary")),
        name="dilated",
    )(qc, kc, vc, *biases)


def _tn_dot(a_t, b):
    return lax.dot_general(a_t, b, (((0,), (0,)), ((), ())), preferred_element_type=F32)


def _mix_out_kernel(oat_ref, obt_ref, oc_ref, x_ref, wa_ref, wb_ref, wc_ref, g_ref, b_ref, o_ref, *oc_scs):
    nd = len(oc_scs) * LANES
    rows_per_plane = oc_scs[0].shape[0] // DIL_PLANES
    for r in range(DIL_PLANES):
        rows = pl.ds(r, rows_per_plane, stride=DIL_PLANES)
        for pblk, sc in enumerate(oc_scs):
            lanes = slice(r * nd + pblk * LANES, r * nd + (pblk + 1) * LANES)
            sc[rows, :] = oc_ref[0, :, lanes].astype(F32)
    oc = jnp.concatenate([sc[...] for sc in oc_scs], axis=1).astype(BF16)
    y = (_tn_dot(oat_ref[0], wa_ref[...]) + _tn_dot(obt_ref[0], wb_ref[...])
         + _dot(oc, wc_ref[...]))
    o_ref[0] = _layer_norm(DEEPNORM_ALPHA * x_ref[0] + y, g_ref[...], b_ref[...])


def _mix_out(oa_t, ob_t, oc, x, wa, wb, wc, g, b, l, tm):
    B, S, D = x.shape

    def cols(a):
        return pl.BlockSpec((1, a.shape[1], tm), lambda bb, i: (bb, 0, i))

    def rows(a):
        return pl.BlockSpec((1, tm, a.shape[2]), lambda bb, i: (bb, i, 0))

    return pl.pallas_call(
        _mix_out_kernel,
        out_shape=jax.ShapeDtypeStruct((B, S, D), F32),
        grid=(B, S // tm),
        in_specs=[cols(oa_t), cols(ob_t),
                  pl.BlockSpec((1, tm // DIL_PLANES, oc.shape[2]), lambda bb, i: (bb, i, 0)),
                  rows(x)] + [_layer_slab(a, l) for a in (wa, wb, wc, g, b)],
        out_specs=rows(x),
        scratch_shapes=[pltpu.VMEM((tm, LANES), F32)] * (oc.shape[2] // DIL_PLANES // LANES),
        compiler_params=_cparams(("parallel", "parallel")),
        name="mix_out",
    )(oa_t, ob_t, oc, x, wa, wb, wc, g, b)


def _mem_kv_kernel(mem_ref, w_ref, k_ref, v_ref):
    kv = _dot(mem_ref[0].astype(BF16), w_ref[...])
    d = k_ref.shape[-1]
    k_ref[0] = kv[:, :d].astype(BF16)
    v_ref[0] = kv[:, d:].astype(BF16)


def _mem_kv(mem, w_kv, l):
    B, M, D = mem.shape
    spec = pl.BlockSpec((1, M, D), lambda b: (b, 0, 0))
    return pl.pallas_call(
        _mem_kv_kernel,
        out_shape=(jax.ShapeDtypeStruct((B, M, D), BF16),) * 2,
        grid=(B,),
        in_specs=[spec, _layer_slab(w_kv, l)],
        out_specs=(spec, spec),
        compiler_params=_cparams(("parallel",)),
        name="mem_kv",
    )(mem, w_kv)


def _xattn_kernel(x_ref, k_ref, v_ref, wq_ref, wo_ref, g_ref, b_ref, o_ref):
    x = x_ref[0]
    d = x.shape[-1]
    dh = d // XATTN_HEADS
    q = (_dot(x.astype(BF16), wq_ref[...]) * (dh ** -0.5)).astype(BF16)
    outs = []
    for h in range(XATTN_HEADS):
        sl = slice(h * dh, (h + 1) * dh)
        s = _nt_dot(q[:, sl], k_ref[0, :, sl])
        p = jnp.exp(s - jnp.max(s, axis=1, keepdims=True))
        o = _dot(p.astype(BF16), v_ref[0, :, sl])
        outs.append((o / jnp.sum(p, axis=1, keepdims=True)).astype(BF16))
    o_all = jnp.concatenate(outs, axis=1)
    y = _dot(o_all, wo_ref[...])
    o_ref[0] = _layer_norm(DEEPNORM_ALPHA * x + y, g_ref[...], b_ref[...])


def _xattn(x, k_mem, v_mem, w_q, w_o, g, b, l, tm):
    B, S, D = x.shape
    M = k_mem.shape[1]
    xspec = pl.BlockSpec((1, tm, D), lambda bb, i: (bb, i, 0))
    mspec = pl.BlockSpec((1, M, D), lambda bb, i: (bb, 0, 0))

    return pl.pallas_call(
        _xattn_kernel,
        out_shape=jax.ShapeDtypeStruct((B, S, D), F32),
        grid=(B, S // tm),
        in_specs=[xspec, mspec, mspec] + [_layer_slab(a, l) for a in (w_q, w_o, g, b)],
        out_specs=xspec,
        compiler_params=_cparams(("parallel", "parallel")),
        name="xattn",
    )(x, k_mem, v_mem, w_q, w_o, g, b)


def _split_bf16(a):
    hi = a.astype(BF16)
    return hi, (a - hi.astype(F32)).astype(BF16)


def _router_gates_t(x, rw_t, rbias):
    x_hi, x_lo = _split_bf16(x)
    w_hi, w_lo = _split_bf16(rw_t)
    t1 = _nt_dot(jnp.concatenate([w_hi, w_lo], axis=0), x_hi)
    logits_t = t1[:N_EXPERTS] + (t1[N_EXPERTS:] + _nt_dot(w_hi, x_lo))
    scores = 1.0 / (1.0 + jnp.exp(-logits_t))
    biased = scores + rbias
    sc = [scores[e:e + 1, :] for e in range(N_EXPERTS)]
    bs = [biased[e:e + 1, :] for e in range(N_EXPERTS)]
    epg = EXPERTS_PER_GROUP

    def beats(a, b, a_first):
        return (a >= b) if a_first else (a > b)

    grp = []
    for gi in range(N_GROUPS):
        v = bs[gi * epg:(gi + 1) * epg]
        best = None
        for a in range(epg):
            for b in range(a + 1, epg):
                pair = v[a] + v[b]
                best = pair if best is None else jnp.maximum(best, pair)
        grp.append(best)
    gates, g_sels = [], []
    for gi in range(N_GROUPS):
        g_sel = None
        for gj in range(N_GROUPS):
            if gj == gi:
                continue
            w = beats(grp[gi], grp[gj], gi < gj)
            g_sel = w if g_sel is None else (g_sel & w)
        g_sels.append(g_sel)
        in_top = []
        for a in range(epg):
            ea = gi * epg + a
            n_above = jnp.zeros(bs[ea].shape, jnp.int32)
            for b in range(epg):
                if b == a:
                    continue
                eb = gi * epg + b
                n_above = n_above + beats(bs[eb], bs[ea], b < a).astype(jnp.int32)
            in_top.append(g_sel & (n_above < 2))
        denom = None
        for a in range(epg):
            term = jnp.where(in_top[a], sc[gi * epg + a], 0.0)
            denom = term if denom is None else denom + term
        denom = jnp.where(g_sel, denom, 1.0)
        for a in range(epg):
            gates.append(jnp.where(in_top[a], sc[gi * epg + a] / denom, 0.0))
    return gates, g_sels


def _moe_kernel(x_ref, rwt_ref, rb_ref, wg_ref, wu_ref, wd_ref, g_ref, b_ref, o_ref,
                gt_sc, sel_sc, rank_sc, acc_sc, xb_sc, *, tm):
    grp = pl.program_id(1)
    epg = EXPERTS_PER_GROUP

    @pl.when(grp == 0)
    def _():
        ts = min(tm, ROUTER_ROWS)
        before = (lax.broadcasted_iota(jnp.int32, (ts, ts), 0)
                  < lax.broadcasted_iota(jnp.int32, (ts, ts), 1)).astype(BF16)
        seen = jnp.zeros((8, 1), F32)
        for r in range(tm // ts):
            cols = slice(r * ts, (r + 1) * ts)
            x = x_ref[cols, :]
            gates, g_sels = _router_gates_t(x, rwt_ref[...], rb_ref[...])
            zeros4 = [jnp.zeros((1, ts), F32)] * (8 - epg)
            for gi in range(N_GROUPS):
                gt_sc[gi, :, cols] = jnp.concatenate(gates[gi * epg:(gi + 1) * epg] + zeros4, axis=0)
            sel8 = jnp.concatenate([s.astype(F32) for s in g_sels] + [jnp.zeros((1, ts), F32)] * (8 - N_GROUPS),
                                   axis=0)
            sel_sc[:, cols] = sel8
            rank_sc[:, cols] = _dot(sel8.astype(BF16), before) + seen
            seen = seen + jnp.sum(sel8, axis=1, keepdims=True)
            xb_sc[cols, :] = x.astype(BF16)
        acc_sc[...] = jnp.zeros_like(acc_sc)

    sel = sel_sc[pl.ds(grp, 1), :] > 0.5
    rank = rank_sc[pl.ds(grp, 1), :].astype(jnp.int32)
    n_tok = jnp.sum(sel_sc[pl.ds(grp, 1), :]).astype(jnp.int32)
    gt = gt_sc[grp]
    g_hi = gt.astype(BF16).astype(F32)
    g_r = gt - g_hi
    g_mid = g_r.astype(BF16).astype(F32)
    g_lo = (g_r - g_mid).astype(BF16).astype(F32)
    g3 = jnp.concatenate([g_hi, g_mid, g_lo, jnp.zeros_like(g_hi)], axis=0).astype(BF16)
    wd_all = wd_ref[...].reshape(epg * D_EXPERT, -1)

    def chunk(first_slot, ch):
        slot = lax.broadcasted_iota(jnp.int32, (ch, tm), 0) + first_slot
        onehot = jnp.where((slot == rank) & sel, 1.0, 0.0).astype(BF16)
        xg = _dot(onehot, xb_sc[...]).astype(BF16)
        gs3 = _nt_dot(onehot, g3)
        gs = gs3[:, 0:8] + (gs3[:, 8:16] + gs3[:, 16:24])
        acts = []
        for e in range(epg):
            hg = _dot(xg, wg_ref[e])
            hu = _dot(xg, wu_ref[e])
            acts.append(((hg / (1.0 + jnp.exp(-hg))) * hu * gs[:, e:e + 1]).astype(BF16))
        y = _dot(jnp.concatenate(acts, axis=1), wd_all).astype(BF16)
        acc_sc[...] += _tn_dot(onehot, y)

    n_wide = (n_tok + (MOE_CHUNK - MOE_TAIL_CHUNK - 1)) // MOE_CHUNK

    def wide(c, carry):
        chunk(c * MOE_CHUNK, MOE_CHUNK)
        return carry

    lax.fori_loop(0, n_wide, wide, 0)

    @pl.when(n_tok > n_wide * MOE_CHUNK)
    def _():
        chunk(n_wide * MOE_CHUNK, MOE_TAIL_CHUNK)

    @pl.when(grp == N_GROUPS - 1)
    def _():
        o_ref[...] = _layer_norm(DEEPNORM_ALPHA * x_ref[...] + acc_sc[...], g_ref[...], b_ref[...])


ROUTER_ROWS = 512
MOE_CHUNK = 256
MOE_TAIL_CHUNK = 128


def _moe(x2d, rw_t, rbias, wg, wu, wd, g, b, l, tm):
    T, D = x2d.shape
    xspec = pl.BlockSpec((tm, D), lambda i, e: (i, 0))
    epg = EXPERTS_PER_GROUP

    def full(a):
        return pl.BlockSpec(a.shape, lambda i, e: (0, 0))

    return pl.pallas_call(
        functools.partial(_moe_kernel, tm=tm),
        out_shape=jax.ShapeDtypeStruct((T, D), F32),
        grid=(T // tm, N_GROUPS),
        in_specs=[xspec, full(rw_t), full(rbias),
                  pl.BlockSpec((None, epg, D, D_EXPERT), lambda i, e: (l, e, 0, 0)),
                  pl.BlockSpec((None, epg, D, D_EXPERT), lambda i, e: (l, e, 0, 0)),
                  pl.BlockSpec((None, epg, D_EXPERT, D), lambda i, e: (l, e, 0, 0)),
                  _layer_slab(g, l), _layer_slab(b, l)],
        out_specs=xspec,
        scratch_shapes=[pltpu.VMEM((N_GROUPS, 8, tm), F32), pltpu.VMEM((8, tm), F32), pltpu.VMEM((8, tm), F32),
                        pltpu.VMEM((tm, D), F32), pltpu.VMEM((tm, D), BF16)],
        compiler_params=_cparams(("parallel", "arbitrary")),
        name="moe",
    )(x2d, rw_t, rbias, wg, wu, wd, g, b)


def _rope_constants():
    half_d = DIL_HEAD_DIM // 2
    inv_d = ROPE_THETA ** (-jnp.arange(half_d, dtype=F32) / half_d)
    half_a = MLA_ROPE_DIM // 2
    inv_a = ROPE_THETA ** (-jnp.arange(half_a, dtype=F32) / half_a)
    lane = np.arange(LANES)
    invf_d = inv_d[lane % half_d].reshape(1, LANES)
    sgn_d = jnp.asarray(np.where(lane < HEAD_LANES, -1.0, 1.0).reshape(1, LANES), F32)
    in_rope = (lane >= MLA_NOPE_DIM) & (lane < MLA_NOPE_DIM + MLA_ROPE_DIM)
    invf_a = jnp.where(jnp.asarray(in_rope), inv_a[(lane - MLA_NOPE_DIM) % half_a], 0.0).reshape(1, LANES)
    sgn_a = jnp.ones((1, LANES), F32)
    return invf_a, sgn_a, invf_d, sgn_d


def _layer_weights(w_in, b_forget, q_gain, kv_gain, w_uq, w_ukv):
    D = w_in.shape[0]
    cuts = np.cumsum([MLA_Q_RANK, MLA_KV_RANK, MLA_ROPE_DIM, 3 * FOX_HEADS * FOX_HEAD_DIM, FOX_HEADS])
    w_cq, w_ckv, w_kr, w_fox, w_ff, w_dil = jnp.split(w_in, cuts.tolist(), axis=1)
    half = MLA_ROPE_DIM // 2
    w_kr_rot = jnp.concatenate([-w_kr[:, half:], w_kr[:, :half]], axis=1)

    def place_rope(w):
        return jnp.pad(w, ((0, 0), (MLA_NOPE_DIM, LANES - MLA_NOPE_DIM - MLA_ROPE_DIM)))

    def place_gate(w):
        rep = jnp.repeat(w, 3, axis=1)
        return jnp.pad(rep, ((0, 0), (FOX_HEAD_DIM, LANES - FOX_HEAD_DIM - 3 * FOX_HEADS)))

    wa = jnp.concatenate([w_cq, w_ckv, place_rope(w_kr), place_rope(w_kr_rot),
                          place_gate(w_ff)], axis=1)

    nd = DIL_HEADS * DIL_HEAD_DIM

    def pair_layout(w):
        half = DIL_HEAD_DIM // 2
        w5 = w.reshape(D, DIL_HEADS // 2, 2, 2, half)
        return w5.transpose(0, 1, 3, 2, 4).reshape(D, nd)

    wdil = jnp.concatenate([pair_layout(w_dil[:, :nd]), pair_layout(w_dil[:, nd:2 * nd]), w_dil[:, 2 * nd:]],
                           axis=1)

    dq = MLA_NOPE_DIM + MLA_ROPE_DIM
    uq = w_uq.reshape(MLA_Q_RANK, MLA_HEADS, dq)
    uq_rope = uq[:, :, MLA_NOPE_DIM:]
    uq_rot = jnp.concatenate([-uq_rope[:, :, half:], uq_rope[:, :, :half]], axis=2)
    pad_tail = LANES - dq
    uq_plain = jnp.pad(uq, ((0, 0), (0, 0), (0, pad_tail))).reshape(MLA_Q_RANK, MLA_HEADS * LANES)
    uq_rotp = jnp.pad(uq_rot, ((0, 0), (0, 0), (MLA_NOPE_DIM, pad_tail))).reshape(MLA_Q_RANK, MLA_HEADS * LANES)
    wuq = jnp.concatenate([uq_plain, uq_rotp], axis=1)

    ukv = w_ukv.reshape(MLA_KV_RANK, MLA_HEADS, MLA_NOPE_DIM + MLA_V_DIM)
    wuk = jnp.pad(ukv[:, :, :MLA_NOPE_DIM], ((0, 0), (0, 0), (0, LANES - MLA_NOPE_DIM)))
    wuk = wuk.reshape(MLA_KV_RANK, MLA_HEADS * LANES)
    wuv = ukv[:, :, MLA_NOPE_DIM:].reshape(MLA_KV_RANK, MLA_HEADS * MLA_V_DIM)

    bf = place_gate(b_forget.astype(F32).reshape(1, -1))
    nf = FOX_HEADS * FOX_HEAD_DIM
    return dict(wa=wa.astype(BF16), wfox=w_fox[:, :2 * nf].astype(BF16),
                wfoxv_t=w_fox[:, 2 * nf:].T.astype(BF16), wdil=wdil.astype(BF16),
                qg=q_gain.reshape(1, -1).astype(F32), kvg=kv_gain.reshape(1, -1).astype(F32),
                wuq=wuq.astype(BF16), wuk=wuk.astype(BF16), wuv_t=wuv.T.astype(BF16), bf=bf)


def _tiles(S):
    tm = min(512, S)
    return dict(rope=min(512, S), proj=tm, flash=min(512, S), mix=tm, xattn=tm, moe=min(1024, S))


def kernel(x, mem, positions, w_in, b_forget, mla_q_gain, mla_kv_gain, mla_w_uq, mla_w_ukv, w_mix_out, ln_mix_g, ln_mix_b, xattn_w_q, xattn_w_kv, xattn_w_o, ln_mem_g, ln_mem_b, router_w, router_bias, expert_w_gate, expert_w_up, expert_w_down, ln_ffn_g, ln_ffn_b):
    B, S, D = x.shape
    depth = w_in.shape[0]
    t = _tiles(S)
    invf_a, sgn_a, invf_d, sgn_d = _rope_constants()
    cos_a, sin_a = _rope_tables(positions, invf_a, sgn_a, t["rope"])
    cos_d, sin_d = _rope_tables(positions, invf_d, sgn_d, t["rope"])
    tabs = (cos_a, sin_a, cos_d, sin_d)
    rw_t = router_w.T.astype(F32)
    rbias = router_bias.reshape(-1, 1).astype(F32)

    def rows(v):
        return v.reshape(depth, 1, -1).astype(F32)

    wts = jax.vmap(_layer_weights)(w_in, b_forget, mla_q_gain, mla_kv_gain, mla_w_uq, mla_w_ukv)
    na, nb = MLA_HEADS * MLA_V_DIM, FOX_HEADS * FOX_HEAD_DIM
    w_mix = w_mix_out.astype(BF16)
    w_mix_a, w_mix_b, w_mix_c = w_mix[:, :na], w_mix[:, na:na + nb], w_mix[:, na + nb:]
    w_q, w_kv, w_o = xattn_w_q.astype(BF16), xattn_w_kv.astype(BF16), xattn_w_o.astype(BF16)
    wg, wu, wd = expert_w_gate.astype(BF16), expert_w_up.astype(BF16), expert_w_down.astype(BF16)
    g_mix, b_mix, g_mem, b_mem = rows(ln_mix_g), rows(ln_mix_b), rows(ln_mem_g), rows(ln_mem_b)
    g_ffn, b_ffn = rows(ln_ffn_g), rows(ln_ffn_b)

    for l in range(depth):
        qa, ka, va_t, qb, kb, vb_t, qc, kc, vc = _proj(x, wts, l, tabs, t["proj"])
        oa_t = _flash_attention(qa, ka, va_t, MLA_HEADS, t["flash"], "mla_flash")
        ob_t = _flash_attention(qb, kb, vb_t, FOX_HEADS, t["flash"], "fox_flash")
        oc = _dilated_attention(qc, kc, vc)
        x = _mix_out(oa_t, ob_t, oc, x, w_mix_a, w_mix_b, w_mix_c, g_mix, b_mix, l, t["mix"])
        k_mem, v_mem = _mem_kv(mem, w_kv, l)
        x = _xattn(x, k_mem, v_mem, w_q, w_o, g_mem, b_mem, l, t["xattn"])
        x2d = _moe(x.reshape(B * S, D), rw_t, rbias, wg, wu, wd, g_ffn, b_ffn, l, t["moe"])
        x = x2d.reshape(B, S, D)
    return x
```

```python
import functools
import math

import numpy as np
import jax
import jax.numpy as jnp
from jax import lax
from jax.experimental import pallas as pl
from jax.experimental.pallas import tpu as pltpu

F32 = jnp.float32
BF16 = jnp.bfloat16
HIGHEST = lax.Precision.HIGHEST

D_MODEL = 1024
DEPTH = 4
MLA_HEADS = 4
MLA_Q_RANK = 256
MLA_KV_RANK = 128
MLA_NOPE_DIM = 64
MLA_ROPE_DIM = 32
MLA_V_DIM = 64
FOX_HEADS = 6
FOX_HEAD_DIM = 64
DIL_HEADS = 6
DIL_HEAD_DIM = 64
DILATED_PAIRS = ((128, 1), (512, 4), (2048, 16))
XATTN_HEADS = 4
N_EXPERTS = 16
N_GROUPS = 4
EXPERTS_PER_GROUP = N_EXPERTS // N_GROUPS
D_EXPERT = 256
ROPE_THETA = 10000.0
NORM_EPS = 1e-5
NEG_INF = -1e30
DEEPNORM_ALPHA = (2 * DEPTH) ** 0.25
LOG2E = math.log2(math.e)

LANES = 128
HEAD_LANES = 64
VMEM_LIMIT_BYTES = 52 * 1024 * 1024

A_CQ = 0
A_CKV = A_CQ + MLA_Q_RANK
A_KR = A_CKV + MLA_KV_RANK
A_KRR = A_KR + LANES
A_FF = A_KRR + LANES
A_COLS = A_FF + LANES


def _cparams(sem):
    return pltpu.CompilerParams(dimension_semantics=sem, vmem_limit_bytes=VMEM_LIMIT_BYTES)


def _layer_slab(stacked, l):
    zeros = (0,) * (stacked.ndim - 1)
    return pl.BlockSpec((None,) + stacked.shape[1:], lambda *_: (l,) + zeros)


def _nt_dot(a, b, precision=None):
    return lax.dot_general(a, b, (((1,), (1,)), ((), ())), precision=precision,
                           preferred_element_type=F32)


def _dot(a, b, precision=None):
    return jnp.dot(a, b, precision=precision, preferred_element_type=F32)


def _layer_norm(y, g, b):
    mu = jnp.mean(y, axis=-1, keepdims=True)
    yc = y - mu
    var = jnp.mean(yc * yc, axis=-1, keepdims=True)
    return yc * lax.rsqrt(var + NORM_EPS) * g + b


def _rms_norm(y, g):
    ms = jnp.mean(y * y, axis=-1, keepdims=True)
    return y * lax.rsqrt(ms + NORM_EPS) * g


def _rope_table_kernel(pos_ref, invf_ref, sgn_ref, cos_ref, sin_ref):
    ang = pos_ref[0].astype(F32) * invf_ref[...]
    cos_ref[0] = jnp.cos(ang)
    sin_ref[0] = jnp.sin(ang) * sgn_ref[...]


def _rope_tables(positions, invf, sgn, ts):
    B, S = positions.shape
    pos3 = positions.reshape(B, S, 1)
    spec = pl.BlockSpec((1, ts, LANES), lambda b, i: (b, i, 0))
    vec = pl.BlockSpec((1, LANES), lambda b, i: (0, 0))
    return pl.pallas_call(
        _rope_table_kernel,
        out_shape=(jax.ShapeDtypeStruct((B, S, LANES), F32),) * 2,
        grid=(B, S // ts),
        in_specs=[pl.BlockSpec((1, ts, 1), lambda b, i: (b, i, 0)), vec, vec],
        out_specs=(spec, spec),
        compiler_params=_cparams(("parallel", "parallel")),
        name="rope_tables",
    )(pos3, invf, sgn)


def _proj_kernel(x_ref, wa_ref, wfox_ref, wfoxv_ref, wdil_ref, qg_ref, kvg_ref, wuq_ref, wuk_ref,
                 wuv_ref, bf_ref, cosa_ref, sina_ref, cosd_ref, sind_ref,
                 qa_ref, ka_ref, va_ref, qb_ref, kb_ref, vb_ref,
                 qc_ref, kc_ref, vc_ref, carry_c, *dil_scs, tm):
    i = pl.program_id(1)
    xb = x_ref[0].astype(BF16)

    ha = _dot(xb, wa_ref[...])
    cqn = _rms_norm(ha[:, A_CQ:A_CQ + MLA_Q_RANK], qg_ref[...]).astype(BF16)
    ckvn = _rms_norm(ha[:, A_CKV:A_CKV + MLA_KV_RANK], kvg_ref[...]).astype(BF16)
    cos_a = cosa_ref[0]
    sin_a = sina_ref[0]
    q2 = _dot(cqn, wuq_ref[...])
    k_nope = _dot(ckvn, wuk_ref[...])
    k_rot = ha[:, A_KR:A_KR + LANES] * cos_a + ha[:, A_KRR:A_KRR + LANES] * sin_a
    q_scale = (MLA_NOPE_DIM + MLA_ROPE_DIM) ** -0.5 * LOG2E
    n_half = MLA_HEADS * LANES
    for h in range(MLA_HEADS):
        sl = slice(h * LANES, (h + 1) * LANES)
        qh = q2[:, sl] * cos_a + q2[:, n_half + h * LANES:n_half + (h + 1) * LANES] * sin_a
        qa_ref[0, :, sl] = (qh * q_scale).astype(BF16)
        ka_ref[0, :, sl] = (k_nope[:, sl] + k_rot).astype(BF16)
    va_ref[0] = _nt_dot(wuv_ref[...], ckvn).astype(BF16)

    lane = lax.broadcasted_iota(jnp.int32, (1, LANES), 1)
    bias_lanes = (lane >= FOX_HEAD_DIM) & (lane < FOX_HEAD_DIM + 3 * FOX_HEADS)
    z = ha[:, A_FF:A_FF + LANES] + bf_ref[...]
    logf = jnp.minimum(z, 0.0) - jnp.log(1.0 + jnp.exp(-jnp.abs(z)))
    logf = jnp.where(bias_lanes, logf, 0.0)

    @pl.when(i == 0)
    def _():
        carry_c[...] = jnp.zeros_like(carry_c)

    p0 = logf.astype(BF16).astype(F32)
    r0 = logf - p0
    p1 = r0.astype(BF16).astype(F32)
    p2 = (r0 - p1).astype(BF16)
    packed = jnp.concatenate([(p0 + pltpu.roll(p1, HEAD_LANES, 1)).astype(BF16), p2], axis=1)
    r = lax.broadcasted_iota(jnp.int32, (tm, tm), 0)
    c = lax.broadcasted_iota(jnp.int32, (tm, tm), 1)
    lower = (c <= r).astype(BF16)
    cum = _dot(lower, packed)
    cum_a = cum[:, :LANES]
    fcol = jnp.where(bias_lanes, cum_a + pltpu.roll(cum_a, HEAD_LANES, 1) + cum[:, LANES:], 0.0) + carry_c[...]
    carry_c[...] = fcol[tm - 1:tm, :]

    nb = fcol * (-LOG2E)
    hi = nb.astype(BF16).astype(F32)
    r1 = nb - hi
    mid = r1.astype(BF16).astype(F32)
    lo3 = (r1 - mid).astype(BF16).astype(F32)
    piece = (lane - FOX_HEAD_DIM) % 3
    k_bias = jnp.where(piece == 0, hi, jnp.where(piece == 1, mid, lo3))

    hf = _dot(xb, wfox_ref[...])
    nf = FOX_HEADS * FOX_HEAD_DIM
    dims = lane < FOX_HEAD_DIM
    fq_scale = FOX_HEAD_DIM ** -0.5 * LOG2E
    for h in range(FOX_HEADS):
        blk = slice((h // 2) * LANES, (h // 2 + 1) * LANES)
        out = slice(h * LANES, (h + 1) * LANES)
        qh = hf[:, blk]
        kh = hf[:, nf + (h // 2) * LANES:nf + (h // 2 + 1) * LANES]
        if h % 2:
            qh = pltpu.roll(qh, HEAD_LANES, 1)
            kh = pltpu.roll(kh, HEAD_LANES, 1)
        own = (lane >= FOX_HEAD_DIM + 3 * h) & (lane < FOX_HEAD_DIM + 3 * h + 3)
        qb_ref[0, :, out] = jnp.where(dims, qh * fq_scale, jnp.where(own, 1.0, 0.0)).astype(BF16)
        kb_ref[0, :, out] = jnp.where(dims, kh, k_bias).astype(BF16)
    vb_ref[0] = _nt_dot(wfoxv_ref[...], xb).astype(BF16)

    hd = _dot(xb, wdil_ref[...])
    cos_d = cosd_ref[0]
    sin_d = sind_ref[0]
    nd = DIL_HEADS * DIL_HEAD_DIM
    for pblk in range(nd // LANES):
        sl = slice(pblk * LANES, (pblk + 1) * LANES)
        qh = hd[:, sl]
        kh = hd[:, nd + pblk * LANES:nd + (pblk + 1) * LANES]
        qh = qh * cos_d + pltpu.roll(qh, HEAD_LANES, 1) * sin_d
        kh = kh * cos_d + pltpu.roll(kh, HEAD_LANES, 1) * sin_d
        dil_scs[pblk][...] = qh * (DIL_HEAD_DIM ** -0.5 * LOG2E)
        dil_scs[3 + pblk][...] = kh
        dil_scs[6 + pblk][...] = hd[:, 2 * nd + pblk * LANES:2 * nd + (pblk + 1) * LANES]
    for a, ref in enumerate((qc_ref, kc_ref, vc_ref)):
        for r in range(DIL_PLANES):
            rows = pl.ds(r, tm // DIL_PLANES, stride=DIL_PLANES)
            for pblk in range(nd // LANES):
                lanes = slice(r * nd + pblk * LANES, r * nd + (pblk + 1) * LANES)
                ref[0, :, lanes] = dil_scs[3 * a + pblk][rows, :].astype(BF16)


def _proj(x, wts, l, tabs, tm):
    B, S, D = x.shape
    cos_a, sin_a, cos_d, sin_d = tabs
    grid = (B, S // tm)

    def tok(width, dtype):
        return (jax.ShapeDtypeStruct((B, S, width), dtype),
                pl.BlockSpec((1, tm, width), lambda b, i: (b, i, 0)))

    tab = pl.BlockSpec((1, tm, LANES), lambda b, i: (b, i, 0))
    nd = DIL_HEADS * DIL_HEAD_DIM

    def planes(dtype):
        return (jax.ShapeDtypeStruct((B, S // DIL_PLANES, DIL_PLANES * nd), dtype),
                pl.BlockSpec((1, tm // DIL_PLANES, DIL_PLANES * nd), lambda b, i: (b, i, 0)))

    def tok_t(height, dtype):
        return (jax.ShapeDtypeStruct((B, height, S), dtype),
                pl.BlockSpec((1, height, tm), lambda b, i: (b, 0, i)))

    outs = [tok(MLA_HEADS * LANES, BF16), tok(MLA_HEADS * LANES, BF16),
            tok_t(MLA_HEADS * MLA_V_DIM, BF16),
            tok(FOX_HEADS * LANES, BF16), tok(FOX_HEADS * LANES, BF16),
            tok_t(FOX_HEADS * FOX_HEAD_DIM, BF16),
            planes(BF16), planes(BF16), planes(BF16)]
    w_list = [wts["wa"], wts["wfox"], wts["wfoxv_t"], wts["wdil"], wts["qg"], wts["kvg"], wts["wuq"],
              wts["wuk"], wts["wuv_t"], wts["bf"]]
    return pl.pallas_call(
        functools.partial(_proj_kernel, tm=tm),
        out_shape=tuple(o[0] for o in outs),
        grid=grid,
        in_specs=[pl.BlockSpec((1, tm, D), lambda b, i: (b, i, 0))] + [_layer_slab(w, l) for w in w_list]
        + [tab, tab, tab, tab],
        out_specs=tuple(o[1] for o in outs),
        scratch_shapes=[pltpu.VMEM((1, LANES), F32)] + [pltpu.VMEM((tm, LANES), F32)] * (3 * nd // LANES),
        compiler_params=_cparams(("parallel", "arbitrary")),
        name="proj",
    )(x, *w_list, cos_a, sin_a, cos_d, sin_d)


FLASH_CHUNK = 32


FLASH_BLOCKS_PER_TRIP = 2
FLASH_KEY_SPLIT = 2
ONES_ROWS = 16


def _flash_kernel(q0_ref, q1_ref, k0_ref, k1_ref, vt_ref, o_ref, *scratch, tq, n_q):
    def q_tile(i, carry):
        _flash_q_tile(i, q0_ref, q1_ref, k0_ref, k1_ref, vt_ref, o_ref, scratch, tq=tq)
        return carry

    lax.fori_loop(0, n_q, q_tile, 0)


def _flash_q_tile(i, q0_ref, q1_ref, k0_ref, k1_ref, vt_ref, o_ref, scratch, *, tq):
    qs = pl.multiple_of(i * tq, tq)
    q_refs = (q0_ref, q1_ref)
    k_refs = (k0_ref, k1_ref)
    n_chain = 2 * FLASH_KEY_SPLIT
    s_scs = scratch[:n_chain]
    p_scs = scratch[n_chain:2 * n_chain]
    m_sc, l_sc, acc0_sc, acc1_sc = scratch[2 * n_chain:]
    acc_scs = (acc0_sc, acc1_sc)
    dv = HEAD_LANES
    tk = tq // FLASH_KEY_SPLIT
    n_chunks = tk // FLASH_CHUNK

    m_sc[...] = jnp.full_like(m_sc, NEG_INF)
    l_sc[...] = jnp.zeros_like(l_sc)
    acc0_sc[...] = jnp.zeros_like(acc0_sc)
    acc1_sc[...] = jnp.zeros_like(acc1_sc)

    def step(j, masked):
        m_locs = []
        for h in range(2):
            for c in range(FLASH_KEY_SPLIT):
                ks = pl.multiple_of(j * tq + c * tk, tk)
                q0 = c * tk if masked else 0
                k = k_refs[h][0, pl.ds(ks, tk), :]
                q = q_refs[h][0, pl.ds(qs + q0, tq - q0), :]
                st = _nt_dot(k, q)
                if masked:
                    key = lax.broadcasted_iota(jnp.int32, (tk, tq - q0), 0)
                    qry = lax.broadcasted_iota(jnp.int32, (tk, tq - q0), 1)
                    st = jnp.where(key <= qry, st, NEG_INF)
                s_scs[h * FLASH_KEY_SPLIT + c][:, q0:] = st
                m8 = jnp.max(st.reshape(tk // 8, 8, tq - q0), axis=0)
                m_locs.append(jnp.max(m8, axis=0, keepdims=True))
        locs = []
        for h in range(2):
            for c in range(FLASH_KEY_SPLIT):
                ks = pl.multiple_of(j * tq + c * tk, tk)
                q0 = c * tk if masked else 0
                s_sc = s_scs[h * FLASH_KEY_SPLIT + c]
                p_sc = p_scs[h * FLASH_KEY_SPLIT + c]
                m_loc = m_locs[h * FLASH_KEY_SPLIT + c]
                for r in range(n_chunks):
                    rows = slice(r * FLASH_CHUNK, (r + 1) * FLASH_CHUNK)
                    p_sc[rows, q0:] = jnp.exp2(s_sc[rows, q0:] - m_loc).astype(BF16)
                vt = vt_ref[0, h * dv:(h + 1) * dv, pl.ds(ks, tk)]
                vt_ext = jnp.concatenate([vt, jnp.ones((ONES_ROWS, tk), BF16)], axis=0)
                pvl = _dot(vt_ext, p_sc[:, q0:])
                if q0:
                    m_loc = jnp.concatenate([jnp.full((1, q0), NEG_INF, F32), m_loc], axis=1)
                    pvl = jnp.concatenate([jnp.zeros((dv + ONES_ROWS, q0), F32), pvl], axis=1)
                locs.append((m_loc, pvl))
        for h in range(2):
            mine = locs[h * FLASH_KEY_SPLIT:(h + 1) * FLASH_KEY_SPLIT]
            m_prev = m_sc[h]
            m_new = m_prev
            for m_loc, _ in mine:
                m_new = jnp.maximum(m_new, m_loc)
            a_prev = jnp.exp2(m_prev - m_new)
            acc = acc_scs[h][...] * a_prev
            l = l_sc[h] * a_prev
            for m_loc, pvl in mine:
                a_loc = jnp.exp2(m_loc - m_new)
                acc = acc + pvl[:dv] * a_loc
                l = l + pvl[dv:dv + 1] * a_loc
            acc_scs[h][...] = acc
            l_sc[h] = l
            m_sc[h] = m_new

    per_trip = FLASH_BLOCKS_PER_TRIP

    def body(jj, carry):
        for u in range(per_trip):
            step(per_trip * jj + u, False)
        return carry

    n_trips = i // per_trip
    lax.fori_loop(0, n_trips, body, 0)

    def single(j, carry):
        step(j, False)
        return carry

    lax.fori_loop(n_trips * per_trip, i, single, 0)
    step(i, True)
    for h in range(2):
        o_ref[0, h * dv:(h + 1) * dv, pl.ds(qs, tq)] = (acc_scs[h][...] / l_sc[h]).astype(o_ref.dtype)


def _flash_attention(q, k, vt, n_heads, tq, name):
    B, S, _ = q.shape
    n_pairs = n_heads // 2
    dv2 = 2 * HEAD_LANES
    spec0 = pl.BlockSpec((1, S, LANES), lambda b, p: (b, 0, 2 * p))
    spec1 = pl.BlockSpec((1, S, LANES), lambda b, p: (b, 0, 2 * p + 1))
    vspec = pl.BlockSpec((1, dv2, S), lambda b, p: (b, p, 0))
    return pl.pallas_call(
        functools.partial(_flash_kernel, tq=tq, n_q=S // tq),
        out_shape=jax.ShapeDtypeStruct((B, n_heads * HEAD_LANES, S), BF16),
        grid=(B, n_pairs),
        in_specs=[spec0, spec1, spec0, spec1, vspec],
        out_specs=vspec,
        scratch_shapes=[pltpu.VMEM((tq // FLASH_KEY_SPLIT, tq), F32)] * (2 * FLASH_KEY_SPLIT)
        + [pltpu.VMEM((tq // FLASH_KEY_SPLIT, tq), BF16)] * (2 * FLASH_KEY_SPLIT)
        + [pltpu.VMEM((2, 1, tq), F32), pltpu.VMEM((2, 1, tq), F32),
                        pltpu.VMEM((HEAD_LANES, tq), F32), pltpu.VMEM((HEAD_LANES, tq), F32)],
        compiler_params=_cparams(("parallel", "parallel")),
        name=name,
    )(q, q, k, k, vt)


DIL_PLANES = 16
DIL_TQ = 128
DIL_UNROLL = 8


def _plane_slot(r16):
    return 4 * (r16 % 4) + r16 // 4


def _dilated_geometry(dil):
    planes = DIL_PLANES // dil
    rows = DIL_TQ // planes
    if dil == 1:
        order = [(c // 4) + 4 * (c % 4) for c in range(planes)]
    else:
        order = list(range(planes))
    return planes, rows, order


def _dilated_bias(dil, first):
    planes, rows, order = _dilated_geometry(dil)
    krows = 2 * rows
    a = np.arange(planes * rows)
    b = np.arange(planes * krows)
    uq = np.asarray(order)[a // rows] + planes * (a % rows + (0 if first else rows))
    uk = np.asarray(order)[b // krows] + planes * (b % krows)
    diff = uq[:, None] - uk[None, :]
    valid = (diff >= 0) & (diff <= DIL_TQ)
    return jnp.asarray(np.where(valid, 0.0, NEG_INF), F32)


def _dilated_kernel(q_ref, k_ref, v_ref, b1f_ref, b1r_ref, b4f_ref, b4r_ref, b16f_ref, b16r_ref, o_ref,
                    q_st, k_st, v_st, m_sc, l_sc, acc_sc, *, rows_per_plane):
    pair = pl.program_id(1)
    n_pairs = DIL_HEADS // 2
    lane = lax.broadcasted_iota(jnp.int32, (1, LANES), 1)
    lo = lane < HEAD_LANES
    q_lo = (lane % HEAD_LANES) < (HEAD_LANES // 2)
    tq = DIL_TQ

    def plane_lanes(r16):
        return pl.ds(pl.multiple_of((r16 * n_pairs + pair) * LANES, LANES), LANES)

    for r16 in range(DIL_PLANES):
        slot = _plane_slot(r16)
        q_st[slot] = q_ref[0, :, plane_lanes(r16)].astype(F32)
        k_st[slot] = k_ref[0, :, plane_lanes(r16)].astype(F32)
        v_st[slot] = v_ref[0, :, plane_lanes(r16)].astype(F32)
    m_sc[...] = jnp.full_like(m_sc, NEG_INF)
    l_sc[...] = jnp.zeros_like(l_sc)
    acc_sc[...] = jnp.zeros_like(acc_sc)

    def block(planes, rows, slot0, row0, first, bias_ref):
        qsl = (pl.ds(slot0, planes), pl.ds(row0, rows), slice(None))
        k_row0 = row0 if first else row0 - rows
        ksl = (pl.ds(slot0, planes), pl.ds(k_row0, 2 * rows), slice(None))
        q = q_st[qsl].reshape(tq, LANES)
        k = k_st[ksl].reshape(-1, LANES).astype(BF16)
        v = v_st[ksl].reshape(-1, LANES).astype(BF16)
        zero = jnp.zeros_like(q)
        qs = (jnp.where(q_lo, q, zero).astype(BF16), jnp.where(q_lo, zero, q).astype(BF16))
        bias = bias_ref[...]
        v_ext = jnp.concatenate([v, jnp.ones_like(v)], axis=1)
        maxes, sums, pvs = [], [], []
        for h in range(2):
            s = _nt_dot(qs[h], k) + bias
            mh = jnp.max(s, axis=1, keepdims=True)
            p = jnp.exp2(s - mh)
            maxes.append(mh)
            pvl = _dot(p.astype(BF16), v_ext)
            pvs.append(pvl[:, :LANES])
            sums.append(pvl[:, LANES:])
        m_blk = jnp.where(lo, maxes[0], maxes[1])
        m_old = m_sc[qsl].reshape(tq, LANES)
        m_new = jnp.maximum(m_old, m_blk)
        a_old = jnp.exp2(m_old - m_new)
        a_blk = jnp.exp2(m_blk - m_new)
        l_new = a_old * l_sc[qsl].reshape(tq, LANES) + a_blk * jnp.where(lo, sums[0], sums[1])
        acc_new = a_old * acc_sc[qsl].reshape(tq, LANES) + a_blk * jnp.where(lo, pvs[0], pvs[1])
        m_sc[qsl] = m_new.reshape(planes, rows, LANES)
        l_sc[qsl] = l_new.reshape(planes, rows, LANES)
        acc_sc[qsl] = acc_new.reshape(planes, rows, LANES)

    bias_refs = {1: (b1f_ref, b1r_ref), 4: (b4f_ref, b4r_ref), 16: (b16f_ref, b16r_ref)}
    for window, dil in DILATED_PAIRS:
        assert window // dil == tq
        planes, rows, _ = _dilated_geometry(dil)
        nblk = rows_per_plane // rows
        bf_ref, br_ref = bias_refs[dil]

        def first_body(sub, carry, planes=planes, rows=rows, bf_ref=bf_ref):
            block(planes, rows, sub * planes, 0, True, bf_ref)
            return carry

        lax.fori_loop(0, dil, first_body, 0, unroll=min(dil, DIL_UNROLL))

        if nblk > 1:
            def rest_body(t, carry, planes=planes, rows=rows, nblk=nblk, br_ref=br_ref):
                sub = t // (nblk - 1)
                blk = t % (nblk - 1) + 1
                block(planes, rows, sub * planes, pl.multiple_of(blk * rows, rows), False, br_ref)
                return carry

            lax.fori_loop(0, dil * (nblk - 1), rest_body, 0, unroll=DIL_UNROLL)

    for r16 in range(DIL_PLANES):
        slot = _plane_slot(r16)
        o_ref[0, :, plane_lanes(r16)] = (acc_sc[slot] / l_sc[slot]).astype(o_ref.dtype)


def _dilated_attention(qc, kc, vc):
    B, rpp, width = qc.shape
    n_pairs = DIL_HEADS // 2
    assert rpp % DIL_TQ == 0 and rpp >= 2 * DIL_TQ

    spec = pl.BlockSpec((1, rpp, width), lambda b, p: (b, 0, 0))
    biases = [_dilated_bias(d, f) for _, d in DILATED_PAIRS for f in (True, False)]
    bspecs = [pl.BlockSpec(bb.shape, lambda b, p: (0, 0)) for bb in biases]
    plane = pltpu.VMEM((DIL_PLANES, rpp, LANES), F32)
    return pl.pallas_call(
        functools.partial(_dilated_kernel, rows_per_plane=rpp),
        out_shape=jax.ShapeDtypeStruct((B, rpp, width), BF16),
        grid=(B, n_pairs),
        in_specs=[spec, spec, spec] + bspecs,
        out_specs=spec,
        scratch_shapes=[plane] * 6,
        compiler_params=_cparams(("parallel", "arbitrary")),
        name="dilated",
    )(qc, kc, vc, *biases)


def _tn_dot(a_t, b):
    return lax.dot_general(a_t, b, (((0,), (0,)), ((), ())), preferred_element_type=F32)


def _mix_out_kernel(oat_ref, obt_ref, oc_ref, x_ref, wa_ref, wb_ref, wc_ref, g_ref, b_ref, o_ref, *oc_scs):
    nd = len(oc_scs) * LANES
    rows_per_plane = oc_scs[0].shape[0] // DIL_PLANES
    for r in range(DIL_PLANES):
        rows = pl.ds(r, rows_per_plane, stride=DIL_PLANES)
        for pblk, sc in enumerate(oc_scs):
            lanes = slice(r * nd + pblk * LANES, r * nd + (pblk + 1) * LANES)
            sc[rows, :] = oc_ref[0, :, lanes].astype(F32)
    oc = jnp.concatenate([sc[...] for sc in oc_scs], axis=1).astype(BF16)
    y = (_tn_dot(oat_ref[0], wa_ref[...]) + _tn_dot(obt_ref[0], wb_ref[...])
         + _dot(oc, wc_ref[...]))
    o_ref[0] = _layer_norm(DEEPNORM_ALPHA * x_ref[0] + y, g_ref[...], b_ref[...])


def _mix_out(oa_t, ob_t, oc, x, wa, wb, wc, g, b, l, tm):
    B, S, D = x.shape

    def cols(a):
        return pl.BlockSpec((1, a.shape[1], tm), lambda bb, i: (bb, 0, i))

    def rows(a):
        return pl.BlockSpec((1, tm, a.shape[2]), lambda bb, i: (bb, i, 0))

    return pl.pallas_call(
        _mix_out_kernel,
        out_shape=jax.ShapeDtypeStruct((B, S, D), F32),
        grid=(B, S // tm),
        in_specs=[cols(oa_t), cols(ob_t),
                  pl.BlockSpec((1, tm // DIL_PLANES, oc.shape[2]), lambda bb, i: (bb, i, 0)),
                  rows(x)] + [_layer_slab(a, l) for a in (wa, wb, wc, g, b)],
        out_specs=rows(x),
        scratch_shapes=[pltpu.VMEM((tm, LANES), F32)] * (oc.shape[2] // DIL_PLANES // LANES),
        compiler_params=_cparams(("parallel", "parallel")),
        name="mix_out",
    )(oa_t, ob_t, oc, x, wa, wb, wc, g, b)


def _mem_kv_kernel(mem_ref, w_ref, k_ref, v_ref):
    kv = _dot(mem_ref[0].astype(BF16), w_ref[...])
    d = k_ref.shape[-1]
    k_ref[0] = kv[:, :d].astype(BF16)
    v_ref[0] = kv[:, d:].astype(BF16)


def _mem_kv(mem, w_kv, l):
    B, M, D = mem.shape
    spec = pl.BlockSpec((1, M, D), lambda b: (b, 0, 0))
    return pl.pallas_call(
        _mem_kv_kernel,
        out_shape=(jax.ShapeDtypeStruct((B, M, D), BF16),) * 2,
        grid=(B,),
        in_specs=[spec, _layer_slab(w_kv, l)],
        out_specs=(spec, spec),
        compiler_params=_cparams(("parallel",)),
        name="mem_kv",
    )(mem, w_kv)


def _xattn_kernel(x_ref, k_ref, v_ref, wq_ref, wo_ref, g_ref, b_ref, o_ref):
    x = x_ref[0]
    d = x.shape[-1]
    dh = d // XATTN_HEADS
    q = (_dot(x.astype(BF16), wq_ref[...]) * (dh ** -0.5)).astype(BF16)
    outs = []
    for h in range(XATTN_HEADS):
        sl = slice(h * dh, (h + 1) * dh)
        s = _nt_dot(q[:, sl], k_ref[0, :, sl])
        p = jnp.exp(s - jnp.max(s, axis=1, keepdims=True))
        o = _dot(p.astype(BF16), v_ref[0, :, sl])
        outs.append((o / jnp.sum(p, axis=1, keepdims=True)).astype(BF16))
    o_all = jnp.concatenate(outs, axis=1)
    y = _dot(o_all, wo_ref[...])
    o_ref[0] = _layer_norm(DEEPNORM_ALPHA * x + y, g_ref[...], b_ref[...])


def _xattn(x, k_mem, v_mem, w_q, w_o, g, b, l, tm):
    B, S, D = x.shape
    M = k_mem.shape[1]
    xspec = pl.BlockSpec((1, tm, D), lambda bb, i: (bb, i, 0))
    mspec = pl.BlockSpec((1, M, D), lambda bb, i: (bb, 0, 0))

    return pl.pallas_call(
        _xattn_kernel,
        out_shape=jax.ShapeDtypeStruct((B, S, D), F32),
        grid=(B, S // tm),
        in_specs=[xspec, mspec, mspec] + [_layer_slab(a, l) for a in (w_q, w_o, g, b)],
        out_specs=xspec,
        compiler_params=_cparams(("parallel", "parallel")),
        name="xattn",
    )(x, k_mem, v_mem, w_q, w_o, g, b)


def _split_bf16(a):
    hi = a.astype(BF16)
    return hi, (a - hi.astype(F32)).astype(BF16)


def _router_gates_t(x, rw_t, rbias):
    x_hi, x_lo = _split_bf16(x)
    w_hi, w_lo = _split_bf16(rw_t)
    t1 = _nt_dot(jnp.concatenate([w_hi, w_lo], axis=0), x_hi)
    logits_t = t1[:N_EXPERTS] + (t1[N_EXPERTS:] + _nt_dot(w_hi, x_lo))
    scores = 1.0 / (1.0 + jnp.exp(-logits_t))
    biased = scores + rbias
    sc = [scores[e:e + 1, :] for e in range(N_EXPERTS)]
    bs = [biased[e:e + 1, :] for e in range(N_EXPERTS)]
    epg = EXPERTS_PER_GROUP

    def beats(a, b, a_first):
        return (a >= b) if a_first else (a > b)

    grp = []
    for gi in range(N_GROUPS):
        v = bs[gi * epg:(gi + 1) * epg]
        best = None
        for a in range(epg):
            for b in range(a + 1, epg):
                pair = v[a] + v[b]
                best = pair if best is None else jnp.maximum(best, pair)
        grp.append(best)
    gates, g_sels = [], []
    for gi in range(N_GROUPS):
        g_sel = None
        for gj in range(N_GROUPS):
            if gj == gi:
                continue
            w = beats(grp[gi], grp[gj], gi < gj)
            g_sel = w if g_sel is None else (g_sel & w)
        g_sels.append(g_sel)
        in_top = []
        for a in range(epg):
            ea = gi * epg + a
            n_above = jnp.zeros(bs[ea].shape, jnp.int32)
            for b in range(epg):
                if b == a:
                    continue
                eb = gi * epg + b
                n_above = n_above + beats(bs[eb], bs[ea], b < a).astype(jnp.int32)
            in_top.append(g_sel & (n_above < 2))
        denom = None
        for a in range(epg):
            term = jnp.where(in_top[a], sc[gi * epg + a], 0.0)
            denom = term if denom is None else denom + term
        denom = jnp.where(g_sel, denom, 1.0)
        for a in range(epg):
            gates.append(jnp.where(in_top[a], sc[gi * epg + a] / denom, 0.0))
    return gates, g_sels


def _moe_kernel(x_ref, rwt_ref, rb_ref, wg_ref, wu_ref, wd_ref, g_ref, b_ref, o_ref,
                gt_sc, sel_sc, rank_sc, acc_sc, xb_sc, *, tm):
    grp = pl.program_id(1)
    epg = EXPERTS_PER_GROUP

    @pl.when(grp == 0)
    def _():
        ts = min(tm, ROUTER_ROWS)
        before = (lax.broadcasted_iota(jnp.int32, (ts, ts), 0)
                  < lax.broadcasted_iota(jnp.int32, (ts, ts), 1)).astype(BF16)
        seen = jnp.zeros((8, 1), F32)
        for r in range(tm // ts):
            cols = slice(r * ts, (r + 1) * ts)
            x = x_ref[cols, :]
            gates, g_sels = _router_gates_t(x, rwt_ref[...], rb_ref[...])
            zeros4 = [jnp.zeros((1, ts), F32)] * (8 - epg)
            for gi in range(N_GROUPS):
                gt_sc[gi, :, cols] = jnp.concatenate(gates[gi * epg:(gi + 1) * epg] + zeros4, axis=0)
            sel8 = jnp.concatenate([s.astype(F32) for s in g_sels] + [jnp.zeros((1, ts), F32)] * (8 - N_GROUPS),
                                   axis=0)
            sel_sc[:, cols] = sel8
            rank_sc[:, cols] = _dot(sel8.astype(BF16), before) + seen
            seen = seen + jnp.sum(sel8, axis=1, keepdims=True)
            xb_sc[cols, :] = x.astype(BF16)
        acc_sc[...] = jnp.zeros_like(acc_sc)

    sel = sel_sc[pl.ds(grp, 1), :] > 0.5
    rank = rank_sc[pl.ds(grp, 1), :].astype(jnp.int32)
    n_tok = jnp.sum(sel_sc[pl.ds(grp, 1), :]).astype(jnp.int32)
    gt = gt_sc[grp]
    g_hi = gt.astype(BF16).astype(F32)
    g_r = gt - g_hi
    g_mid = g_r.astype(BF16).astype(F32)
    g_lo = (g_r - g_mid).astype(BF16).astype(F32)
    g3 = jnp.concatenate([g_hi, g_mid, g_lo, jnp.zeros_like(g_hi)], axis=0).astype(BF16)
    wd_all = wd_ref[...].reshape(epg * D_EXPERT, -1)

    def chunk(first_slot, ch):
        slot = lax.broadcasted_iota(jnp.int32, (ch, tm), 0) + first_slot
        onehot = jnp.where((slot == rank) & sel, 1.0, 0.0).astype(BF16)
        xg = _dot(onehot, xb_sc[...]).astype(BF16)
        gs3 = _nt_dot(onehot, g3)
        gs = gs3[:, 0:8] + (gs3[:, 8:16] + gs3[:, 16:24])
        acts = []
        for e in range(epg):
            hg = _dot(xg, wg_ref[e])
            hu = _dot(xg, wu_ref[e])
            acts.append(((hg / (1.0 + jnp.exp(-hg))) * hu * gs[:, e:e + 1]).astype(BF16))
        y = _dot(jnp.concatenate(acts, axis=1), wd_all).astype(BF16)
        acc_sc[...] += _tn_dot(onehot, y)

    n_wide = (n_tok + (MOE_CHUNK - MOE_TAIL_CHUNK - 1)) // MOE_CHUNK

    def wide(c, carry):
        chunk(c * MOE_CHUNK, MOE_CHUNK)
        return carry

    lax.fori_loop(0, n_wide, wide, 0)

    @pl.when(n_tok > n_wide * MOE_CHUNK)
    def _():
        chunk(n_wide * MOE_CHUNK, MOE_TAIL_CHUNK)

    @pl.when(grp == N_GROUPS - 1)
    def _():
        o_ref[...] = _layer_norm(DEEPNORM_ALPHA * x_ref[...] + acc_sc[...], g_ref[...], b_ref[...])


ROUTER_ROWS = 512
MOE_CHUNK = 256
MOE_TAIL_CHUNK = 128


def _moe(x2d, rw_t, rbias, wg, wu, wd, g, b, l, tm):
    T, D = x2d.shape
    xspec = pl.BlockSpec((tm, D), lambda i, e: (i, 0))
    epg = EXPERTS_PER_GROUP

    def full(a):
        return pl.BlockSpec(a.shape, lambda i, e: (0, 0))

    return pl.pallas_call(
        functools.partial(_moe_kernel, tm=tm),
        out_shape=jax.ShapeDtypeStruct((T, D), F32),
        grid=(T // tm, N_GROUPS),
        in_specs=[xspec, full(rw_t), full(rbias),
                  pl.BlockSpec((None, epg, D, D_EXPERT), lambda i, e: (l, e, 0, 0)),
                  pl.BlockSpec((None, epg, D, D_EXPERT), lambda i, e: (l, e, 0, 0)),
                  pl.BlockSpec((None, epg, D_EXPERT, D), lambda i, e: (l, e, 0, 0)),
                  _layer_slab(g, l), _layer_slab(b, l)],
        out_specs=xspec,
        scratch_shapes=[pltpu.VMEM((N_GROUPS, 8, tm), F32), pltpu.VMEM((8, tm), F32), pltpu.VMEM((8, tm), F32),
                        pltpu.VMEM((tm, D), F32), pltpu.VMEM((tm, D), BF16)],
        compiler_params=_cparams(("parallel", "arbitrary")),
        name="moe",
    )(x2d, rw_t, rbias, wg, wu, wd, g, b)


def _rope_constants():
    half_d = DIL_HEAD_DIM // 2
    inv_d = ROPE_THETA ** (-jnp.arange(half_d, dtype=F32) / half_d)
    half_a = MLA_ROPE_DIM // 2
    inv_a = ROPE_THETA ** (-jnp.arange(half_a, dtype=F32) / half_a)
    lane = np.arange(LANES)
    invf_d = inv_d[lane % half_d].reshape(1, LANES)
    sgn_d = jnp.asarray(np.where(lane < HEAD_LANES, -1.0, 1.0).reshape(1, LANES), F32)
    in_rope = (lane >= MLA_NOPE_DIM) & (lane < MLA_NOPE_DIM + MLA_ROPE_DIM)
    invf_a = jnp.where(jnp.asarray(in_rope), inv_a[(lane - MLA_NOPE_DIM) % half_a], 0.0).reshape(1, LANES)
    sgn_a = jnp.ones((1, LANES), F32)
    return invf_a, sgn_a, invf_d, sgn_d


def _layer_weights(w_in, b_forget, q_gain, kv_gain, w_uq, w_ukv):
    D = w_in.shape[0]
    cuts = np.cumsum([MLA_Q_RANK, MLA_KV_RANK, MLA_ROPE_DIM, 3 * FOX_HEADS * FOX_HEAD_DIM, FOX_HEADS])
    w_cq, w_ckv, w_kr, w_fox, w_ff, w_dil = jnp.split(w_in, cuts.tolist(), axis=1)
    half = MLA_ROPE_DIM // 2
    w_kr_rot = jnp.concatenate([-w_kr[:, half:], w_kr[:, :half]], axis=1)

    def place_rope(w):
        return jnp.pad(w, ((0, 0), (MLA_NOPE_DIM, LANES - MLA_NOPE_DIM - MLA_ROPE_DIM)))

    def place_gate(w):
        rep = jnp.repeat(w, 3, axis=1)
        return jnp.pad(rep, ((0, 0), (FOX_HEAD_DIM, LANES - FOX_HEAD_DIM - 3 * FOX_HEADS)))

    wa = jnp.concatenate([w_cq, w_ckv, place_rope(w_kr), place_rope(w_kr_rot),
                          place_gate(w_ff)], axis=1)

    nd = DIL_HEADS * DIL_HEAD_DIM

    def pair_layout(w):
        half = DIL_HEAD_DIM // 2
        w5 = w.reshape(D, DIL_HEADS // 2, 2, 2, half)
        return w5.transpose(0, 1, 3, 2, 4).reshape(D, nd)

    wdil = jnp.concatenate([pair_layout(w_dil[:, :nd]), pair_layout(w_dil[:, nd:2 * nd]), w_dil[:, 2 * nd:]],
                           axis=1)

    dq = MLA_NOPE_DIM + MLA_ROPE_DIM
    uq = w_uq.reshape(MLA_Q_RANK, MLA_HEADS, dq)
    uq_rope = uq[:, :, MLA_NOPE_DIM:]
    uq_rot = jnp.concatenate([-uq_rope[:, :, half:], uq_rope[:, :, :half]], axis=2)
    pad_tail = LANES - dq
    uq_plain = jnp.pad(uq, ((0, 0), (0, 0), (0, pad_tail))).reshape(MLA_Q_RANK, MLA_HEADS * LANES)
    uq_rotp = jnp.pad(uq_rot, ((0, 0), (0, 0), (MLA_NOPE_DIM, pad_tail))).reshape(MLA_Q_RANK, MLA_HEADS * LANES)
    wuq = jnp.concatenate([uq_plain, uq_rotp], axis=1)

    ukv = w_ukv.reshape(MLA_KV_RANK, MLA_HEADS, MLA_NOPE_DIM + MLA_V_DIM)
    wuk = jnp.pad(ukv[:, :, :MLA_NOPE_DIM], ((0, 0), (0, 0), (0, LANES - MLA_NOPE_DIM)))
    wuk = wuk.reshape(MLA_KV_RANK, MLA_HEADS * LANES)
    wuv = ukv[:, :, MLA_NOPE_DIM:].reshape(MLA_KV_RANK, MLA_HEADS * MLA_V_DIM)

    bf = place_gate(b_forget.astype(F32).reshape(1, -1))
    nf = FOX_HEADS * FOX_HEAD_DIM
    return dict(wa=wa.astype(BF16), wfox=w_fox[:, :2 * nf].astype(BF16),
                wfoxv_t=w_fox[:, 2 * nf:].T.astype(BF16), wdil=wdil.astype(BF16),
                qg=q_gain.reshape(1, -1).astype(F32), kvg=kv_gain.reshape(1, -1).astype(F32),
                wuq=wuq.astype(BF16), wuk=wuk.astype(BF16), wuv_t=wuv.T.astype(BF16), bf=bf)


def _tiles(S):
    tm = min(512, S)
    return dict(rope=min(512, S), proj=tm, flash=min(512, S), mix=tm, xattn=tm, moe=min(1024, S))


def kernel(x, mem, positions, w_in, b_forget, mla_q_gain, mla_kv_gain, mla_w_uq, mla_w_ukv, w_mix_out, ln_mix_g, ln_mix_b, xattn_w_q, xattn_w_kv, xattn_w_o, ln_mem_g, ln_mem_b, router_w, router_bias, expert_w_gate, expert_w_up, expert_w_down, ln_ffn_g, ln_ffn_b):
    B, S, D = x.shape
    depth = w_in.shape[0]
    t = _tiles(S)
    invf_a, sgn_a, invf_d, sgn_d = _rope_constants()
    cos_a, sin_a = _rope_tables(positions, invf_a, sgn_a, t["rope"])
    cos_d, sin_d = _rope_tables(positions, invf_d, sgn_d, t["rope"])
    tabs = (cos_a, sin_a, cos_d, sin_d)
    rw_t = router_w.T.astype(F32)
    rbias = router_bias.reshape(-1, 1).astype(F32)

    def rows(v):
        return v.reshape(depth, 1, -1).astype(F32)

    wts = jax.vmap(_layer_weights)(w_in, b_forget, mla_q_gain, mla_kv_gain, mla_w_uq, mla_w_ukv)
    na, nb = MLA_HEADS * MLA_V_DIM, FOX_HEADS * FOX_HEAD_DIM
    w_mix = w_mix_out.astype(BF16)
    w_mix_a, w_mix_b, w_mix_c = w_mix[:, :na], w_mix[:, na:na + nb], w_mix[:, na + nb:]
    w_q, w_kv, w_o = xattn_w_q.astype(BF16), xattn_w_kv.astype(BF16), xattn_w_o.astype(BF16)
    wg, wu, wd = expert_w_gate.astype(BF16), expert_w_up.astype(BF16), expert_w_down.astype(BF16)
    g_mix, b_mix, g_mem, b_mem = rows(ln_mix_g), rows(ln_mix_b), rows(ln_mem_g), rows(ln_mem_b)
    g_ffn, b_ffn = rows(ln_ffn_g), rows(ln_ffn_b)

    for l in range(depth):
        qa, ka, va_t, qb, kb, vb_t, qc, kc, vc = _proj(x, wts, l, tabs, t["proj"])
        oa_t = _flash_attention(qa, ka, va_t, MLA_HEADS, t["flash"], "mla_flash")
        ob_t = _flash_attention(qb, kb, vb_t, FOX_HEADS, t["flash"], "fox_flash")
        oc = _dilated_attention(qc, kc, vc)
        x = _mix_out(oa_t, ob_t, oc, x, w_mix_a, w_mix_b, w_mix_c, g_mix, b_mix, l, t["mix"])
        k_mem, v_mem = _mem_kv(mem, w_kv, l)
        x = _xattn(x, k_mem, v_mem, w_q, w_o, g_mem, b_mem, l, t["xattn"])
        x2d = _moe(x.reshape(B * S, D), rw_t, rbias, wg, wu, wd, g_ffn, b_ffn, l, t["moe"])
        x = x2d.reshape(B, S, D)
    return x
```

```python
import functools
import math

import numpy as np
import jax
import jax.numpy as jnp
from jax import lax
from jax.experimental import pallas as pl
from jax.experimental.pallas import tpu as pltpu

F32 = jnp.float32
BF16 = jnp.bfloat16
HIGHEST = lax.Precision.HIGHEST

D_MODEL = 1024
DEPTH = 4
MLA_HEADS = 4
MLA_Q_RANK = 256
MLA_KV_RANK = 128
MLA_NOPE_DIM = 64
MLA_ROPE_DIM = 32
MLA_V_DIM = 64
FOX_HEADS = 6
FOX_HEAD_DIM = 64
DIL_HEADS = 6
DIL_HEAD_DIM = 64
DILATED_PAIRS = ((128, 1), (512, 4), (2048, 16))
XATTN_HEADS = 4
N_EXPERTS = 16
N_GROUPS = 4
EXPERTS_PER_GROUP = N_EXPERTS // N_GROUPS
D_EXPERT = 256
ROPE_THETA = 10000.0
NORM_EPS = 1e-5
NEG_INF = -1e30
DEEPNORM_ALPHA = (2 * DEPTH) ** 0.25
LOG2E = math.log2(math.e)

LANES = 128
HEAD_LANES = 64
VMEM_LIMIT_BYTES = 52 * 1024 * 1024

A_CQ = 0
A_CKV = A_CQ + MLA_Q_RANK
A_KR = A_CKV + MLA_KV_RANK
A_KRR = A_KR + LANES
A_FF = A_KRR + LANES
A_COLS = A_FF + LANES


def _cparams(sem):
    return pltpu.CompilerParams(dimension_semantics=sem, vmem_limit_bytes=VMEM_LIMIT_BYTES)


def _layer_slab(stacked, l):
    zeros = (0,) * (stacked.ndim - 1)
    return pl.BlockSpec((None,) + stacked.shape[1:], lambda *_: (l,) + zeros)


def _nt_dot(a, b, precision=None):
    return lax.dot_general(a, b, (((1,), (1,)), ((), ())), precision=precision,
                           preferred_element_type=F32)


def _dot(a, b, precision=None):
    return jnp.dot(a, b, precision=precision, preferred_element_type=F32)


def _layer_norm(y, g, b):
    mu = jnp.mean(y, axis=-1, keepdims=True)
    yc = y - mu
    var = jnp.mean(yc * yc, axis=-1, keepdims=True)
    return yc * lax.rsqrt(var + NORM_EPS) * g + b


def _rms_norm(y, g):
    ms = jnp.mean(y * y, axis=-1, keepdims=True)
    return y * lax.rsqrt(ms + NORM_EPS) * g


def _rope_table_kernel(pos_ref, invf_ref, cosa_ref, sina_ref, cosd_ref, sind_ref):
    lane = lax.broadcasted_iota(jnp.int32, (1, LANES), 1)
    ang = pos_ref[0].astype(F32) * invf_ref[...]
    cs = jnp.cos(ang)
    sn = jnp.sin(ang)
    in_a = (lane >= MLA_NOPE_DIM) & (lane < MLA_NOPE_DIM + MLA_ROPE_DIM)
    cosa_ref[0] = jnp.where(in_a, cs, 1.0)
    sina_ref[0] = jnp.where(in_a, sn, 0.0)
    half = DIL_HEAD_DIM // 2
    c32 = jnp.where(lane < half, cs, 0.0)
    s32 = jnp.where(lane < half, sn, 0.0)
    cd, sd = c32, s32
    for shift in (half, 2 * half, 3 * half):
        cd = cd + pltpu.roll(c32, shift, 1)
        sd = sd + pltpu.roll(s32, shift, 1)
    cosd_ref[0] = cd
    sind_ref[0] = jnp.where(lane < HEAD_LANES, -sd, sd)


def _rope_tables(positions, invf, ts):
    B, S = positions.shape
    pos3 = positions.reshape(B, S, 1)
    spec = pl.BlockSpec((1, ts, LANES), lambda b, i: (b, i, 0))
    vec = pl.BlockSpec((1, LANES), lambda b, i: (0, 0))
    return pl.pallas_call(
        _rope_table_kernel,
        out_shape=(jax.ShapeDtypeStruct((B, S, LANES), F32),) * 4,
        grid=(B, S // ts),
        in_specs=[pl.BlockSpec((1, ts, 1), lambda b, i: (b, i, 0)), vec],
        out_specs=(spec,) * 4,
        compiler_params=_cparams(("parallel", "parallel")),
        name="rope_tables",
    )(pos3, invf)


def _proj_kernel(x_ref, wa_ref, wfox_ref, wfoxv_ref, wdil_ref, qg_ref, kvg_ref, wuq_ref, wuk_ref,
                 wuv_ref, bf_ref, cosa_ref, sina_ref, cosd_ref, sind_ref,
                 qa_ref, ka_ref, va_ref, qb_ref, kb_ref, vb_ref,
                 qc_ref, kc_ref, vc_ref, carry_c, *dil_scs, tm):
    i = pl.program_id(1)
    xb = x_ref[0].astype(BF16)

    ha = _dot(xb, wa_ref[...])
    cqn = _rms_norm(ha[:, A_CQ:A_CQ + MLA_Q_RANK], qg_ref[...]).astype(BF16)
    ckvn = _rms_norm(ha[:, A_CKV:A_CKV + MLA_KV_RANK], kvg_ref[...]).astype(BF16)
    cos_a = cosa_ref[0]
    sin_a = sina_ref[0]
    q2 = _dot(cqn, wuq_ref[...])
    k_nope = _dot(ckvn, wuk_ref[...])
    k_rot = ha[:, A_KR:A_KR + LANES] * cos_a + ha[:, A_KRR:A_KRR + LANES] * sin_a
    q_scale = (MLA_NOPE_DIM + MLA_ROPE_DIM) ** -0.5 * LOG2E
    n_half = MLA_HEADS * LANES
    for h in range(MLA_HEADS):
        sl = slice(h * LANES, (h + 1) * LANES)
        qh = q2[:, sl] * cos_a + q2[:, n_half + h * LANES:n_half + (h + 1) * LANES] * sin_a
        qa_ref[0, :, sl] = (qh * q_scale).astype(BF16)
        ka_ref[0, :, sl] = (k_nope[:, sl] + k_rot).astype(BF16)
    va_ref[0] = _nt_dot(wuv_ref[...], ckvn).astype(BF16)

    lane = lax.broadcasted_iota(jnp.int32, (1, LANES), 1)
    bias_lanes = (lane >= FOX_HEAD_DIM) & (lane < FOX_HEAD_DIM + 3 * FOX_HEADS)
    z = ha[:, A_FF:A_FF + LANES] + bf_ref[...]
    logf = jnp.minimum(z, 0.0) - jnp.log(1.0 + jnp.exp(-jnp.abs(z)))
    logf = jnp.where(bias_lanes, logf, 0.0)

    @pl.when(i == 0)
    def _():
        carry_c[...] = jnp.zeros_like(carry_c)

    p0 = logf.astype(BF16).astype(F32)
    r0 = logf - p0
    p1 = r0.astype(BF16).astype(F32)
    p2 = (r0 - p1).astype(BF16)
    packed = jnp.concatenate([(p0 + pltpu.roll(p1, HEAD_LANES, 1)).astype(BF16), p2], axis=1)
    r = lax.broadcasted_iota(jnp.int32, (tm, tm), 0)
    c = lax.broadcasted_iota(jnp.int32, (tm, tm), 1)
    lower = (c <= r).astype(BF16)
    cum = _dot(lower, packed)
    cum_a = cum[:, :LANES]
    fcol = jnp.where(bias_lanes, cum_a + pltpu.roll(cum_a, HEAD_LANES, 1) + cum[:, LANES:], 0.0) + carry_c[...]
    carry_c[...] = fcol[tm - 1:tm, :]

    nb = fcol * (-LOG2E)
    hi = nb.astype(BF16).astype(F32)
    r1 = nb - hi
    mid = r1.astype(BF16).astype(F32)
    lo3 = (r1 - mid).astype(BF16).astype(F32)
    piece = (lane - FOX_HEAD_DIM) % 3
    k_bias = jnp.where(piece == 0, hi, jnp.where(piece == 1, mid, lo3))

    hf = _dot(xb, wfox_ref[...])
    nf = FOX_HEADS * FOX_HEAD_DIM
    dims = lane < FOX_HEAD_DIM
    fq_scale = FOX_HEAD_DIM ** -0.5 * LOG2E
    for h in range(FOX_HEADS):
        blk = slice((h // 2) * LANES, (h // 2 + 1) * LANES)
        out = slice(h * LANES, (h + 1) * LANES)
        qh = hf[:, blk]
        kh = hf[:, nf + (h // 2) * LANES:nf + (h // 2 + 1) * LANES]
        if h % 2:
            qh = pltpu.roll(qh, HEAD_LANES, 1)
            kh = pltpu.roll(kh, HEAD_LANES, 1)
        own = (lane >= FOX_HEAD_DIM + 3 * h) & (lane < FOX_HEAD_DIM + 3 * h + 3)
        qb_ref[0, :, out] = jnp.where(dims, qh * fq_scale, jnp.where(own, 1.0, 0.0)).astype(BF16)
        kb_ref[0, :, out] = jnp.where(dims, kh, k_bias).astype(BF16)
    vb_ref[0] = _nt_dot(wfoxv_ref[...], xb).astype(BF16)

    hd = _dot(xb, wdil_ref[...])
    cos_d = cosd_ref[0]
    sin_d = sind_ref[0]
    nd = DIL_HEADS * DIL_HEAD_DIM
    for pblk in range(nd // LANES):
        sl = slice(pblk * LANES, (pblk + 1) * LANES)
        qh = hd[:, sl]
        kh = hd[:, nd + pblk * LANES:nd + (pblk + 1) * LANES]
        qh = qh * cos_d + pltpu.roll(qh, HEAD_LANES, 1) * sin_d
        kh = kh * cos_d + pltpu.roll(kh, HEAD_LANES, 1) * sin_d
        dil_scs[pblk][...] = qh * (DIL_HEAD_DIM ** -0.5 * LOG2E)
        dil_scs[3 + pblk][...] = kh
        dil_scs[6 + pblk][...] = hd[:, 2 * nd + pblk * LANES:2 * nd + (pblk + 1) * LANES]
    for a, ref in enumerate((qc_ref, kc_ref, vc_ref)):
        for r in range(DIL_PLANES):
            rows = pl.ds(r, tm // DIL_PLANES, stride=DIL_PLANES)
            for pblk in range(nd // LANES):
                lanes = slice(r * nd + pblk * LANES, r * nd + (pblk + 1) * LANES)
                ref[0, :, lanes] = dil_scs[3 * a + pblk][rows, :].astype(BF16)


def _proj(x, wts, l, tabs, tm):
    B, S, D = x.shape
    cos_a, sin_a, cos_d, sin_d = tabs
    grid = (B, S // tm)

    def tok(width, dtype):
        return (jax.ShapeDtypeStruct((B, S, width), dtype),
                pl.BlockSpec((1, tm, width), lambda b, i: (b, i, 0)))

    tab = pl.BlockSpec((1, tm, LANES), lambda b, i: (b, i, 0))
    nd = DIL_HEADS * DIL_HEAD_DIM

    def planes(dtype):
        return (jax.ShapeDtypeStruct((B, S // DIL_PLANES, DIL_PLANES * nd), dtype),
                pl.BlockSpec((1, tm // DIL_PLANES, DIL_PLANES * nd), lambda b, i: (b, i, 0)))

    def tok_t(height, dtype):
        return (jax.ShapeDtypeStruct((B, height, S), dtype),
                pl.BlockSpec((1, height, tm), lambda b, i: (b, 0, i)))

    outs = [tok(MLA_HEADS * LANES, BF16), tok(MLA_HEADS * LANES, BF16),
            tok_t(MLA_HEADS * MLA_V_DIM, BF16),
            tok(FOX_HEADS * LANES, BF16), tok(FOX_HEADS * LANES, BF16),
            tok_t(FOX_HEADS * FOX_HEAD_DIM, BF16),
            planes(BF16), planes(BF16), planes(BF16)]
    w_list = [wts["wa"], wts["wfox"], wts["wfoxv_t"], wts["wdil"], wts["qg"], wts["kvg"], wts["wuq"],
              wts["wuk"], wts["wuv_t"], wts["bf"]]
    return pl.pallas_call(
        functools.partial(_proj_kernel, tm=tm),
        out_shape=tuple(o[0] for o in outs),
        grid=grid,
        in_specs=[pl.BlockSpec((1, tm, D), lambda b, i: (b, i, 0))] + [_layer_slab(w, l) for w in w_list]
        + [tab, tab, tab, tab],
        out_specs=tuple(o[1] for o in outs),
        scratch_shapes=[pltpu.VMEM((1, LANES), F32)] + [pltpu.VMEM((tm, LANES), F32)] * (3 * nd // LANES),
        compiler_params=_cparams(("parallel", "arbitrary")),
        name="proj",
    )(x, *w_list, cos_a, sin_a, cos_d, sin_d)


FLASH_CHUNK = 32


FLASH_BLOCKS_PER_TRIP = 2
FLASH_KEY_SPLIT = 2
ONES_ROWS = 16


def _flash_kernel(q0_ref, q1_ref, k0_ref, k1_ref, vt_ref, o_ref, *scratch, tq, n_q):
    def q_tile(i, carry):
        _flash_q_tile(i, q0_ref, q1_ref, k0_ref, k1_ref, vt_ref, o_ref, scratch, tq=tq)
        return carry

    lax.fori_loop(0, n_q, q_tile, 0)


def _flash_q_tile(i, q0_ref, q1_ref, k0_ref, k1_ref, vt_ref, o_ref, scratch, *, tq):
    qs = pl.multiple_of(i * tq, tq)
    q_refs = (q0_ref, q1_ref)
    k_refs = (k0_ref, k1_ref)
    n_chain = 2 * FLASH_KEY_SPLIT
    s_scs = scratch[:n_chain]
    p_scs = scratch[n_chain:2 * n_chain]
    m_sc, l_sc, acc0_sc, acc1_sc = scratch[2 * n_chain:]
    acc_scs = (acc0_sc, acc1_sc)
    dv = HEAD_LANES
    tk = tq // FLASH_KEY_SPLIT
    n_chunks = tk // FLASH_CHUNK

    m_sc[...] = jnp.full_like(m_sc, NEG_INF)
    l_sc[...] = jnp.zeros_like(l_sc)
    acc0_sc[...] = jnp.zeros_like(acc0_sc)
    acc1_sc[...] = jnp.zeros_like(acc1_sc)

    def step(j, masked):
        m_locs = []
        for h in range(2):
            for c in range(FLASH_KEY_SPLIT):
                ks = pl.multiple_of(j * tq + c * tk, tk)
                q0 = c * tk if masked else 0
                k = k_refs[h][0, pl.ds(ks, tk), :]
                q = q_refs[h][0, pl.ds(qs + q0, tq - q0), :]
                st = _nt_dot(k, q)
                if masked:
                    key = lax.broadcasted_iota(jnp.int32, (tk, tq - q0), 0)
                    qry = lax.broadcasted_iota(jnp.int32, (tk, tq - q0), 1)
                    st = jnp.where(key <= qry, st, NEG_INF)
                s_scs[h * FLASH_KEY_SPLIT + c][:, q0:] = st
                m8 = jnp.max(st.reshape(tk // 8, 8, tq - q0), axis=0)
                m_locs.append(jnp.max(m8, axis=0, keepdims=True))
        locs = []
        for h in range(2):
            for c in range(FLASH_KEY_SPLIT):
                ks = pl.multiple_of(j * tq + c * tk, tk)
                q0 = c * tk if masked else 0
                s_sc = s_scs[h * FLASH_KEY_SPLIT + c]
                p_sc = p_scs[h * FLASH_KEY_SPLIT + c]
                m_loc = m_locs[h * FLASH_KEY_SPLIT + c]
                for r in range(n_chunks):
                    rows = slice(r * FLASH_CHUNK, (r + 1) * FLASH_CHUNK)
                    p_sc[rows, q0:] = jnp.exp2(s_sc[rows, q0:] - m_loc).astype(BF16)
                vt = vt_ref[0, h * dv:(h + 1) * dv, pl.ds(ks, tk)]
                vt_ext = jnp.concatenate([vt, jnp.ones((ONES_ROWS, tk), BF16)], axis=0)
                pvl = _dot(vt_ext, p_sc[:, q0:])
                if q0:
                    m_loc = jnp.concatenate([jnp.full((1, q0), NEG_INF, F32), m_loc], axis=1)
                    pvl = jnp.concatenate([jnp.zeros((dv + ONES_ROWS, q0), F32), pvl], axis=1)
                locs.append((m_loc, pvl))
        for h in range(2):
            mine = locs[h * FLASH_KEY_SPLIT:(h + 1) * FLASH_KEY_SPLIT]
            m_prev = m_sc[h]
            m_new = m_prev
            for m_loc, _ in mine:
                m_new = jnp.maximum(m_new, m_loc)
            a_prev = jnp.exp2(m_prev - m_new)
            acc = acc_scs[h][...] * a_prev
            l = l_sc[h] * a_prev
            for m_loc, pvl in mine:
                a_loc = jnp.exp2(m_loc - m_new)
                acc = acc + pvl[:dv] * a_loc
                l = l + pvl[dv:dv + 1] * a_loc
            acc_scs[h][...] = acc
            l_sc[h] = l
            m_sc[h] = m_new

    per_trip = FLASH_BLOCKS_PER_TRIP

    def body(jj, carry):
        for u in range(per_trip):
            step(per_trip * jj + u, False)
        return carry

    n_trips = i // per_trip
    lax.fori_loop(0, n_trips, body, 0)

    def single(j, carry):
        step(j, False)
        return carry

    lax.fori_loop(n_trips * per_trip, i, single, 0)
    step(i, True)
    for h in range(2):
        o_ref[0, h * dv:(h + 1) * dv, pl.ds(qs, tq)] = (acc_scs[h][...] / l_sc[h]).astype(o_ref.dtype)


def _flash_attention(q, k, vt, n_heads, tq, name):
    B, S, _ = q.shape
    n_pairs = n_heads // 2
    dv2 = 2 * HEAD_LANES
    spec0 = pl.BlockSpec((1, S, LANES), lambda b, p: (b, 0, 2 * p))
    spec1 = pl.BlockSpec((1, S, LANES), lambda b, p: (b, 0, 2 * p + 1))
    vspec = pl.BlockSpec((1, dv2, S), lambda b, p: (b, p, 0))
    return pl.pallas_call(
        functools.partial(_flash_kernel, tq=tq, n_q=S // tq),
        out_shape=jax.ShapeDtypeStruct((B, n_heads * HEAD_LANES, S), BF16),
        grid=(B, n_pairs),
        in_specs=[spec0, spec1, spec0, spec1, vspec],
        out_specs=vspec,
        scratch_shapes=[pltpu.VMEM((tq // FLASH_KEY_SPLIT, tq), F32)] * (2 * FLASH_KEY_SPLIT)
        + [pltpu.VMEM((tq // FLASH_KEY_SPLIT, tq), BF16)] * (2 * FLASH_KEY_SPLIT)
        + [pltpu.VMEM((2, 1, tq), F32), pltpu.VMEM((2, 1, tq), F32),
                        pltpu.VMEM((HEAD_LANES, tq), F32), pltpu.VMEM((HEAD_LANES, tq), F32)],
        compiler_params=_cparams(("parallel", "parallel")),
        name=name,
    )(q, q, k, k, vt)


DIL_PLANES = 16
DIL_TQ = 128
DIL_UNROLL = 8


def _plane_slot(r16):
    return 4 * (r16 % 4) + r16 // 4


def _dilated_geometry(dil):
    planes = DIL_PLANES // dil
    rows = DIL_TQ // planes
    if dil == 1:
        order = [(c // 4) + 4 * (c % 4) for c in range(planes)]
    else:
        order = list(range(planes))
    return planes, rows, order


def _dilated_bias(dil, first):
    planes, rows, order = _dilated_geometry(dil)
    krows = 2 * rows
    a = np.arange(planes * rows)
    b = np.arange(planes * krows)
    uq = np.asarray(order)[a // rows] + planes * (a % rows + (0 if first else rows))
    uk = np.asarray(order)[b // krows] + planes * (b % krows)
    diff = uq[:, None] - uk[None, :]
    valid = (diff >= 0) & (diff <= DIL_TQ)
    return jnp.asarray(np.where(valid, 0.0, NEG_INF), F32)


def _dilated_kernel(q_ref, k_ref, v_ref, b1f_ref, b1r_ref, b4f_ref, b4r_ref, b16f_ref, b16r_ref, o_ref,
                    q_st, k_st, v_st, m_sc, l_sc, acc_sc, *, rows_per_plane):
    pair = pl.program_id(1)
    n_pairs = DIL_HEADS // 2
    lane = lax.broadcasted_iota(jnp.int32, (1, LANES), 1)
    lo = lane < HEAD_LANES
    q_lo = (lane % HEAD_LANES) < (HEAD_LANES // 2)
    tq = DIL_TQ

    def plane_lanes(r16):
        return pl.ds(pl.multiple_of((r16 * n_pairs + pair) * LANES, LANES), LANES)

    for r16 in range(DIL_PLANES):
        slot = _plane_slot(r16)
        q_st[slot] = q_ref[0, :, plane_lanes(r16)].astype(F32)
        k_st[slot] = k_ref[0, :, plane_lanes(r16)].astype(F32)
        v_st[slot] = v_ref[0, :, plane_lanes(r16)].astype(F32)
    m_sc[...] = jnp.full_like(m_sc, NEG_INF)
    l_sc[...] = jnp.zeros_like(l_sc)
    acc_sc[...] = jnp.zeros_like(acc_sc)

    def block(planes, rows, slot0, row0, first, bias_ref):
        qsl = (pl.ds(slot0, planes), pl.ds(row0, rows), slice(None))
        k_row0 = row0 if first else row0 - rows
        ksl = (pl.ds(slot0, planes), pl.ds(k_row0, 2 * rows), slice(None))
        q = q_st[qsl].reshape(tq, LANES)
        k = k_st[ksl].reshape(-1, LANES).astype(BF16)
        v = v_st[ksl].reshape(-1, LANES).astype(BF16)
        zero = jnp.zeros_like(q)
        qs = (jnp.where(q_lo, q, zero).astype(BF16), jnp.where(q_lo, zero, q).astype(BF16))
        bias = bias_ref[...]
        v_ext = jnp.concatenate([v, jnp.ones_like(v)], axis=1)
        maxes, sums, pvs = [], [], []
        for h in range(2):
            s = _nt_dot(qs[h], k) + bias
            mh = jnp.max(s, axis=1, keepdims=True)
            p = jnp.exp2(s - mh)
            maxes.append(mh)
            pvl = _dot(p.astype(BF16), v_ext)
            pvs.append(pvl[:, :LANES])
            sums.append(pvl[:, LANES:])
        m_blk = jnp.where(lo, maxes[0], maxes[1])
        m_old = m_sc[qsl].reshape(tq, LANES)
        m_new = jnp.maximum(m_old, m_blk)
        a_old = jnp.exp2(m_old - m_new)
        a_blk = jnp.exp2(m_blk - m_new)
        l_new = a_old * l_sc[qsl].reshape(tq, LANES) + a_blk * jnp.where(lo, sums[0], sums[1])
        acc_new = a_old * acc_sc[qsl].reshape(tq, LANES) + a_blk * jnp.where(lo, pvs[0], pvs[1])
        m_sc[qsl] = m_new.reshape(planes, rows, LANES)
        l_sc[qsl] = l_new.reshape(planes, rows, LANES)
        acc_sc[qsl] = acc_new.reshape(planes, rows, LANES)

    bias_refs = {1: (b1f_ref, b1r_ref), 4: (b4f_ref, b4r_ref), 16: (b16f_ref, b16r_ref)}
    for window, dil in DILATED_PAIRS:
        assert window // dil == tq
        planes, rows, _ = _dilated_geometry(dil)
        nblk = rows_per_plane // rows
        bf_ref, br_ref = bias_refs[dil]

        def first_body(sub, carry, planes=planes, rows=rows, bf_ref=bf_ref):
            block(planes, rows, sub * planes, 0, True, bf_ref)
            return carry

        lax.fori_loop(0, dil, first_body, 0, unroll=min(dil, DIL_UNROLL))

        if nblk > 1:
            def rest_body(t, carry, planes=planes, rows=rows, nblk=nblk, br_ref=br_ref):
                sub = t // (nblk - 1)
                blk = t % (nblk - 1) + 1
                block(planes, rows, sub * planes, pl.multiple_of(blk * rows, rows), False, br_ref)
                return carry

            lax.fori_loop(0, dil * (nblk - 1), rest_body, 0, unroll=DIL_UNROLL)

    for r16 in range(DIL_PLANES):
        slot = _plane_slot(r16)
        o_ref[0, :, plane_lanes(r16)] = (acc_sc[slot] / l_sc[slot]).astype(o_ref.dtype)


def _dilated_attention(qc, kc, vc):
    B, rpp, width = qc.shape
    n_pairs = DIL_HEADS // 2
    assert rpp % DIL_TQ == 0 and rpp >= 2 * DIL_TQ

    spec = pl.BlockSpec((1, rpp, width), lambda b, p: (b, 0, 0))
    biases = [_dilated_bias(d, f) for _, d in DILATED_PAIRS for f in (True, False)]
    bspecs = [pl.BlockSpec(bb.shape, lambda b, p: (0, 0)) for bb in biases]
    plane = pltpu.VMEM((DIL_PLANES, rpp, LANES), F32)
    return pl.pallas_call(
        functools.partial(_dilated_kernel, rows_per_plane=rpp),
        out_shape=jax.ShapeDtypeStruct((B, rpp, width), BF16),
        grid=(B, n_pairs),
        in_specs=[spec, spec, spec] + bspecs,
        out_specs=spec,
        scratch_shapes=[plane] * 6,
        compiler_params=_cparams(("parallel", "arbitrary")),
        name="dilated",
    )(qc, kc, vc, *biases)


def _tn_dot(a_t, b):
    return lax.dot_general(a_t, b, (((0,), (0,)), ((), ())), preferred_element_type=F32)


def _mix_xattn_kernel(oat_ref, obt_ref, oc_ref, x_ref, wa_ref, wb_ref, wc_ref, g_ref, b_ref,
                      k_ref, v_ref, wq_ref, wo_ref, g2_ref, b2_ref, o_ref, *oc_scs):
    nd = len(oc_scs) * LANES
    rows_per_plane = oc_scs[0].shape[0] // DIL_PLANES
    for r in range(DIL_PLANES):
        rows = pl.ds(r, rows_per_plane, stride=DIL_PLANES)
        for pblk, sc in enumerate(oc_scs):
            lanes = slice(r * nd + pblk * LANES, r * nd + (pblk + 1) * LANES)
            sc[rows, :] = oc_ref[0, :, lanes].astype(F32)
    oc = jnp.concatenate([sc[...] for sc in oc_scs], axis=1).astype(BF16)
    y = (_tn_dot(oat_ref[0], wa_ref[...]) + _tn_dot(obt_ref[0], wb_ref[...])
         + _dot(oc, wc_ref[...]))
    x1 = _layer_norm(DEEPNORM_ALPHA * x_ref[0] + y, g_ref[...], b_ref[...])
    o_ref[0] = _xattn_rows(x1, k_ref, v_ref, wq_ref, wo_ref, g2_ref, b2_ref)


def _mix_xattn(oa_t, ob_t, oc, x, wa, wb, wc, g, b, k_mem, v_mem, w_q, w_o, g2, b2, l, tm):
    B, S, D = x.shape
    M = k_mem.shape[1]
    mspec = pl.BlockSpec((1, M, D), lambda bb, i: (bb, 0, 0))

    def cols(a):
        return pl.BlockSpec((1, a.shape[1], tm), lambda bb, i: (bb, 0, i))

    def rows(a):
        return pl.BlockSpec((1, tm, a.shape[2]), lambda bb, i: (bb, i, 0))

    return pl.pallas_call(
        _mix_xattn_kernel,
        out_shape=jax.ShapeDtypeStruct((B, S, D), F32),
        grid=(B, S // tm),
        in_specs=[cols(oa_t), cols(ob_t),
                  pl.BlockSpec((1, tm // DIL_PLANES, oc.shape[2]), lambda bb, i: (bb, i, 0)),
                  rows(x)] + [_layer_slab(a, l) for a in (wa, wb, wc, g, b)]
        + [mspec, mspec] + [_layer_slab(a, l) for a in (w_q, w_o, g2, b2)],
        out_specs=rows(x),
        scratch_shapes=[pltpu.VMEM((tm, LANES), F32)] * (oc.shape[2] // DIL_PLANES // LANES),
        compiler_params=_cparams(("parallel", "parallel")),
        name="mix_xattn",
    )(oa_t, ob_t, oc, x, wa, wb, wc, g, b, k_mem, v_mem, w_q, w_o, g2, b2)


def _mem_kv_kernel(mem_ref, w_ref, k_ref, v_ref):
    kv = _dot(mem_ref[0].astype(BF16), w_ref[...])
    d = k_ref.shape[-1]
    k_ref[0] = kv[:, :d].astype(BF16)
    v_ref[0] = kv[:, d:].astype(BF16)


def _mem_kv(mem, w_kv, l):
    B, M, D = mem.shape
    spec = pl.BlockSpec((1, M, D), lambda b: (b, 0, 0))
    return pl.pallas_call(
        _mem_kv_kernel,
        out_shape=(jax.ShapeDtypeStruct((B, M, D), BF16),) * 2,
        grid=(B,),
        in_specs=[spec, _layer_slab(w_kv, l)],
        out_specs=(spec, spec),
        compiler_params=_cparams(("parallel",)),
        name="mem_kv",
    )(mem, w_kv)


def _xattn_rows(x, k_ref, v_ref, wq_ref, wo_ref, g_ref, b_ref):
    d = x.shape[-1]
    dh = d // XATTN_HEADS
    q = (_dot(x.astype(BF16), wq_ref[...]) * (dh ** -0.5 * LOG2E)).astype(BF16)
    outs = []
    for h in range(XATTN_HEADS):
        sl = slice(h * dh, (h + 1) * dh)
        s = _nt_dot(q[:, sl], k_ref[0, :, sl])
        p = jnp.exp2(s - jnp.max(s, axis=1, keepdims=True))
        o = _dot(p.astype(BF16), v_ref[0, :, sl])
        outs.append((o / jnp.sum(p, axis=1, keepdims=True)).astype(BF16))
    o_all = jnp.concatenate(outs, axis=1)
    y = _dot(o_all, wo_ref[...])
    return _layer_norm(DEEPNORM_ALPHA * x + y, g_ref[...], b_ref[...])


def _split_bf16(a):
    hi = a.astype(BF16)
    return hi, (a - hi.astype(F32)).astype(BF16)


def _router_gates_t(x, rw_t, rbias):
    x_hi, x_lo = _split_bf16(x)
    w_hi, w_lo = _split_bf16(rw_t)
    t1 = _nt_dot(jnp.concatenate([w_hi, w_lo], axis=0), x_hi)
    logits_t = t1[:N_EXPERTS] + (t1[N_EXPERTS:] + _nt_dot(w_hi, x_lo))
    scores = 1.0 / (1.0 + jnp.exp(-logits_t))
    biased = scores + rbias
    sc = [scores[e:e + 1, :] for e in range(N_EXPERTS)]
    bs = [biased[e:e + 1, :] for e in range(N_EXPERTS)]
    epg = EXPERTS_PER_GROUP

    def beats(a, b, a_first):
        return (a >= b) if a_first else (a > b)

    grp = []
    for gi in range(N_GROUPS):
        v = bs[gi * epg:(gi + 1) * epg]
        best = None
        for a in range(epg):
            for b in range(a + 1, epg):
                pair = v[a] + v[b]
                best = pair if best is None else jnp.maximum(best, pair)
        grp.append(best)
    gates, g_sels = [], []
    for gi in range(N_GROUPS):
        g_sel = None
        for gj in range(N_GROUPS):
            if gj == gi:
                continue
            w = beats(grp[gi], grp[gj], gi < gj)
            g_sel = w if g_sel is None else (g_sel & w)
        g_sels.append(g_sel)
        in_top = []
        for a in range(epg):
            ea = gi * epg + a
            n_above = jnp.zeros(bs[ea].shape, jnp.int32)
            for b in range(epg):
                if b == a:
                    continue
                eb = gi * epg + b
                n_above = n_above + beats(bs[eb], bs[ea], b < a).astype(jnp.int32)
            in_top.append(g_sel & (n_above < 2))
        denom = None
        for a in range(epg):
            term = jnp.where(in_top[a], sc[gi * epg + a], 0.0)
            denom = term if denom is None else denom + term
        denom = jnp.where(g_sel, denom, 1.0)
        for a in range(epg):
            gates.append(jnp.where(in_top[a], sc[gi * epg + a] / denom, 0.0))
    return gates, g_sels


def _moe_kernel(x_ref, rwt_ref, rb_ref, wg_ref, wu_ref, wd_ref, g_ref, b_ref, o_ref,
                gt_sc, sel_sc, rank_sc, acc_sc, xb_sc, *, tm):
    grp = pl.program_id(1)
    epg = EXPERTS_PER_GROUP

    @pl.when(grp == 0)
    def _():
        ts = min(tm, ROUTER_ROWS)
        before = (lax.broadcasted_iota(jnp.int32, (ts, ts), 0)
                  < lax.broadcasted_iota(jnp.int32, (ts, ts), 1)).astype(BF16)
        seen = jnp.zeros((8, 1), F32)
        for r in range(tm // ts):
            cols = slice(r * ts, (r + 1) * ts)
            x = x_ref[cols, :]
            gates, g_sels = _router_gates_t(x, rwt_ref[...], rb_ref[...])
            zeros4 = [jnp.zeros((1, ts), F32)] * (8 - epg)
            for gi in range(N_GROUPS):
                gt_sc[gi, :, cols] = jnp.concatenate(gates[gi * epg:(gi + 1) * epg] + zeros4, axis=0)
            sel8 = jnp.concatenate([s.astype(F32) for s in g_sels] + [jnp.zeros((1, ts), F32)] * (8 - N_GROUPS),
                                   axis=0)
            sel_sc[:, cols] = sel8
            rank_sc[:, cols] = _dot(sel8.astype(BF16), before) + seen
            seen = seen + jnp.sum(sel8, axis=1, keepdims=True)
            xb_sc[cols, :] = x.astype(BF16)
        acc_sc[...] = jnp.zeros_like(acc_sc)

    sel = sel_sc[pl.ds(grp, 1), :] > 0.5
    rank = rank_sc[pl.ds(grp, 1), :].astype(jnp.int32)
    n_tok = jnp.sum(sel_sc[pl.ds(grp, 1), :]).astype(jnp.int32)
    gt = gt_sc[grp]
    g_hi = gt.astype(BF16).astype(F32)
    g_r = gt - g_hi
    g_mid = g_r.astype(BF16).astype(F32)
    g_lo = (g_r - g_mid).astype(BF16).astype(F32)
    g3 = jnp.concatenate([g_hi, g_mid, g_lo, jnp.zeros_like(g_hi)], axis=0).astype(BF16)
    wd_all = wd_ref[...].reshape(epg * D_EXPERT, -1)

    def chunk(first_slot, ch):
        slot = lax.broadcasted_iota(jnp.int32, (ch, tm), 0) + first_slot
        onehot = jnp.where((slot == rank) & sel, 1.0, 0.0).astype(BF16)
        xg = _dot(onehot, xb_sc[...]).astype(BF16)
        gs3 = _nt_dot(onehot, g3)
        gs = gs3[:, 0:8] + (gs3[:, 8:16] + gs3[:, 16:24])
        acts = []
        for e in range(epg):
            hg = _dot(xg, wg_ref[e])
            hu = _dot(xg, wu_ref[e])
            acts.append(((hg / (1.0 + jnp.exp(-hg))) * hu * gs[:, e:e + 1]).astype(BF16))
        y = _dot(jnp.concatenate(acts, axis=1), wd_all).astype(BF16)
        acc_sc[...] += _tn_dot(onehot, y)

    n_wide = (n_tok + (MOE_CHUNK - MOE_TAIL_CHUNK - 1)) // MOE_CHUNK

    def wide(c, carry):
        chunk(c * MOE_CHUNK, MOE_CHUNK)
        return carry

    lax.fori_loop(0, n_wide, wide, 0)

    @pl.when(n_tok > n_wide * MOE_CHUNK)
    def _():
        chunk(n_wide * MOE_CHUNK, MOE_TAIL_CHUNK)

    @pl.when(grp == N_GROUPS - 1)
    def _():
        o_ref[...] = _layer_norm(DEEPNORM_ALPHA * x_ref[...] + acc_sc[...], g_ref[...], b_ref[...])


ROUTER_ROWS = 512
MOE_CHUNK = 256
MOE_TAIL_CHUNK = 128


def _moe(x2d, rw_t, rbias, wg, wu, wd, g, b, l, tm):
    T, D = x2d.shape
    xspec = pl.BlockSpec((tm, D), lambda i, e: (i, 0))
    epg = EXPERTS_PER_GROUP

    def full(a):
        return pl.BlockSpec(a.shape, lambda i, e: (0, 0))

    return pl.pallas_call(
        functools.partial(_moe_kernel, tm=tm),
        out_shape=jax.ShapeDtypeStruct((T, D), F32),
        grid=(T // tm, N_GROUPS),
        in_specs=[xspec, full(rw_t), full(rbias),
                  pl.BlockSpec((None, epg, D, D_EXPERT), lambda i, e: (l, e, 0, 0)),
                  pl.BlockSpec((None, epg, D, D_EXPERT), lambda i, e: (l, e, 0, 0)),
                  pl.BlockSpec((None, epg, D_EXPERT, D), lambda i, e: (l, e, 0, 0)),
                  _layer_slab(g, l), _layer_slab(b, l)],
        out_specs=xspec,
        scratch_shapes=[pltpu.VMEM((N_GROUPS, 8, tm), F32), pltpu.VMEM((8, tm), F32), pltpu.VMEM((8, tm), F32),
                        pltpu.VMEM((tm, D), F32), pltpu.VMEM((tm, D), BF16)],
        compiler_params=_cparams(("parallel", "arbitrary")),
        name="moe",
    )(x2d, rw_t, rbias, wg, wu, wd, g, b)


def _rope_constants():
    half_d = DIL_HEAD_DIM // 2
    inv_d = ROPE_THETA ** (-jnp.arange(half_d, dtype=F32) / half_d)
    half_a = MLA_ROPE_DIM // 2
    inv_a = ROPE_THETA ** (-jnp.arange(half_a, dtype=F32) / half_a)
    lane = np.arange(LANES)
    in_rope = (lane >= MLA_NOPE_DIM) & (lane < MLA_NOPE_DIM + MLA_ROPE_DIM)
    invf = jnp.where(jnp.asarray(lane < half_d), inv_d[lane % half_d],
                     jnp.where(jnp.asarray(in_rope), inv_a[(lane - MLA_NOPE_DIM) % half_a], 0.0))
    return invf.reshape(1, LANES)


def _layer_weights(w_in, b_forget, q_gain, kv_gain, w_uq, w_ukv):
    D = w_in.shape[0]
    cuts = np.cumsum([MLA_Q_RANK, MLA_KV_RANK, MLA_ROPE_DIM, 3 * FOX_HEADS * FOX_HEAD_DIM, FOX_HEADS])
    w_cq, w_ckv, w_kr, w_fox, w_ff, w_dil = jnp.split(w_in, cuts.tolist(), axis=1)
    half = MLA_ROPE_DIM // 2
    w_kr_rot = jnp.concatenate([-w_kr[:, half:], w_kr[:, :half]], axis=1)

    def place_rope(w):
        return jnp.pad(w, ((0, 0), (MLA_NOPE_DIM, LANES - MLA_NOPE_DIM - MLA_ROPE_DIM)))

    def place_gate(w):
        rep = jnp.repeat(w, 3, axis=1)
        return jnp.pad(rep, ((0, 0), (FOX_HEAD_DIM, LANES - FOX_HEAD_DIM - 3 * FOX_HEADS)))

    wa = jnp.concatenate([w_cq, w_ckv, place_rope(w_kr), place_rope(w_kr_rot),
                          place_gate(w_ff)], axis=1)

    nd = DIL_HEADS * DIL_HEAD_DIM

    def pair_layout(w):
        half = DIL_HEAD_DIM // 2
        w5 = w.reshape(D, DIL_HEADS // 2, 2, 2, half)
        return w5.transpose(0, 1, 3, 2, 4).reshape(D, nd)

    wdil = jnp.concatenate([pair_layout(w_dil[:, :nd]), pair_layout(w_dil[:, nd:2 * nd]), w_dil[:, 2 * nd:]],
                           axis=1)

    dq = MLA_NOPE_DIM + MLA_ROPE_DIM
    uq = w_uq.reshape(MLA_Q_RANK, MLA_HEADS, dq)
    uq_rope = uq[:, :, MLA_NOPE_DIM:]
    uq_rot = jnp.concatenate([-uq_rope[:, :, half:], uq_rope[:, :, :half]], axis=2)
    pad_tail = LANES - dq
    uq_plain = jnp.pad(uq, ((0, 0), (0, 0), (0, pad_tail))).reshape(MLA_Q_RANK, MLA_HEADS * LANES)
    uq_rotp = jnp.pad(uq_rot, ((0, 0), (0, 0), (MLA_NOPE_DIM, pad_tail))).reshape(MLA_Q_RANK, MLA_HEADS * LANES)
    wuq = jnp.concatenate([uq_plain, uq_rotp], axis=1)

    ukv = w_ukv.reshape(MLA_KV_RANK, MLA_HEADS, MLA_NOPE_DIM + MLA_V_DIM)
    wuk = jnp.pad(ukv[:, :, :MLA_NOPE_DIM], ((0, 0), (0, 0), (0, LANES - MLA_NOPE_DIM)))
    wuk = wuk.reshape(MLA_KV_RANK, MLA_HEADS * LANES)
    wuv = ukv[:, :, MLA_NOPE_DIM:].reshape(MLA_KV_RANK, MLA_HEADS * MLA_V_DIM)

    bf = place_gate(b_forget.astype(F32).reshape(1, -1))
    nf = FOX_HEADS * FOX_HEAD_DIM
    return dict(wa=wa.astype(BF16), wfox=w_fox[:, :2 * nf].astype(BF16),
                wfoxv_t=w_fox[:, 2 * nf:].T.astype(BF16), wdil=wdil.astype(BF16),
                qg=q_gain.reshape(1, -1).astype(F32), kvg=kv_gain.reshape(1, -1).astype(F32),
                wuq=wuq.astype(BF16), wuk=wuk.astype(BF16), wuv_t=wuv.T.astype(BF16), bf=bf)


def _tiles(S):
    tm = min(512, S)
    return dict(rope=min(512, S), proj=tm, flash=min(512, S), mix=tm, moe=min(1024, S))


def kernel(x, mem, positions, w_in, b_forget, mla_q_gain, mla_kv_gain, mla_w_uq, mla_w_ukv, w_mix_out, ln_mix_g, ln_mix_b, xattn_w_q, xattn_w_kv, xattn_w_o, ln_mem_g, ln_mem_b, router_w, router_bias, expert_w_gate, expert_w_up, expert_w_down, ln_ffn_g, ln_ffn_b):
    B, S, D = x.shape
    depth = w_in.shape[0]
    t = _tiles(S)
    tabs = _rope_tables(positions, _rope_constants(), t["rope"])
    rw_t = router_w.T.astype(F32)
    rbias = router_bias.reshape(-1, 1).astype(F32)

    def rows(v):
        return v.reshape(depth, 1, -1).astype(F32)

    wts = jax.vmap(_layer_weights)(w_in, b_forget, mla_q_gain, mla_kv_gain, mla_w_uq, mla_w_ukv)
    na, nb = MLA_HEADS * MLA_V_DIM, FOX_HEADS * FOX_HEAD_DIM
    w_mix = w_mix_out.astype(BF16)
    w_mix_a, w_mix_b, w_mix_c = w_mix[:, :na], w_mix[:, na:na + nb], w_mix[:, na + nb:]
    w_q, w_kv, w_o = xattn_w_q.astype(BF16), xattn_w_kv.astype(BF16), xattn_w_o.astype(BF16)
    wg, wu, wd = expert_w_gate.astype(BF16), expert_w_up.astype(BF16), expert_w_down.astype(BF16)
    g_mix, b_mix, g_mem, b_mem = rows(ln_mix_g), rows(ln_mix_b), rows(ln_mem_g), rows(ln_mem_b)
    g_ffn, b_ffn = rows(ln_ffn_g), rows(ln_ffn_b)

    for l in range(depth):
        qa, ka, va_t, qb, kb, vb_t, qc, kc, vc = _proj(x, wts, l, tabs, t["proj"])
        oa_t = _flash_attention(qa, ka, va_t, MLA_HEADS, t["flash"], "mla_flash")
        ob_t = _flash_attention(qb, kb, vb_t, FOX_HEADS, t["flash"], "fox_flash")
        oc = _dilated_attention(qc, kc, vc)
        k_mem, v_mem = _mem_kv(mem, w_kv, l)
        x = _mix_xattn(oa_t, ob_t, oc, x, w_mix_a, w_mix_b, w_mix_c, g_mix, b_mix,
                       k_mem, v_mem, w_q, w_o, g_mem, b_mem, l, t["mix"])
        x2d = _moe(x.reshape(B * S, D), rw_t, rbias, wg, wu, wd, g_ffn, b_ffn, l, t["moe"])
        x = x2d.reshape(B, S, D)
    return x
```

```python
import functools
import math

import numpy as np
import jax
import jax.numpy as jnp
from jax import lax
from jax.experimental import pallas as pl
from jax.experimental.pallas import tpu as pltpu

F32 = jnp.float32
BF16 = jnp.bfloat16

DEPTH = 4
MLA_HEADS = 4
MLA_Q_RANK = 256
MLA_KV_RANK = 128
MLA_NOPE_DIM = 64
MLA_ROPE_DIM = 32
MLA_V_DIM = 64
FOX_HEADS = 6
FOX_HEAD_DIM = 64
DIL_HEADS = 6
DIL_HEAD_DIM = 64
DILATED_PAIRS = ((128, 1), (512, 4), (2048, 16))
XATTN_HEADS = 4
N_EXPERTS = 16
N_GROUPS = 4
EXPERTS_PER_GROUP = N_EXPERTS // N_GROUPS
D_EXPERT = 256
ROPE_THETA = 10000.0
NORM_EPS = 1e-5
NEG_INF = -1e30
DEEPNORM_ALPHA = (2 * DEPTH) ** 0.25
LOG2E = math.log2(math.e)

LANES = 128
HEAD_LANES = 64
VMEM_LIMIT_BYTES = 52 * 1024 * 1024

A_CQ = 0
A_CKV = A_CQ + MLA_Q_RANK
A_KR = A_CKV + MLA_KV_RANK
A_KRR = A_KR + LANES
A_FF = A_KRR + LANES


def _cparams(sem):
    return pltpu.CompilerParams(dimension_semantics=sem, vmem_limit_bytes=VMEM_LIMIT_BYTES)


def _layer_slab(stacked, l):
    zeros = (0,) * (stacked.ndim - 1)
    return pl.BlockSpec((None,) + stacked.shape[1:], lambda *_: (l,) + zeros)


def _nt_dot(a, b):
    return lax.dot_general(a, b, (((1,), (1,)), ((), ())), preferred_element_type=F32)


def _dot(a, b):
    return jnp.dot(a, b, preferred_element_type=F32)


def _layer_norm(y, g, b):
    mu = jnp.mean(y, axis=-1, keepdims=True)
    yc = y - mu
    var = jnp.mean(yc * yc, axis=-1, keepdims=True)
    return yc * lax.rsqrt(var + NORM_EPS) * g + b


def _rms_norm(y, g):
    ms = jnp.mean(y * y, axis=-1, keepdims=True)
    return y * lax.rsqrt(ms + NORM_EPS) * g


def _rope_table_kernel(pos_ref, invf_ref, cosa_ref, sina_ref, cosd_ref, sind_ref):
    lane = lax.broadcasted_iota(jnp.int32, (1, LANES), 1)
    ang = pos_ref[0].astype(F32) * invf_ref[...]
    cs = jnp.cos(ang)
    sn = jnp.sin(ang)
    in_a = (lane >= MLA_NOPE_DIM) & (lane < MLA_NOPE_DIM + MLA_ROPE_DIM)
    cosa_ref[0] = jnp.where(in_a, cs, 1.0)
    sina_ref[0] = jnp.where(in_a, sn, 0.0)
    half = DIL_HEAD_DIM // 2
    c32 = jnp.where(lane < half, cs, 0.0)
    s32 = jnp.where(lane < half, sn, 0.0)
    cd, sd = c32, s32
    for shift in (half, 2 * half, 3 * half):
        cd = cd + pltpu.roll(c32, shift, 1)
        sd = sd + pltpu.roll(s32, shift, 1)
    cosd_ref[0] = cd
    sind_ref[0] = jnp.where(lane < HEAD_LANES, -sd, sd)


def _rope_tables(positions, invf, ts):
    B, S = positions.shape
    pos3 = positions.reshape(B, S, 1)
    spec = pl.BlockSpec((1, ts, LANES), lambda b, i: (b, i, 0))
    vec = pl.BlockSpec((1, LANES), lambda b, i: (0, 0))
    return pl.pallas_call(
        _rope_table_kernel,
        out_shape=(jax.ShapeDtypeStruct((B, S, LANES), F32),) * 4,
        grid=(B, S // ts),
        in_specs=[pl.BlockSpec((1, ts, 1), lambda b, i: (b, i, 0)), vec],
        out_specs=(spec,) * 4,
        compiler_params=_cparams(("parallel", "parallel")),
        name="rope_tables",
    )(pos3, invf)


def _proj_kernel(x_ref, wa_ref, wfox_ref, wfoxv_ref, wdil_ref, qg_ref, kvg_ref, wuq_ref, wuk_ref,
                 wuv_ref, bf_ref, cosa_ref, sina_ref, cosd_ref, sind_ref,
                 qa_ref, ka_ref, va_ref, qb_ref, kb_ref, vb_ref,
                 qc_ref, kc_ref, vc_ref, carry_c, *dil_scs, tm):
    i = pl.program_id(1)
    xb = x_ref[0].astype(BF16)

    ha = _dot(xb, wa_ref[...])
    cqn = _rms_norm(ha[:, A_CQ:A_CQ + MLA_Q_RANK], qg_ref[...]).astype(BF16)
    ckvn = _rms_norm(ha[:, A_CKV:A_CKV + MLA_KV_RANK], kvg_ref[...]).astype(BF16)
    cos_a = cosa_ref[0]
    sin_a = sina_ref[0]
    q2 = _dot(cqn, wuq_ref[...])
    k_nope = _dot(ckvn, wuk_ref[...])
    k_rot = ha[:, A_KR:A_KR + LANES] * cos_a + ha[:, A_KRR:A_KRR + LANES] * sin_a
    q_scale = (MLA_NOPE_DIM + MLA_ROPE_DIM) ** -0.5 * LOG2E
    n_half = MLA_HEADS * LANES
    for h in range(MLA_HEADS):
        sl = slice(h * LANES, (h + 1) * LANES)
        qh = q2[:, sl] * cos_a + q2[:, n_half + h * LANES:n_half + (h + 1) * LANES] * sin_a
        qa_ref[0, :, sl] = (qh * q_scale).astype(BF16)
        ka_ref[0, :, sl] = (k_nope[:, sl] + k_rot).astype(BF16)
    va_ref[0] = _nt_dot(wuv_ref[...], ckvn).astype(BF16)

    lane = lax.broadcasted_iota(jnp.int32, (1, LANES), 1)
    bias_lanes = (lane >= FOX_HEAD_DIM) & (lane < FOX_HEAD_DIM + 3 * FOX_HEADS)
    z = ha[:, A_FF:A_FF + LANES] + bf_ref[...]
    logf = jnp.minimum(z, 0.0) - jnp.log(1.0 + jnp.exp(-jnp.abs(z)))
    logf = jnp.where(bias_lanes, logf, 0.0)

    @pl.when(i == 0)
    def _():
        carry_c[...] = jnp.zeros_like(carry_c)

    p0 = logf.astype(BF16).astype(F32)
    r0 = logf - p0
    p1 = r0.astype(BF16).astype(F32)
    p2 = (r0 - p1).astype(BF16)
    packed = jnp.concatenate([(p0 + pltpu.roll(p1, HEAD_LANES, 1)).astype(BF16), p2], axis=1)
    r = lax.broadcasted_iota(jnp.int32, (tm, tm), 0)
    c = lax.broadcasted_iota(jnp.int32, (tm, tm), 1)
    lower = (c <= r).astype(BF16)
    cum = _dot(lower, packed)
    cum_a = cum[:, :LANES]
    fcol = jnp.where(bias_lanes, cum_a + pltpu.roll(cum_a, HEAD_LANES, 1) + cum[:, LANES:], 0.0) + carry_c[...]
    carry_c[...] = fcol[tm - 1:tm, :]

    nb = fcol * (-LOG2E)
    hi = nb.astype(BF16).astype(F32)
    r1 = nb - hi
    mid = r1.astype(BF16).astype(F32)
    lo3 = (r1 - mid).astype(BF16).astype(F32)
    piece = (lane - FOX_HEAD_DIM) % 3
    k_bias = jnp.where(piece == 0, hi, jnp.where(piece == 1, mid, lo3))

    hf = _dot(xb, wfox_ref[...])
    nf = FOX_HEADS * FOX_HEAD_DIM
    dims = lane < FOX_HEAD_DIM
    fq_scale = FOX_HEAD_DIM ** -0.5 * LOG2E
    for h in range(FOX_HEADS):
        blk = slice((h // 2) * LANES, (h // 2 + 1) * LANES)
        out = slice(h * LANES, (h + 1) * LANES)
        qh = hf[:, blk]
        kh = hf[:, nf + (h // 2) * LANES:nf + (h // 2 + 1) * LANES]
        if h % 2:
            qh = pltpu.roll(qh, HEAD_LANES, 1)
            kh = pltpu.roll(kh, HEAD_LANES, 1)
        own = (lane >= FOX_HEAD_DIM + 3 * h) & (lane < FOX_HEAD_DIM + 3 * h + 3)
        qb_ref[0, :, out] = jnp.where(dims, qh * fq_scale, jnp.where(own, 1.0, 0.0)).astype(BF16)
        kb_ref[0, :, out] = jnp.where(dims, kh, k_bias).astype(BF16)
    vb_ref[0] = _nt_dot(wfoxv_ref[...], xb).astype(BF16)

    hd = _dot(xb, wdil_ref[...])
    cos_d = cosd_ref[0]
    sin_d = sind_ref[0]
    nd = DIL_HEADS * DIL_HEAD_DIM
    for pblk in range(nd // LANES):
        sl = slice(pblk * LANES, (pblk + 1) * LANES)
        qh = hd[:, sl]
        kh = hd[:, nd + pblk * LANES:nd + (pblk + 1) * LANES]
        qh = qh * cos_d + pltpu.roll(qh, HEAD_LANES, 1) * sin_d
        kh = kh * cos_d + pltpu.roll(kh, HEAD_LANES, 1) * sin_d
        dil_scs[pblk][...] = qh * (DIL_HEAD_DIM ** -0.5 * LOG2E)
        dil_scs[3 + pblk][...] = kh
        dil_scs[6 + pblk][...] = hd[:, 2 * nd + pblk * LANES:2 * nd + (pblk + 1) * LANES]
    for a, ref in enumerate((qc_ref, kc_ref, vc_ref)):
        for r in range(DIL_PLANES):
            rows = pl.ds(r, tm // DIL_PLANES, stride=DIL_PLANES)
            for pblk in range(nd // LANES):
                lanes = slice(r * nd + pblk * LANES, r * nd + (pblk + 1) * LANES)
                ref[0, :, lanes] = dil_scs[3 * a + pblk][rows, :].astype(BF16)


def _proj(x, wts, l, tabs, tm):
    B, S, D = x.shape
    cos_a, sin_a, cos_d, sin_d = tabs
    grid = (B, S // tm)

    def tok(width, dtype):
        return (jax.ShapeDtypeStruct((B, S, width), dtype),
                pl.BlockSpec((1, tm, width), lambda b, i: (b, i, 0)))

    tab = pl.BlockSpec((1, tm, LANES), lambda b, i: (b, i, 0))
    nd = DIL_HEADS * DIL_HEAD_DIM

    def planes(dtype):
        return (jax.ShapeDtypeStruct((B, S // DIL_PLANES, DIL_PLANES * nd), dtype),
                pl.BlockSpec((1, tm // DIL_PLANES, DIL_PLANES * nd), lambda b, i: (b, i, 0)))

    def tok_t(height, dtype):
        return (jax.ShapeDtypeStruct((B, height, S), dtype),
                pl.BlockSpec((1, height, tm), lambda b, i: (b, 0, i)))

    outs = [tok(MLA_HEADS * LANES, BF16), tok(MLA_HEADS * LANES, BF16),
            tok_t(MLA_HEADS * MLA_V_DIM, BF16),
            tok(FOX_HEADS * LANES, BF16), tok(FOX_HEADS * LANES, BF16),
            tok_t(FOX_HEADS * FOX_HEAD_DIM, BF16),
            planes(BF16), planes(BF16), planes(BF16)]
    w_list = [wts["wa"], wts["wfox"], wts["wfoxv_t"], wts["wdil"], wts["qg"], wts["kvg"], wts["wuq"],
              wts["wuk"], wts["wuv_t"], wts["bf"]]
    return pl.pallas_call(
        functools.partial(_proj_kernel, tm=tm),
        out_shape=tuple(o[0] for o in outs),
        grid=grid,
        in_specs=[pl.BlockSpec((1, tm, D), lambda b, i: (b, i, 0))] + [_layer_slab(w, l) for w in w_list]
        + [tab, tab, tab, tab],
        out_specs=tuple(o[1] for o in outs),
        scratch_shapes=[pltpu.VMEM((1, LANES), F32)] + [pltpu.VMEM((tm, LANES), F32)] * (3 * nd // LANES),
        compiler_params=_cparams(("parallel", "arbitrary")),
        name="proj",
    )(x, *w_list, cos_a, sin_a, cos_d, sin_d)


FLASH_CHUNK = 32


FLASH_BLOCKS_PER_TRIP = 2
FLASH_KEY_SPLIT = 4
ONES_ROWS = 16


def _flash_kernel(q0_ref, q1_ref, k0_ref, k1_ref, vt_ref, o_ref, *scratch, tq, n_q):
    def q_tile(i, carry):
        _flash_q_tile(i, q0_ref, q1_ref, k0_ref, k1_ref, vt_ref, o_ref, scratch, tq=tq)
        return carry

    lax.fori_loop(0, n_q, q_tile, 0)


def _flash_q_tile(i, q0_ref, q1_ref, k0_ref, k1_ref, vt_ref, o_ref, scratch, *, tq):
    qs = pl.multiple_of(i * tq, tq)
    q_refs = (q0_ref, q1_ref)
    k_refs = (k0_ref, k1_ref)
    n_chain = 2 * FLASH_KEY_SPLIT
    s_scs = scratch[:n_chain]
    p_scs = scratch[n_chain:2 * n_chain]
    m_sc, l_sc, acc0_sc, acc1_sc = scratch[2 * n_chain:]
    acc_scs = (acc0_sc, acc1_sc)
    dv = HEAD_LANES
    tk = tq // FLASH_KEY_SPLIT
    n_chunks = tk // FLASH_CHUNK

    m_sc[...] = jnp.full_like(m_sc, NEG_INF)
    l_sc[...] = jnp.zeros_like(l_sc)
    acc0_sc[...] = jnp.zeros_like(acc0_sc)
    acc1_sc[...] = jnp.zeros_like(acc1_sc)

    def step(j, masked):
        m_locs = []
        for h in range(2):
            for c in range(FLASH_KEY_SPLIT):
                ks = pl.multiple_of(j * tq + c * tk, tk)
                q0 = c * tk if masked else 0
                k = k_refs[h][0, pl.ds(ks, tk), :]
                q = q_refs[h][0, pl.ds(qs + q0, tq - q0), :]
                st = _nt_dot(k, q)
                if masked:
                    key = lax.broadcasted_iota(jnp.int32, (tk, tq - q0), 0)
                    qry = lax.broadcasted_iota(jnp.int32, (tk, tq - q0), 1)
                    st = jnp.where(key <= qry, st, NEG_INF)
                s_scs[h * FLASH_KEY_SPLIT + c][:, q0:] = st
                m8 = jnp.max(st.reshape(tk // 8, 8, tq - q0), axis=0)
                m_locs.append(jnp.max(m8, axis=0, keepdims=True))
        locs = []
        for h in range(2):
            for c in range(FLASH_KEY_SPLIT):
                ks = pl.multiple_of(j * tq + c * tk, tk)
                q0 = c * tk if masked else 0
                s_sc = s_scs[h * FLASH_KEY_SPLIT + c]
                p_sc = p_scs[h * FLASH_KEY_SPLIT + c]
                m_loc = m_locs[h * FLASH_KEY_SPLIT + c]
                for r in range(n_chunks):
                    rows = slice(r * FLASH_CHUNK, (r + 1) * FLASH_CHUNK)
                    p_sc[rows, q0:] = jnp.exp2(s_sc[rows, q0:] - m_loc).astype(BF16)
                vt = vt_ref[0, h * dv:(h + 1) * dv, pl.ds(ks, tk)]
                vt_ext = jnp.concatenate([vt, jnp.ones((ONES_ROWS, tk), BF16)], axis=0)
                pvl = _dot(vt_ext, p_sc[:, q0:])
                if q0:
                    m_loc = jnp.concatenate([jnp.full((1, q0), NEG_INF, F32), m_loc], axis=1)
                    pvl = jnp.concatenate([jnp.zeros((dv + ONES_ROWS, q0), F32), pvl], axis=1)
                locs.append((m_loc, pvl))
        for h in range(2):
            mine = locs[h * FLASH_KEY_SPLIT:(h + 1) * FLASH_KEY_SPLIT]
            m_prev = m_sc[h]
            m_new = m_prev
            for m_loc, _ in mine:
                m_new = jnp.maximum(m_new, m_loc)
            a_prev = jnp.exp2(m_prev - m_new)
            acc = acc_scs[h][...] * a_prev
            l = l_sc[h] * a_prev
            for m_loc, pvl in mine:
                a_loc = jnp.exp2(m_loc - m_new)
                acc = acc + pvl[:dv] * a_loc
                l = l + pvl[dv:dv + 1] * a_loc
            acc_scs[h][...] = acc
            l_sc[h] = l
            m_sc[h] = m_new

    per_trip = FLASH_BLOCKS_PER_TRIP

    def body(jj, carry):
        for u in range(per_trip):
            step(per_trip * jj + u, False)
        return carry

    n_trips = i // per_trip
    lax.fori_loop(0, n_trips, body, 0)

    def single(j, carry):
        step(j, False)
        return carry

    lax.fori_loop(n_trips * per_trip, i, single, 0)
    step(i, True)
    for h in range(2):
        o_ref[0, h * dv:(h + 1) * dv, pl.ds(qs, tq)] = (acc_scs[h][...] / l_sc[h]).astype(o_ref.dtype)


def _flash_attention(q, k, vt, n_heads, tq, name):
    B, S, _ = q.shape
    n_pairs = n_heads // 2
    dv2 = 2 * HEAD_LANES
    spec0 = pl.BlockSpec((1, S, LANES), lambda b, p: (b, 0, 2 * p))
    spec1 = pl.BlockSpec((1, S, LANES), lambda b, p: (b, 0, 2 * p + 1))
    vspec = pl.BlockSpec((1, dv2, S), lambda b, p: (b, p, 0))
    return pl.pallas_call(
        functools.partial(_flash_kernel, tq=tq, n_q=S // tq),
        out_shape=jax.ShapeDtypeStruct((B, n_heads * HEAD_LANES, S), BF16),
        grid=(B, n_pairs),
        in_specs=[spec0, spec1, spec0, spec1, vspec],
        out_specs=vspec,
        scratch_shapes=[pltpu.VMEM((tq // FLASH_KEY_SPLIT, tq), F32)] * (2 * FLASH_KEY_SPLIT)
        + [pltpu.VMEM((tq // FLASH_KEY_SPLIT, tq), BF16)] * (2 * FLASH_KEY_SPLIT)
        + [pltpu.VMEM((2, 1, tq), F32), pltpu.VMEM((2, 1, tq), F32),
                        pltpu.VMEM((HEAD_LANES, tq), F32), pltpu.VMEM((HEAD_LANES, tq), F32)],
        compiler_params=_cparams(("parallel", "parallel")),
        name=name,
    )(q, q, k, k, vt)


DIL_PLANES = 16
DIL_TQ = 128
DIL_UNROLL = 8


def _plane_slot(r16):
    return 4 * (r16 % 4) + r16 // 4


def _dilated_geometry(dil):
    planes = DIL_PLANES // dil
    rows = DIL_TQ // planes
    if dil == 1:
        order = [(c // 4) + 4 * (c % 4) for c in range(planes)]
    else:
        order = list(range(planes))
    return planes, rows, order


def _dilated_bias(dil, first):
    planes, rows, order = _dilated_geometry(dil)
    krows = 2 * rows
    a = np.arange(planes * rows)
    b = np.arange(planes * krows)
    uq = np.asarray(order)[a // rows] + planes * (a % rows + (0 if first else rows))
    uk = np.asarray(order)[b // krows] + planes * (b % krows)
    diff = uq[:, None] - uk[None, :]
    valid = (diff >= 0) & (diff <= DIL_TQ)
    return jnp.asarray(np.where(valid, 0.0, NEG_INF), F32)


def _dilated_kernel(q_ref, k_ref, v_ref, b1f_ref, b1r_ref, b4f_ref, b4r_ref, b16f_ref, b16r_ref, o_ref,
                    q_st, k_st, v_st, m_sc, l_sc, acc_sc, *, rows_per_plane):
    pair = pl.program_id(1)
    n_pairs = DIL_HEADS // 2
    lane = lax.broadcasted_iota(jnp.int32, (1, LANES), 1)
    lo = lane < HEAD_LANES
    q_lo = (lane % HEAD_LANES) < (HEAD_LANES // 2)
    tq = DIL_TQ

    def plane_lanes(r16):
        return pl.ds(pl.multiple_of((r16 * n_pairs + pair) * LANES, LANES), LANES)

    for r16 in range(DIL_PLANES):
        slot = _plane_slot(r16)
        q_st[slot] = q_ref[0, :, plane_lanes(r16)].astype(F32)
        k_st[slot] = k_ref[0, :, plane_lanes(r16)].astype(F32)
        v_st[slot] = v_ref[0, :, plane_lanes(r16)].astype(F32)
    m_sc[...] = jnp.full_like(m_sc, NEG_INF)
    l_sc[...] = jnp.zeros_like(l_sc)
    acc_sc[...] = jnp.zeros_like(acc_sc)

    def block(planes, rows, slot0, row0, first, bias_ref):
        qsl = (pl.ds(slot0, planes), pl.ds(row0, rows), slice(None))
        k_row0 = row0 if first else row0 - rows
        ksl = (pl.ds(slot0, planes), pl.ds(k_row0, 2 * rows), slice(None))
        q = q_st[qsl].reshape(tq, LANES)
        k = k_st[ksl].reshape(-1, LANES).astype(BF16)
        v = v_st[ksl].reshape(-1, LANES).astype(BF16)
        zero = jnp.zeros_like(q)
        qs = (jnp.where(q_lo, q, zero).astype(BF16), jnp.where(q_lo, zero, q).astype(BF16))
        bias = bias_ref[...]
        v_ext = jnp.concatenate([v, jnp.ones_like(v)], axis=1)
        maxes, sums, pvs = [], [], []
        for h in range(2):
            s = _nt_dot(qs[h], k) + bias
            mh = jnp.max(s, axis=1, keepdims=True)
            p = jnp.exp2(s - mh)
            maxes.append(mh)
            pvl = _dot(p.astype(BF16), v_ext)
            pvs.append(pvl[:, :LANES])
            sums.append(pvl[:, LANES:])
        m_blk = jnp.where(lo, maxes[0], maxes[1])
        m_old = m_sc[qsl].reshape(tq, LANES)
        m_new = jnp.maximum(m_old, m_blk)
        a_old = jnp.exp2(m_old - m_new)
        a_blk = jnp.exp2(m_blk - m_new)
        l_new = a_old * l_sc[qsl].reshape(tq, LANES) + a_blk * jnp.where(lo, sums[0], sums[1])
        acc_new = a_old * acc_sc[qsl].reshape(tq, LANES) + a_blk * jnp.where(lo, pvs[0], pvs[1])
        m_sc[qsl] = m_new.reshape(planes, rows, LANES)
        l_sc[qsl] = l_new.reshape(planes, rows, LANES)
        acc_sc[qsl] = acc_new.reshape(planes, rows, LANES)

    bias_refs = {1: (b1f_ref, b1r_ref), 4: (b4f_ref, b4r_ref), 16: (b16f_ref, b16r_ref)}
    for window, dil in DILATED_PAIRS:
        assert window // dil == tq
        planes, rows, _ = _dilated_geometry(dil)
        nblk = rows_per_plane // rows
        bf_ref, br_ref = bias_refs[dil]

        def first_body(sub, carry, planes=planes, rows=rows, bf_ref=bf_ref):
            block(planes, rows, sub * planes, 0, True, bf_ref)
            return carry

        lax.fori_loop(0, dil, first_body, 0, unroll=min(dil, DIL_UNROLL))

        if nblk > 1:
            def rest_body(t, carry, planes=planes, rows=rows, nblk=nblk, br_ref=br_ref):
                sub = t // (nblk - 1)
                blk = t % (nblk - 1) + 1
                block(planes, rows, sub * planes, pl.multiple_of(blk * rows, rows), False, br_ref)
                return carry

            lax.fori_loop(0, dil * (nblk - 1), rest_body, 0, unroll=DIL_UNROLL)

    for r16 in range(DIL_PLANES):
        slot = _plane_slot(r16)
        o_ref[0, :, plane_lanes(r16)] = (acc_sc[slot] / l_sc[slot]).astype(o_ref.dtype)


def _dilated_attention(qc, kc, vc):
    B, rpp, width = qc.shape
    n_pairs = DIL_HEADS // 2
    assert rpp % DIL_TQ == 0 and rpp >= 2 * DIL_TQ

    spec = pl.BlockSpec((1, rpp, width), lambda b, p: (b, 0, 0))
    biases = [_dilated_bias(d, f) for _, d in DILATED_PAIRS for f in (True, False)]
    bspecs = [pl.BlockSpec(bb.shape, lambda b, p: (0, 0)) for bb in biases]
    plane = pltpu.VMEM((DIL_PLANES, rpp, LANES), F32)
    return pl.pallas_call(
        functools.partial(_dilated_kernel, rows_per_plane=rpp),
        out_shape=jax.ShapeDtypeStruct((B, rpp, width), BF16),
        grid=(B, n_pairs),
        in_specs=[spec, spec, spec] + bspecs,
        out_specs=spec,
        scratch_shapes=[plane] * 6,
        compiler_params=_cparams(("parallel", "arbitrary")),
        name="dilated",
    )(qc, kc, vc, *biases)


def _tn_dot(a_t, b):
    return lax.dot_general(a_t, b, (((0,), (0,)), ((), ())), preferred_element_type=F32)


def _mix_xattn_kernel(oat_ref, obt_ref, oc_ref, x_ref, wa_ref, wb_ref, wc_ref, g_ref, b_ref,
                      k_ref, v_ref, wq_ref, wo_ref, g2_ref, b2_ref, o_ref, *oc_scs):
    nd = len(oc_scs) * LANES
    rows_per_plane = oc_scs[0].shape[0] // DIL_PLANES
    for r in range(DIL_PLANES):
        rows = pl.ds(r, rows_per_plane, stride=DIL_PLANES)
        for pblk, sc in enumerate(oc_scs):
            lanes = slice(r * nd + pblk * LANES, r * nd + (pblk + 1) * LANES)
            sc[rows, :] = oc_ref[0, :, lanes].astype(F32)
    oc = jnp.concatenate([sc[...] for sc in oc_scs], axis=1).astype(BF16)
    y = (_tn_dot(oat_ref[0], wa_ref[...]) + _tn_dot(obt_ref[0], wb_ref[...])
         + _dot(oc, wc_ref[...]))
    x1 = _layer_norm(DEEPNORM_ALPHA * x_ref[0] + y, g_ref[...], b_ref[...])
    o_ref[0] = _xattn_rows(x1, k_ref, v_ref, wq_ref, wo_ref, g2_ref, b2_ref)


def _mix_xattn(oa_t, ob_t, oc, x, wa, wb, wc, g, b, k_mem, v_mem, w_q, w_o, g2, b2, l, tm):
    B, S, D = x.shape
    M = k_mem.shape[1]
    mspec = pl.BlockSpec((1, M, D), lambda bb, i: (bb, 0, 0))

    def cols(a):
        return pl.BlockSpec((1, a.shape[1], tm), lambda bb, i: (bb, 0, i))

    def rows(a):
        return pl.BlockSpec((1, tm, a.shape[2]), lambda bb, i: (bb, i, 0))

    return pl.pallas_call(
        _mix_xattn_kernel,
        out_shape=jax.ShapeDtypeStruct((B, S, D), F32),
        grid=(B, S // tm),
        in_specs=[cols(oa_t), cols(ob_t),
                  pl.BlockSpec((1, tm // DIL_PLANES, oc.shape[2]), lambda bb, i: (bb, i, 0)),
                  rows(x)] + [_layer_slab(a, l) for a in (wa, wb, wc, g, b)]
        + [mspec, mspec] + [_layer_slab(a, l) for a in (w_q, w_o, g2, b2)],
        out_specs=rows(x),
        scratch_shapes=[pltpu.VMEM((tm, LANES), F32)] * (oc.shape[2] // DIL_PLANES // LANES),
        compiler_params=_cparams(("parallel", "parallel")),
        name="mix_xattn",
    )(oa_t, ob_t, oc, x, wa, wb, wc, g, b, k_mem, v_mem, w_q, w_o, g2, b2)


def _mem_kv_kernel(mem_ref, w_ref, k_ref, v_ref):
    kv = _dot(mem_ref[0].astype(BF16), w_ref[...])
    d = k_ref.shape[-1]
    k_ref[0] = kv[:, :d].astype(BF16)
    v_ref[0] = kv[:, d:].astype(BF16)


def _mem_kv(mem, w_kv, l):
    B, M, D = mem.shape
    spec = pl.BlockSpec((1, M, D), lambda b: (b, 0, 0))
    return pl.pallas_call(
        _mem_kv_kernel,
        out_shape=(jax.ShapeDtypeStruct((B, M, D), BF16),) * 2,
        grid=(B,),
        in_specs=[spec, _layer_slab(w_kv, l)],
        out_specs=(spec, spec),
        compiler_params=_cparams(("parallel",)),
        name="mem_kv",
    )(mem, w_kv)


def _xattn_rows(x, k_ref, v_ref, wq_ref, wo_ref, g_ref, b_ref):
    d = x.shape[-1]
    dh = d // XATTN_HEADS
    q = (_dot(x.astype(BF16), wq_ref[...]) * (dh ** -0.5 * LOG2E)).astype(BF16)
    outs = []
    for h in range(XATTN_HEADS):
        sl = slice(h * dh, (h + 1) * dh)
        s = _nt_dot(q[:, sl], k_ref[0, :, sl])
        p = jnp.exp2(s - jnp.max(s, axis=1, keepdims=True))
        o = _dot(p.astype(BF16), v_ref[0, :, sl])
        outs.append((o / jnp.sum(p, axis=1, keepdims=True)).astype(BF16))
    o_all = jnp.concatenate(outs, axis=1)
    y = _dot(o_all, wo_ref[...])
    return _layer_norm(DEEPNORM_ALPHA * x + y, g_ref[...], b_ref[...])


def _split_bf16(a):
    hi = a.astype(BF16)
    return hi, (a - hi.astype(F32)).astype(BF16)


def _router_gates_t(x, rw_t, rbias):
    x_hi, x_lo = _split_bf16(x)
    w_hi, w_lo = _split_bf16(rw_t)
    t1 = _nt_dot(jnp.concatenate([w_hi, w_lo], axis=0), x_hi)
    logits_t = t1[:N_EXPERTS] + (t1[N_EXPERTS:] + _nt_dot(w_hi, x_lo))
    scores = 1.0 / (1.0 + jnp.exp(-logits_t))
    biased = scores + rbias
    sc = [scores[e:e + 1, :] for e in range(N_EXPERTS)]
    bs = [biased[e:e + 1, :] for e in range(N_EXPERTS)]
    epg = EXPERTS_PER_GROUP

    def beats(a, b, a_first):
        return (a >= b) if a_first else (a > b)

    grp = []
    for gi in range(N_GROUPS):
        v = bs[gi * epg:(gi + 1) * epg]
        best = None
        for a in range(epg):
            for b in range(a + 1, epg):
                pair = v[a] + v[b]
                best = pair if best is None else jnp.maximum(best, pair)
        grp.append(best)
    gates, g_sels = [], []
    for gi in range(N_GROUPS):
        g_sel = None
        for gj in range(N_GROUPS):
            if gj == gi:
                continue
            w = beats(grp[gi], grp[gj], gi < gj)
            g_sel = w if g_sel is None else (g_sel & w)
        g_sels.append(g_sel)
        in_top = []
        for a in range(epg):
            ea = gi * epg + a
            n_above = jnp.zeros(bs[ea].shape, jnp.int32)
            for b in range(epg):
                if b == a:
                    continue
                eb = gi * epg + b
                n_above = n_above + beats(bs[eb], bs[ea], b < a).astype(jnp.int32)
            in_top.append(g_sel & (n_above < 2))
        denom = None
        for a in range(epg):
            term = jnp.where(in_top[a], sc[gi * epg + a], 0.0)
            denom = term if denom is None else denom + term
        denom = jnp.where(g_sel, denom, 1.0)
        for a in range(epg):
            gates.append(jnp.where(in_top[a], sc[gi * epg + a] / denom, 0.0))
    return gates, g_sels


def _moe_kernel(x_ref, rwt_ref, rb_ref, wg_ref, wu_ref, wd_ref, g_ref, b_ref, o_ref,
                gt_sc, sel_sc, rank_sc, acc_sc, xb_sc, *, tm):
    grp = pl.program_id(1)
    epg = EXPERTS_PER_GROUP

    @pl.when(grp == 0)
    def _():
        ts = min(tm, ROUTER_ROWS)
        before = (lax.broadcasted_iota(jnp.int32, (ts, ts), 0)
                  < lax.broadcasted_iota(jnp.int32, (ts, ts), 1)).astype(BF16)
        seen = jnp.zeros((8, 1), F32)
        for r in range(tm // ts):
            cols = slice(r * ts, (r + 1) * ts)
            x = x_ref[cols, :]
            gates, g_sels = _router_gates_t(x, rwt_ref[...], rb_ref[...])
            zeros4 = [jnp.zeros((1, ts), F32)] * (8 - epg)
            for gi in range(N_GROUPS):
                gt_sc[gi, :, cols] = jnp.concatenate(gates[gi * epg:(gi + 1) * epg] + zeros4, axis=0)
            sel8 = jnp.concatenate([s.astype(F32) for s in g_sels] + [jnp.zeros((1, ts), F32)] * (8 - N_GROUPS),
                                   axis=0)
            sel_sc[:, cols] = sel8
            rank_sc[:, cols] = _dot(sel8.astype(BF16), before) + seen
            seen = seen + jnp.sum(sel8, axis=1, keepdims=True)
            xb_sc[cols, :] = x.astype(BF16)
        acc_sc[...] = jnp.zeros_like(acc_sc)

    sel = sel_sc[pl.ds(grp, 1), :] > 0.5
    rank = rank_sc[pl.ds(grp, 1), :].astype(jnp.int32)
    n_tok = jnp.sum(sel_sc[pl.ds(grp, 1), :]).astype(jnp.int32)
    gt = gt_sc[grp]
    g_hi = gt.astype(BF16).astype(F32)
    g_r = gt - g_hi
    g_mid = g_r.astype(BF16).astype(F32)
    g_lo = (g_r - g_mid).astype(BF16).astype(F32)
    g3 = jnp.concatenate([g_hi, g_mid, g_lo, jnp.zeros_like(g_hi)], axis=0).astype(BF16)
    wd_all = wd_ref[...].reshape(epg * D_EXPERT, -1)

    def chunk(first_slot, ch):
        slot = lax.broadcasted_iota(jnp.int32, (ch, tm), 0) + first_slot
        onehot = jnp.where((slot == rank) & sel, 1.0, 0.0).astype(BF16)
        xg = _dot(onehot, xb_sc[...]).astype(BF16)
        gs3 = _nt_dot(onehot, g3)
        gs = gs3[:, 0:8] + (gs3[:, 8:16] + gs3[:, 16:24])
        acts = []
        for e in range(epg):
            hg = _dot(xg, wg_ref[e])
            hu = _dot(xg, wu_ref[e])
            acts.append(((hg / (1.0 + jnp.exp(-hg))) * hu * gs[:, e:e + 1]).astype(BF16))
        y = _dot(jnp.concatenate(acts, axis=1), wd_all).astype(BF16)
        acc_sc[...] += _tn_dot(onehot, y)

    n_wide = (n_tok + (MOE_CHUNK - MOE_TAIL_CHUNK - 1)) // MOE_CHUNK

    def wide(c, carry):
        chunk(c * MOE_CHUNK, MOE_CHUNK)
        return carry

    lax.fori_loop(0, n_wide, wide, 0)

    @pl.when(n_tok > n_wide * MOE_CHUNK)
    def _():
        chunk(n_wide * MOE_CHUNK, MOE_TAIL_CHUNK)

    @pl.when(grp == N_GROUPS - 1)
    def _():
        o_ref[...] = _layer_norm(DEEPNORM_ALPHA * x_ref[...] + acc_sc[...], g_ref[...], b_ref[...])


ROUTER_ROWS = 512
MOE_CHUNK = 256
MOE_TAIL_CHUNK = 128


def _moe(x2d, rw_t, rbias, wg, wu, wd, g, b, l, tm):
    T, D = x2d.shape
    xspec = pl.BlockSpec((tm, D), lambda i, e: (i, 0))
    epg = EXPERTS_PER_GROUP

    def full(a):
        return pl.BlockSpec(a.shape, lambda i, e: (0, 0))

    return pl.pallas_call(
        functools.partial(_moe_kernel, tm=tm),
        out_shape=jax.ShapeDtypeStruct((T, D), F32),
        grid=(T // tm, N_GROUPS),
        in_specs=[xspec, full(rw_t), full(rbias),
                  pl.BlockSpec((None, epg, D, D_EXPERT), lambda i, e: (l, e, 0, 0)),
                  pl.BlockSpec((None, epg, D, D_EXPERT), lambda i, e: (l, e, 0, 0)),
                  pl.BlockSpec((None, epg, D_EXPERT, D), lambda i, e: (l, e, 0, 0)),
                  _layer_slab(g, l), _layer_slab(b, l)],
        out_specs=xspec,
        scratch_shapes=[pltpu.VMEM((N_GROUPS, 8, tm), F32), pltpu.VMEM((8, tm), F32), pltpu.VMEM((8, tm), F32),
                        pltpu.VMEM((tm, D), F32), pltpu.VMEM((tm, D), BF16)],
        compiler_params=_cparams(("parallel", "arbitrary")),
        name="moe",
    )(x2d, rw_t, rbias, wg, wu, wd, g, b)


def _rope_constants():
    half_d = DIL_HEAD_DIM // 2
    inv_d = ROPE_THETA ** (-jnp.arange(half_d, dtype=F32) / half_d)
    half_a = MLA_ROPE_DIM // 2
    inv_a = ROPE_THETA ** (-jnp.arange(half_a, dtype=F32) / half_a)
    lane = np.arange(LANES)
    in_rope = (lane >= MLA_NOPE_DIM) & (lane < MLA_NOPE_DIM + MLA_ROPE_DIM)
    invf = jnp.where(jnp.asarray(lane < half_d), inv_d[lane % half_d],
                     jnp.where(jnp.asarray(in_rope), inv_a[(lane - MLA_NOPE_DIM) % half_a], 0.0))
    return invf.reshape(1, LANES)


def _layer_weights(w_in, b_forget, q_gain, kv_gain, w_uq, w_ukv):
    D = w_in.shape[0]
    cuts = np.cumsum([MLA_Q_RANK, MLA_KV_RANK, MLA_ROPE_DIM, 3 * FOX_HEADS * FOX_HEAD_DIM, FOX_HEADS])
    w_cq, w_ckv, w_kr, w_fox, w_ff, w_dil = jnp.split(w_in, cuts.tolist(), axis=1)
    half = MLA_ROPE_DIM // 2
    w_kr_rot = jnp.concatenate([-w_kr[:, half:], w_kr[:, :half]], axis=1)

    def place_rope(w):
        return jnp.pad(w, ((0, 0), (MLA_NOPE_DIM, LANES - MLA_NOPE_DIM - MLA_ROPE_DIM)))

    def place_gate(w):
        rep = jnp.repeat(w, 3, axis=1)
        return jnp.pad(rep, ((0, 0), (FOX_HEAD_DIM, LANES - FOX_HEAD_DIM - 3 * FOX_HEADS)))

    wa = jnp.concatenate([w_cq, w_ckv, place_rope(w_kr), place_rope(w_kr_rot),
                          place_gate(w_ff)], axis=1)

    nd = DIL_HEADS * DIL_HEAD_DIM

    def pair_layout(w):
        half = DIL_HEAD_DIM // 2
        w5 = w.reshape(D, DIL_HEADS // 2, 2, 2, half)
        return w5.transpose(0, 1, 3, 2, 4).reshape(D, nd)

    wdil = jnp.concatenate([pair_layout(w_dil[:, :nd]), pair_layout(w_dil[:, nd:2 * nd]), w_dil[:, 2 * nd:]],
                           axis=1)

    dq = MLA_NOPE_DIM + MLA_ROPE_DIM
    uq = w_uq.reshape(MLA_Q_RANK, MLA_HEADS, dq)
    uq_rope = uq[:, :, MLA_NOPE_DIM:]
    uq_rot = jnp.concatenate([-uq_rope[:, :, half:], uq_rope[:, :, :half]], axis=2)
    pad_tail = LANES - dq
    uq_plain = jnp.pad(uq, ((0, 0), (0, 0), (0, pad_tail))).reshape(MLA_Q_RANK, MLA_HEADS * LANES)
    uq_rotp = jnp.pad(uq_rot, ((0, 0), (0, 0), (MLA_NOPE_DIM, pad_tail))).reshape(MLA_Q_RANK, MLA_HEADS * LANES)
    wuq = jnp.concatenate([uq_plain, uq_rotp], axis=1)

    ukv = w_ukv.reshape(MLA_KV_RANK, MLA_HEADS, MLA_NOPE_DIM + MLA_V_DIM)
    wuk = jnp.pad(ukv[:, :, :MLA_NOPE_DIM], ((0, 0), (0, 0), (0, LANES - MLA_NOPE_DIM)))
    wuk = wuk.reshape(MLA_KV_RANK, MLA_HEADS * LANES)
    wuv = ukv[:, :, MLA_NOPE_DIM:].reshape(MLA_KV_RANK, MLA_HEADS * MLA_V_DIM)

    bf = place_gate(b_forget.astype(F32).reshape(1, -1))
    nf = FOX_HEADS * FOX_HEAD_DIM
    return dict(wa=wa.astype(BF16), wfox=w_fox[:, :2 * nf].astype(BF16),
                wfoxv_t=w_fox[:, 2 * nf:].T.astype(BF16), wdil=wdil.astype(BF16),
                qg=q_gain.reshape(1, -1).astype(F32), kvg=kv_gain.reshape(1, -1).astype(F32),
                wuq=wuq.astype(BF16), wuk=wuk.astype(BF16), wuv_t=wuv.T.astype(BF16), bf=bf)


def _tiles(S):
    tm = min(512, S)
    return dict(rope=min(512, S), proj=tm, flash=min(512, S), mix=tm, moe=min(1024, S))


def kernel(x, mem, positions, w_in, b_forget, mla_q_gain, mla_kv_gain, mla_w_uq, mla_w_ukv, w_mix_out, ln_mix_g, ln_mix_b, xattn_w_q, xattn_w_kv, xattn_w_o, ln_mem_g, ln_mem_b, router_w, router_bias, expert_w_gate, expert_w_up, expert_w_down, ln_ffn_g, ln_ffn_b):
    B, S, D = x.shape
    depth = w_in.shape[0]
    t = _tiles(S)
    tabs = _rope_tables(positions, _rope_constants(), t["rope"])
    rw_t = router_w.T.astype(F32)
    rbias = router_bias.reshape(-1, 1).astype(F32)

    def rows(v):
        return v.reshape(depth, 1, -1).astype(F32)

    wts = jax.vmap(_layer_weights)(w_in, b_forget, mla_q_gain, mla_kv_gain, mla_w_uq, mla_w_ukv)
    na, nb = MLA_HEADS * MLA_V_DIM, FOX_HEADS * FOX_HEAD_DIM
    w_mix = w_mix_out.astype(BF16)
    w_mix_a, w_mix_b, w_mix_c = w_mix[:, :na], w_mix[:, na:na + nb], w_mix[:, na + nb:]
    w_q, w_kv, w_o = xattn_w_q.astype(BF16), xattn_w_kv.astype(BF16), xattn_w_o.astype(BF16)
    wg, wu, wd = expert_w_gate.astype(BF16), expert_w_up.astype(BF16), expert_w_down.astype(BF16)
    g_mix, b_mix, g_mem, b_mem = rows(ln_mix_g), rows(ln_mix_b), rows(ln_mem_g), rows(ln_mem_b)
    g_ffn, b_ffn = rows(ln_ffn_g), rows(ln_ffn_b)

    for l in range(depth):
        qa, ka, va_t, qb, kb, vb_t, qc, kc, vc = _proj(x, wts, l, tabs, t["proj"])
        oa_t = _flash_attention(qa, ka, va_t, MLA_HEADS, t["flash"], "mla_flash")
        ob_t = _flash_attention(qb, kb, vb_t, FOX_HEADS, t["flash"], "fox_flash")
        oc = _dilated_attention(qc, kc, vc)
        k_mem, v_mem = _mem_kv(mem, w_kv, l)
        x = _mix_xattn(oa_t, ob_t, oc, x, w_mix_a, w_mix_b, w_mix_c, g_mix, b_mix,
                       k_mem, v_mem, w_q, w_o, g_mem, b_mem, l, t["mix"])
        x2d = _moe(x.reshape(B * S, D), rw_t, rbias, wg, wu, wd, g_ffn, b_ffn, l, t["moe"])
        x = x2d.reshape(B, S, D)
    return x
```

```python
import functools
import math

import numpy as np
import jax
import jax.numpy as jnp
from jax import lax
from jax.experimental import pallas as pl
from jax.experimental.pallas import tpu as pltpu

F32 = jnp.float32
BF16 = jnp.bfloat16

DEPTH = 4
MLA_HEADS = 4
MLA_Q_RANK = 256
MLA_KV_RANK = 128
MLA_NOPE_DIM = 64
MLA_ROPE_DIM = 32
MLA_V_DIM = 64
FOX_HEADS = 6
FOX_HEAD_DIM = 64
DIL_HEADS = 6
DIL_HEAD_DIM = 64
DILATED_PAIRS = ((128, 1), (512, 4), (2048, 16))
XATTN_HEADS = 4
N_EXPERTS = 16
N_GROUPS = 4
EXPERTS_PER_GROUP = N_EXPERTS // N_GROUPS
D_EXPERT = 256
ROPE_THETA = 10000.0
NORM_EPS = 1e-5
NEG_INF = -1e30
DEEPNORM_ALPHA = (2 * DEPTH) ** 0.25
LOG2E = math.log2(math.e)

LANES = 128
HEAD_LANES = 64
VMEM_LIMIT_BYTES = 52 * 1024 * 1024

A_CQ = 0
A_CKV = A_CQ + MLA_Q_RANK
A_KR = A_CKV + MLA_KV_RANK
A_KRR = A_KR + LANES
A_FF = A_KRR + LANES


def _cparams(sem):
    return pltpu.CompilerParams(dimension_semantics=sem, vmem_limit_bytes=VMEM_LIMIT_BYTES)


def _layer_slab(stacked, l):
    zeros = (0,) * (stacked.ndim - 1)
    return pl.BlockSpec((None,) + stacked.shape[1:], lambda *_: (l,) + zeros)


def _nt_dot(a, b):
    return lax.dot_general(a, b, (((1,), (1,)), ((), ())), preferred_element_type=F32)


def _dot(a, b):
    return jnp.dot(a, b, preferred_element_type=F32)


def _layer_norm(y, g, b):
    mu = jnp.mean(y, axis=-1, keepdims=True)
    yc = y - mu
    var = jnp.mean(yc * yc, axis=-1, keepdims=True)
    return yc * lax.rsqrt(var + NORM_EPS) * g + b


def _rms_norm(y, g):
    ms = jnp.mean(y * y, axis=-1, keepdims=True)
    return y * lax.rsqrt(ms + NORM_EPS) * g


def _rope_table_kernel(pos_ref, invf_ref, cosa_ref, sina_ref, cosd_ref, sind_ref):
    lane = lax.broadcasted_iota(jnp.int32, (1, LANES), 1)
    ang = pos_ref[0].astype(F32) * invf_ref[...]
    cs = jnp.cos(ang)
    sn = jnp.sin(ang)
    in_a = (lane >= MLA_NOPE_DIM) & (lane < MLA_NOPE_DIM + MLA_ROPE_DIM)
    cosa_ref[0] = jnp.where(in_a, cs, 1.0)
    sina_ref[0] = jnp.where(in_a, sn, 0.0)
    half = DIL_HEAD_DIM // 2
    c32 = jnp.where(lane < half, cs, 0.0)
    s32 = jnp.where(lane < half, sn, 0.0)
    cd, sd = c32, s32
    for shift in (half, 2 * half, 3 * half):
        cd = cd + pltpu.roll(c32, shift, 1)
        sd = sd + pltpu.roll(s32, shift, 1)
    cosd_ref[0] = cd
    sind_ref[0] = jnp.where(lane < HEAD_LANES, -sd, sd)


def _rope_tables(positions, invf, ts):
    B, S = positions.shape
    pos3 = positions.reshape(B, S, 1)
    spec = pl.BlockSpec((1, ts, LANES), lambda b, i: (b, i, 0))
    vec = pl.BlockSpec((1, LANES), lambda b, i: (0, 0))
    return pl.pallas_call(
        _rope_table_kernel,
        out_shape=(jax.ShapeDtypeStruct((B, S, LANES), F32),) * 4,
        grid=(B, S // ts),
        in_specs=[pl.BlockSpec((1, ts, 1), lambda b, i: (b, i, 0)), vec],
        out_specs=(spec,) * 4,
        compiler_params=_cparams(("parallel", "parallel")),
        name="rope_tables",
    )(pos3, invf)


def _proj_kernel(x_ref, wa_ref, wfox_ref, wfoxv_ref, wdil_ref, qg_ref, kvg_ref, wuq_ref, wuk_ref,
                 wuv_ref, bf_ref, cosa_ref, sina_ref, cosd_ref, sind_ref,
                 qa_ref, ka_ref, va_ref, qb_ref, kb_ref, vb_ref,
                 qc_ref, kc_ref, vc_ref, carry_c, *dil_scs, tm):
    i = pl.program_id(1)
    xb = x_ref[0].astype(BF16)

    ha = _dot(xb, wa_ref[...])
    cqn = _rms_norm(ha[:, A_CQ:A_CQ + MLA_Q_RANK], qg_ref[...]).astype(BF16)
    ckvn = _rms_norm(ha[:, A_CKV:A_CKV + MLA_KV_RANK], kvg_ref[...]).astype(BF16)
    cos_a = cosa_ref[0]
    sin_a = sina_ref[0]
    q2 = _dot(cqn, wuq_ref[...])
    k_nope = _dot(ckvn, wuk_ref[...])
    k_rot = ha[:, A_KR:A_KR + LANES] * cos_a + ha[:, A_KRR:A_KRR + LANES] * sin_a
    q_scale = (MLA_NOPE_DIM + MLA_ROPE_DIM) ** -0.5 * LOG2E
    n_half = MLA_HEADS * LANES
    for h in range(MLA_HEADS):
        sl = slice(h * LANES, (h + 1) * LANES)
        qh = q2[:, sl] * cos_a + q2[:, n_half + h * LANES:n_half + (h + 1) * LANES] * sin_a
        qa_ref[0, :, sl] = (qh * q_scale).astype(BF16)
        ka_ref[0, :, sl] = (k_nope[:, sl] + k_rot).astype(BF16)
    va_ref[0] = _nt_dot(wuv_ref[...], ckvn).astype(BF16)

    lane = lax.broadcasted_iota(jnp.int32, (1, LANES), 1)
    bias_lanes = (lane >= FOX_HEAD_DIM) & (lane < FOX_HEAD_DIM + 3 * FOX_HEADS)
    z = ha[:, A_FF:A_FF + LANES] + bf_ref[...]
    logf = jnp.minimum(z, 0.0) - jnp.log(1.0 + jnp.exp(-jnp.abs(z)))
    logf = jnp.where(bias_lanes, logf, 0.0)

    @pl.when(i == 0)
    def _():
        carry_c[...] = jnp.zeros_like(carry_c)

    p0 = logf.astype(BF16).astype(F32)
    r0 = logf - p0
    p1 = r0.astype(BF16).astype(F32)
    p2 = (r0 - p1).astype(BF16)
    packed = jnp.concatenate([(p0 + pltpu.roll(p1, HEAD_LANES, 1)).astype(BF16), p2], axis=1)
    r = lax.broadcasted_iota(jnp.int32, (tm, tm), 0)
    c = lax.broadcasted_iota(jnp.int32, (tm, tm), 1)
    lower = (c <= r).astype(BF16)
    cum = _dot(lower, packed)
    cum_a = cum[:, :LANES]
    fcol = jnp.where(bias_lanes, cum_a + pltpu.roll(cum_a, HEAD_LANES, 1) + cum[:, LANES:], 0.0) + carry_c[...]
    carry_c[...] = fcol[tm - 1:tm, :]

    nb = fcol * (-LOG2E)
    hi = nb.astype(BF16).astype(F32)
    r1 = nb - hi
    mid = r1.astype(BF16).astype(F32)
    lo3 = (r1 - mid).astype(BF16).astype(F32)
    piece = (lane - FOX_HEAD_DIM) % 3
    k_bias = jnp.where(piece == 0, hi, jnp.where(piece == 1, mid, lo3))

    hf = _dot(xb, wfox_ref[...])
    nf = FOX_HEADS * FOX_HEAD_DIM
    dims = lane < FOX_HEAD_DIM
    fq_scale = FOX_HEAD_DIM ** -0.5 * LOG2E
    for h in range(FOX_HEADS):
        blk = slice((h // 2) * LANES, (h // 2 + 1) * LANES)
        out = slice(h * LANES, (h + 1) * LANES)
        qh = hf[:, blk]
        kh = hf[:, nf + (h // 2) * LANES:nf + (h // 2 + 1) * LANES]
        if h % 2:
            qh = pltpu.roll(qh, HEAD_LANES, 1)
            kh = pltpu.roll(kh, HEAD_LANES, 1)
        own = (lane >= FOX_HEAD_DIM + 3 * h) & (lane < FOX_HEAD_DIM + 3 * h + 3)
        qb_ref[0, :, out] = jnp.where(dims, qh * fq_scale, jnp.where(own, 1.0, 0.0)).astype(BF16)
        kb_ref[0, :, out] = jnp.where(dims, kh, k_bias).astype(BF16)
    vb_ref[0] = _nt_dot(wfoxv_ref[...], xb).astype(BF16)

    hd = _dot(xb, wdil_ref[...])
    cos_d = cosd_ref[0]
    sin_d = sind_ref[0]
    nd = DIL_HEADS * DIL_HEAD_DIM
    for pblk in range(nd // LANES):
        sl = slice(pblk * LANES, (pblk + 1) * LANES)
        qh = hd[:, sl]
        kh = hd[:, nd + pblk * LANES:nd + (pblk + 1) * LANES]
        qh = qh * cos_d + pltpu.roll(qh, HEAD_LANES, 1) * sin_d
        kh = kh * cos_d + pltpu.roll(kh, HEAD_LANES, 1) * sin_d
        dil_scs[pblk][...] = qh * (DIL_HEAD_DIM ** -0.5 * LOG2E)
        dil_scs[3 + pblk][...] = kh
        dil_scs[6 + pblk][...] = hd[:, 2 * nd + pblk * LANES:2 * nd + (pblk + 1) * LANES]
    for a, ref in enumerate((qc_ref, kc_ref, vc_ref)):
        for r in range(DIL_PLANES):
            rows = pl.ds(r, tm // DIL_PLANES, stride=DIL_PLANES)
            for pblk in range(nd // LANES):
                lanes = slice(r * nd + pblk * LANES, r * nd + (pblk + 1) * LANES)
                ref[0, :, lanes] = dil_scs[3 * a + pblk][rows, :].astype(BF16)


def _proj(x, wts, l, tabs, tm):
    B, S, D = x.shape
    cos_a, sin_a, cos_d, sin_d = tabs
    grid = (B, S // tm)

    def tok(width, dtype):
        return (jax.ShapeDtypeStruct((B, S, width), dtype),
                pl.BlockSpec((1, tm, width), lambda b, i: (b, i, 0)))

    tab = pl.BlockSpec((1, tm, LANES), lambda b, i: (b, i, 0))
    nd = DIL_HEADS * DIL_HEAD_DIM

    def planes(dtype):
        return (jax.ShapeDtypeStruct((B, S // DIL_PLANES, DIL_PLANES * nd), dtype),
                pl.BlockSpec((1, tm // DIL_PLANES, DIL_PLANES * nd), lambda b, i: (b, i, 0)))

    def tok_t(height, dtype):
        return (jax.ShapeDtypeStruct((B, height, S), dtype),
                pl.BlockSpec((1, height, tm), lambda b, i: (b, 0, i)))

    outs = [tok(MLA_HEADS * LANES, BF16), tok(MLA_HEADS * LANES, BF16),
            tok_t(MLA_HEADS * MLA_V_DIM, BF16),
            tok(FOX_HEADS * LANES, BF16), tok(FOX_HEADS * LANES, BF16),
            tok_t(FOX_HEADS * FOX_HEAD_DIM, BF16),
            planes(BF16), planes(BF16), planes(BF16)]
    w_list = [wts["wa"], wts["wfox"], wts["wfoxv_t"], wts["wdil"], wts["qg"], wts["kvg"], wts["wuq"],
              wts["wuk"], wts["wuv_t"], wts["bf"]]
    return pl.pallas_call(
        functools.partial(_proj_kernel, tm=tm),
        out_shape=tuple(o[0] for o in outs),
        grid=grid,
        in_specs=[pl.BlockSpec((1, tm, D), lambda b, i: (b, i, 0))] + [_layer_slab(w, l) for w in w_list]
        + [tab, tab, tab, tab],
        out_specs=tuple(o[1] for o in outs),
        scratch_shapes=[pltpu.VMEM((1, LANES), F32)] + [pltpu.VMEM((tm, LANES), F32)] * (3 * nd // LANES),
        compiler_params=_cparams(("parallel", "arbitrary")),
        name="proj",
    )(x, *w_list, cos_a, sin_a, cos_d, sin_d)


FLASH_CHUNK = 32


FLASH_BLOCKS_PER_TRIP = 2
FLASH_KEY_SPLIT = 4
ONES_ROWS = 16


def _flash_kernel(q0_ref, q1_ref, k0_ref, k1_ref, vt_ref, o_ref, *scratch, tq, n_q):
    def q_tile(i, carry):
        _flash_q_tile(i, q0_ref, q1_ref, k0_ref, k1_ref, vt_ref, o_ref, scratch, tq=tq)
        return carry

    lax.fori_loop(0, n_q, q_tile, 0)


def _flash_q_tile(i, q0_ref, q1_ref, k0_ref, k1_ref, vt_ref, o_ref, scratch, *, tq):
    qs = pl.multiple_of(i * tq, tq)
    q_refs = (q0_ref, q1_ref)
    k_refs = (k0_ref, k1_ref)
    n_chain = 2 * FLASH_KEY_SPLIT
    s_scs = scratch[:n_chain]
    p_scs = scratch[n_chain:2 * n_chain]
    m_sc, l_sc, acc0_sc, acc1_sc = scratch[2 * n_chain:]
    acc_scs = (acc0_sc, acc1_sc)
    dv = HEAD_LANES
    tk = tq // FLASH_KEY_SPLIT
    n_chunks = tk // FLASH_CHUNK

    m_sc[...] = jnp.full_like(m_sc, NEG_INF)
    l_sc[...] = jnp.zeros_like(l_sc)
    acc0_sc[...] = jnp.zeros_like(acc0_sc)
    acc1_sc[...] = jnp.zeros_like(acc1_sc)

    def step(j, masked):
        m_locs = []
        for h in range(2):
            for c in range(FLASH_KEY_SPLIT):
                ks = pl.multiple_of(j * tq + c * tk, tk)
                q0 = c * tk if masked else 0
                k = k_refs[h][0, pl.ds(ks, tk), :]
                q = q_refs[h][0, pl.ds(qs + q0, tq - q0), :]
                st = _nt_dot(k, q)
                if masked:
                    key = lax.broadcasted_iota(jnp.int32, (tk, tq - q0), 0)
                    qry = lax.broadcasted_iota(jnp.int32, (tk, tq - q0), 1)
                    st = jnp.where(key <= qry, st, NEG_INF)
                s_scs[h * FLASH_KEY_SPLIT + c][:, q0:] = st
                m8 = jnp.max(st.reshape(tk // 8, 8, tq - q0), axis=0)
                m_locs.append(jnp.max(m8, axis=0, keepdims=True))
        locs = []
        for h in range(2):
            for c in range(FLASH_KEY_SPLIT):
                ks = pl.multiple_of(j * tq + c * tk, tk)
                q0 = c * tk if masked else 0
                s_sc = s_scs[h * FLASH_KEY_SPLIT + c]
                p_sc = p_scs[h * FLASH_KEY_SPLIT + c]
                m_loc = m_locs[h * FLASH_KEY_SPLIT + c]
                for r in range(n_chunks):
                    rows = slice(r * FLASH_CHUNK, (r + 1) * FLASH_CHUNK)
                    p_sc[rows, q0:] = jnp.exp2(s_sc[rows, q0:] - m_loc).astype(BF16)
                vt = vt_ref[0, h * dv:(h + 1) * dv, pl.ds(ks, tk)]
                vt_ext = jnp.concatenate([vt, jnp.ones((ONES_ROWS, tk), BF16)], axis=0)
                pvl = _dot(vt_ext, p_sc[:, q0:])
                if q0:
                    m_loc = jnp.concatenate([jnp.full((1, q0), NEG_INF, F32), m_loc], axis=1)
                    pvl = jnp.concatenate([jnp.zeros((dv + ONES_ROWS, q0), F32), pvl], axis=1)
                locs.append((m_loc, pvl))
        for h in range(2):
            mine = locs[h * FLASH_KEY_SPLIT:(h + 1) * FLASH_KEY_SPLIT]
            m_prev = m_sc[h]
            m_new = m_prev
            for m_loc, _ in mine:
                m_new = jnp.maximum(m_new, m_loc)
            a_prev = jnp.exp2(m_prev - m_new)
            acc = acc_scs[h][...] * a_prev
            l = l_sc[h] * a_prev
            for m_loc, pvl in mine:
                a_loc = jnp.exp2(m_loc - m_new)
                acc = acc + pvl[:dv] * a_loc
                l = l + pvl[dv:dv + 1] * a_loc
            acc_scs[h][...] = acc
            l_sc[h] = l
            m_sc[h] = m_new

    per_trip = FLASH_BLOCKS_PER_TRIP

    def body(jj, carry):
        for u in range(per_trip):
            step(per_trip * jj + u, False)
        return carry

    n_trips = i // per_trip
    lax.fori_loop(0, n_trips, body, 0)

    def single(j, carry):
        step(j, False)
        return carry

    lax.fori_loop(n_trips * per_trip, i, single, 0)
    step(i, True)
    for h in range(2):
        o_ref[0, h * dv:(h + 1) * dv, pl.ds(qs, tq)] = (acc_scs[h][...] / l_sc[h]).astype(o_ref.dtype)


def _flash_attention(q, k, vt, n_heads, tq, name):
    B, S, _ = q.shape
    n_pairs = n_heads // 2
    dv2 = 2 * HEAD_LANES
    spec0 = pl.BlockSpec((1, S, LANES), lambda b, p: (b, 0, 2 * p))
    spec1 = pl.BlockSpec((1, S, LANES), lambda b, p: (b, 0, 2 * p + 1))
    vspec = pl.BlockSpec((1, dv2, S), lambda b, p: (b, p, 0))
    return pl.pallas_call(
        functools.partial(_flash_kernel, tq=tq, n_q=S // tq),
        out_shape=jax.ShapeDtypeStruct((B, n_heads * HEAD_LANES, S), BF16),
        grid=(B, n_pairs),
        in_specs=[spec0, spec1, spec0, spec1, vspec],
        out_specs=vspec,
        scratch_shapes=[pltpu.VMEM((tq // FLASH_KEY_SPLIT, tq), F32)] * (2 * FLASH_KEY_SPLIT)
        + [pltpu.VMEM((tq // FLASH_KEY_SPLIT, tq), BF16)] * (2 * FLASH_KEY_SPLIT)
        + [pltpu.VMEM((2, 1, tq), F32), pltpu.VMEM((2, 1, tq), F32),
                        pltpu.VMEM((HEAD_LANES, tq), F32), pltpu.VMEM((HEAD_LANES, tq), F32)],
        compiler_params=_cparams(("parallel", "parallel")),
        name=name,
    )(q, q, k, k, vt)


DIL_PLANES = 16
DIL_TQ = 128
DIL_UNROLL = 16


def _plane_slot(r16):
    return 4 * (r16 % 4) + r16 // 4


def _dilated_geometry(dil):
    planes = DIL_PLANES // dil
    rows = DIL_TQ // planes
    if dil == 1:
        order = [(c // 4) + 4 * (c % 4) for c in range(planes)]
    else:
        order = list(range(planes))
    return planes, rows, order


def _dilated_bias(dil, first):
    planes, rows, order = _dilated_geometry(dil)
    krows = 2 * rows
    a = np.arange(planes * rows)
    b = np.arange(planes * krows)
    uq = np.asarray(order)[a // rows] + planes * (a % rows + (0 if first else rows))
    uk = np.asarray(order)[b // krows] + planes * (b % krows)
    diff = uq[:, None] - uk[None, :]
    valid = (diff >= 0) & (diff <= DIL_TQ)
    return jnp.asarray(np.where(valid, 0.0, NEG_INF), F32)


def _dilated_kernel(q_ref, k_ref, v_ref, b1f_ref, b1r_ref, b4f_ref, b4r_ref, b16f_ref, b16r_ref, o_ref,
                    q_st, k_st, v_st, m_sc, l_sc, acc_sc, *, rows_per_plane):
    pair = pl.program_id(1)
    n_pairs = DIL_HEADS // 2
    lane = lax.broadcasted_iota(jnp.int32, (1, LANES), 1)
    lo = lane < HEAD_LANES
    q_lo = (lane % HEAD_LANES) < (HEAD_LANES // 2)
    tq = DIL_TQ

    def plane_lanes(r16):
        return pl.ds(pl.multiple_of((r16 * n_pairs + pair) * LANES, LANES), LANES)

    for r16 in range(DIL_PLANES):
        slot = _plane_slot(r16)
        q_st[slot] = q_ref[0, :, plane_lanes(r16)].astype(F32)
        k_st[slot] = k_ref[0, :, plane_lanes(r16)].astype(F32)
        v_st[slot] = v_ref[0, :, plane_lanes(r16)].astype(F32)
    m_sc[...] = jnp.full_like(m_sc, NEG_INF)
    l_sc[...] = jnp.zeros_like(l_sc)
    acc_sc[...] = jnp.zeros_like(acc_sc)

    def block(planes, rows, slot0, row0, first, bias_ref):
        qsl = (pl.ds(slot0, planes), pl.ds(row0, rows), slice(None))
        k_row0 = row0 if first else row0 - rows
        ksl = (pl.ds(slot0, planes), pl.ds(k_row0, 2 * rows), slice(None))
        q = q_st[qsl].reshape(tq, LANES)
        k = k_st[ksl].reshape(-1, LANES).astype(BF16)
        v = v_st[ksl].reshape(-1, LANES).astype(BF16)
        zero = jnp.zeros_like(q)
        qs = (jnp.where(q_lo, q, zero).astype(BF16), jnp.where(q_lo, zero, q).astype(BF16))
        bias = bias_ref[...]
        v_ext = jnp.concatenate([v, jnp.ones_like(v)], axis=1)
        maxes, sums, pvs = [], [], []
        for h in range(2):
            s = _nt_dot(qs[h], k) + bias
            mh = jnp.max(s, axis=1, keepdims=True)
            p = jnp.exp2(s - mh)
            maxes.append(mh)
            pvl = _dot(p.astype(BF16), v_ext)
            pvs.append(pvl[:, :LANES])
            sums.append(pvl[:, LANES:])
        m_blk = jnp.where(lo, maxes[0], maxes[1])
        m_old = m_sc[qsl].reshape(tq, LANES)
        m_new = jnp.maximum(m_old, m_blk)
        a_old = jnp.exp2(m_old - m_new)
        a_blk = jnp.exp2(m_blk - m_new)
        l_new = a_old * l_sc[qsl].reshape(tq, LANES) + a_blk * jnp.where(lo, sums[0], sums[1])
        acc_new = a_old * acc_sc[qsl].reshape(tq, LANES) + a_blk * jnp.where(lo, pvs[0], pvs[1])
        m_sc[qsl] = m_new.reshape(planes, rows, LANES)
        l_sc[qsl] = l_new.reshape(planes, rows, LANES)
        acc_sc[qsl] = acc_new.reshape(planes, rows, LANES)

    bias_refs = {1: (b1f_ref, b1r_ref), 4: (b4f_ref, b4r_ref), 16: (b16f_ref, b16r_ref)}
    for window, dil in DILATED_PAIRS:
        assert window // dil == tq
        planes, rows, _ = _dilated_geometry(dil)
        nblk = rows_per_plane // rows
        bf_ref, br_ref = bias_refs[dil]

        def first_body(sub, carry, planes=planes, rows=rows, bf_ref=bf_ref):
            block(planes, rows, sub * planes, 0, True, bf_ref)
            return carry

        lax.fori_loop(0, dil, first_body, 0, unroll=min(dil, DIL_UNROLL))

        if nblk > 1:
            def rest_body(t, carry, planes=planes, rows=rows, nblk=nblk, br_ref=br_ref):
                sub = t // (nblk - 1)
                blk = t % (nblk - 1) + 1
                block(planes, rows, sub * planes, pl.multiple_of(blk * rows, rows), False, br_ref)
                return carry

            lax.fori_loop(0, dil * (nblk - 1), rest_body, 0, unroll=DIL_UNROLL)

    for r16 in range(DIL_PLANES):
        slot = _plane_slot(r16)
        o_ref[0, :, plane_lanes(r16)] = (acc_sc[slot] / l_sc[slot]).astype(o_ref.dtype)


def _dilated_attention(qc, kc, vc):
    B, rpp, width = qc.shape
    n_pairs = DIL_HEADS // 2
    assert rpp % DIL_TQ == 0 and rpp >= 2 * DIL_TQ

    spec = pl.BlockSpec((1, rpp, width), lambda b, p: (b, 0, 0))
    biases = [_dilated_bias(d, f) for _, d in DILATED_PAIRS for f in (True, False)]
    bspecs = [pl.BlockSpec(bb.shape, lambda b, p: (0, 0)) for bb in biases]
    plane = pltpu.VMEM((DIL_PLANES, rpp, LANES), F32)
    return pl.pallas_call(
        functools.partial(_dilated_kernel, rows_per_plane=rpp),
        out_shape=jax.ShapeDtypeStruct((B, rpp, width), BF16),
        grid=(B, n_pairs),
        in_specs=[spec, spec, spec] + bspecs,
        out_specs=spec,
        scratch_shapes=[plane] * 6,
        compiler_params=_cparams(("parallel", "arbitrary")),
        name="dilated",
    )(qc, kc, vc, *biases)


def _tn_dot(a_t, b):
    return lax.dot_general(a_t, b, (((0,), (0,)), ((), ())), preferred_element_type=F32)


def _mix_xattn_kernel(oat_ref, obt_ref, oc_ref, x_ref, wa_ref, wb_ref, wc_ref, g_ref, b_ref,
                      k_ref, v_ref, wq_ref, wo_ref, g2_ref, b2_ref, o_ref, *oc_scs):
    nd = len(oc_scs) * LANES
    rows_per_plane = oc_scs[0].shape[0] // DIL_PLANES
    for r in range(DIL_PLANES):
        rows = pl.ds(r, rows_per_plane, stride=DIL_PLANES)
        for pblk, sc in enumerate(oc_scs):
            lanes = slice(r * nd + pblk * LANES, r * nd + (pblk + 1) * LANES)
            sc[rows, :] = oc_ref[0, :, lanes].astype(F32)
    oc = jnp.concatenate([sc[...] for sc in oc_scs], axis=1).astype(BF16)
    y = (_tn_dot(oat_ref[0], wa_ref[...]) + _tn_dot(obt_ref[0], wb_ref[...])
         + _dot(oc, wc_ref[...]))
    x1 = _layer_norm(DEEPNORM_ALPHA * x_ref[0] + y, g_ref[...], b_ref[...])
    o_ref[0] = _xattn_rows(x1, k_ref, v_ref, wq_ref, wo_ref, g2_ref, b2_ref)


def _mix_xattn(oa_t, ob_t, oc, x, wa, wb, wc, g, b, k_mem, v_mem, w_q, w_o, g2, b2, l, tm):
    B, S, D = x.shape
    M = k_mem.shape[1]
    mspec = pl.BlockSpec((1, M, D), lambda bb, i: (bb, 0, 0))

    def cols(a):
        return pl.BlockSpec((1, a.shape[1], tm), lambda bb, i: (bb, 0, i))

    def rows(a):
        return pl.BlockSpec((1, tm, a.shape[2]), lambda bb, i: (bb, i, 0))

    return pl.pallas_call(
        _mix_xattn_kernel,
        out_shape=jax.ShapeDtypeStruct((B, S, D), F32),
        grid=(B, S // tm),
        in_specs=[cols(oa_t), cols(ob_t),
                  pl.BlockSpec((1, tm // DIL_PLANES, oc.shape[2]), lambda bb, i: (bb, i, 0)),
                  rows(x)] + [_layer_slab(a, l) for a in (wa, wb, wc, g, b)]
        + [mspec, mspec] + [_layer_slab(a, l) for a in (w_q, w_o, g2, b2)],
        out_specs=rows(x),
        scratch_shapes=[pltpu.VMEM((tm, LANES), F32)] * (oc.shape[2] // DIL_PLANES // LANES),
        compiler_params=_cparams(("parallel", "parallel")),
        name="mix_xattn",
    )(oa_t, ob_t, oc, x, wa, wb, wc, g, b, k_mem, v_mem, w_q, w_o, g2, b2)


def _mem_kv_kernel(mem_ref, w_ref, k_ref, v_ref):
    kv = _dot(mem_ref[0].astype(BF16), w_ref[...])
    d = k_ref.shape[-1]
    k_ref[0] = kv[:, :d].astype(BF16)
    v_ref[0] = kv[:, d:].astype(BF16)


def _mem_kv(mem, w_kv, l):
    B, M, D = mem.shape
    spec = pl.BlockSpec((1, M, D), lambda b: (b, 0, 0))
    return pl.pallas_call(
        _mem_kv_kernel,
        out_shape=(jax.ShapeDtypeStruct((B, M, D), BF16),) * 2,
        grid=(B,),
        in_specs=[spec, _layer_slab(w_kv, l)],
        out_specs=(spec, spec),
        compiler_params=_cparams(("parallel",)),
        name="mem_kv",
    )(mem, w_kv)


def _xattn_rows(x, k_ref, v_ref, wq_ref, wo_ref, g_ref, b_ref):
    d = x.shape[-1]
    dh = d // XATTN_HEADS
    q = (_dot(x.astype(BF16), wq_ref[...]) * (dh ** -0.5 * LOG2E)).astype(BF16)
    outs = []
    for h in range(XATTN_HEADS):
        sl = slice(h * dh, (h + 1) * dh)
        s = _nt_dot(q[:, sl], k_ref[0, :, sl])
        p = jnp.exp2(s - jnp.max(s, axis=1, keepdims=True))
        o = _dot(p.astype(BF16), v_ref[0, :, sl])
        outs.append((o / jnp.sum(p, axis=1, keepdims=True)).astype(BF16))
    o_all = jnp.concatenate(outs, axis=1)
    y = _dot(o_all, wo_ref[...])
    return _layer_norm(DEEPNORM_ALPHA * x + y, g_ref[...], b_ref[...])


def _split_bf16(a):
    hi = a.astype(BF16)
    return hi, (a - hi.astype(F32)).astype(BF16)


def _router_gates_t(x, rw_t, rbias):
    x_hi, x_lo = _split_bf16(x)
    w_hi, w_lo = _split_bf16(rw_t)
    t1 = _nt_dot(jnp.concatenate([w_hi, w_lo], axis=0), x_hi)
    logits_t = t1[:N_EXPERTS] + (t1[N_EXPERTS:] + _nt_dot(w_hi, x_lo))
    scores = 1.0 / (1.0 + jnp.exp(-logits_t))
    biased = scores + rbias
    sc = [scores[e:e + 1, :] for e in range(N_EXPERTS)]
    bs = [biased[e:e + 1, :] for e in range(N_EXPERTS)]
    epg = EXPERTS_PER_GROUP

    def beats(a, b, a_first):
        return (a >= b) if a_first else (a > b)

    grp = []
    for gi in range(N_GROUPS):
        v = bs[gi * epg:(gi + 1) * epg]
        best = None
        for a in range(epg):
            for b in range(a + 1, epg):
                pair = v[a] + v[b]
                best = pair if best is None else jnp.maximum(best, pair)
        grp.append(best)
    gates, g_sels = [], []
    for gi in range(N_GROUPS):
        g_sel = None
        for gj in range(N_GROUPS):
            if gj == gi:
                continue
            w = beats(grp[gi], grp[gj], gi < gj)
            g_sel = w if g_sel is None else (g_sel & w)
        g_sels.append(g_sel)
        in_top = []
        for a in range(epg):
            ea = gi * epg + a
            n_above = jnp.zeros(bs[ea].shape, jnp.int32)
            for b in range(epg):
                if b == a:
                    continue
                eb = gi * epg + b
                n_above = n_above + beats(bs[eb], bs[ea], b < a).astype(jnp.int32)
            in_top.append(g_sel & (n_above < 2))
        denom = None
        for a in range(epg):
            term = jnp.where(in_top[a], sc[gi * epg + a], 0.0)
            denom = term if denom is None else denom + term
        denom = jnp.where(g_sel, denom, 1.0)
        for a in range(epg):
            gates.append(jnp.where(in_top[a], sc[gi * epg + a] / denom, 0.0))
    return gates, g_sels


def _moe_kernel(x_ref, rwt_ref, rb_ref, wg_ref, wu_ref, wd_ref, g_ref, b_ref, o_ref,
                gt_sc, sel_sc, rank_sc, acc_sc, xb_sc, *, tm):
    grp = pl.program_id(1)
    epg = EXPERTS_PER_GROUP

    @pl.when(grp == 0)
    def _():
        ts = min(tm, ROUTER_ROWS)
        before = (lax.broadcasted_iota(jnp.int32, (ts, ts), 0)
                  < lax.broadcasted_iota(jnp.int32, (ts, ts), 1)).astype(BF16)
        seen = jnp.zeros((8, 1), F32)
        for r in range(tm // ts):
            cols = slice(r * ts, (r + 1) * ts)
            x = x_ref[cols, :]
            gates, g_sels = _router_gates_t(x, rwt_ref[...], rb_ref[...])
            zeros4 = [jnp.zeros((1, ts), F32)] * (8 - epg)
            for gi in range(N_GROUPS):
                gt_sc[gi, :, cols] = jnp.concatenate(gates[gi * epg:(gi + 1) * epg] + zeros4, axis=0)
            sel8 = jnp.concatenate([s.astype(F32) for s in g_sels] + [jnp.zeros((1, ts), F32)] * (8 - N_GROUPS),
                                   axis=0)
            sel_sc[:, cols] = sel8
            rank_sc[:, cols] = _dot(sel8.astype(BF16), before) + seen
            seen = seen + jnp.sum(sel8, axis=1, keepdims=True)
            xb_sc[cols, :] = x.astype(BF16)
        acc_sc[...] = jnp.zeros_like(acc_sc)

    sel = sel_sc[pl.ds(grp, 1), :] > 0.5
    rank = rank_sc[pl.ds(grp, 1), :].astype(jnp.int32)
    n_tok = jnp.sum(sel_sc[pl.ds(grp, 1), :]).astype(jnp.int32)
    gt = gt_sc[grp]
    g_hi = gt.astype(BF16).astype(F32)
    g_r = gt - g_hi
    g_mid = g_r.astype(BF16).astype(F32)
    g_lo = (g_r - g_mid).astype(BF16).astype(F32)
    g3 = jnp.concatenate([g_hi, g_mid, g_lo, jnp.zeros_like(g_hi)], axis=0).astype(BF16)
    wd_all = wd_ref[...].reshape(epg * D_EXPERT, -1)

    def chunk(first_slot, ch):
        slot = lax.broadcasted_iota(jnp.int32, (ch, tm), 0) + first_slot
        onehot = jnp.where((slot == rank) & sel, 1.0, 0.0).astype(BF16)
        xg = _dot(onehot, xb_sc[...]).astype(BF16)
        gs3 = _nt_dot(onehot, g3)
        gs = gs3[:, 0:8] + (gs3[:, 8:16] + gs3[:, 16:24])
        acts = []
        for e in range(epg):
            hg = _dot(xg, wg_ref[e])
            hu = _dot(xg, wu_ref[e])
            acts.append(((hg / (1.0 + jnp.exp(-hg))) * hu * gs[:, e:e + 1]).astype(BF16))
        y = _dot(jnp.concatenate(acts, axis=1), wd_all).astype(BF16)
        acc_sc[...] += _tn_dot(onehot, y)

    n_wide = (n_tok + (MOE_CHUNK - MOE_TAIL_CHUNK - 1)) // MOE_CHUNK

    def wide(c, carry):
        chunk(c * MOE_CHUNK, MOE_CHUNK)
        return carry

    lax.fori_loop(0, n_wide, wide, 0)

    @pl.when(n_tok > n_wide * MOE_CHUNK)
    def _():
        chunk(n_wide * MOE_CHUNK, MOE_TAIL_CHUNK)

    @pl.when(grp == N_GROUPS - 1)
    def _():
        o_ref[...] = _layer_norm(DEEPNORM_ALPHA * x_ref[...] + acc_sc[...], g_ref[...], b_ref[...])


ROUTER_ROWS = 512
MOE_CHUNK = 256
MOE_TAIL_CHUNK = 128


def _moe(x2d, rw_t, rbias, wg, wu, wd, g, b, l, tm):
    T, D = x2d.shape
    xspec = pl.BlockSpec((tm, D), lambda i, e: (i, 0))
    epg = EXPERTS_PER_GROUP

    def full(a):
        return pl.BlockSpec(a.shape, lambda i, e: (0, 0))

    return pl.pallas_call(
        functools.partial(_moe_kernel, tm=tm),
        out_shape=jax.ShapeDtypeStruct((T, D), F32),
        grid=(T // tm, N_GROUPS),
        in_specs=[xspec, full(rw_t), full(rbias),
                  pl.BlockSpec((None, epg, D, D_EXPERT), lambda i, e: (l, e, 0, 0)),
                  pl.BlockSpec((None, epg, D, D_EXPERT), lambda i, e: (l, e, 0, 0)),
                  pl.BlockSpec((None, epg, D_EXPERT, D), lambda i, e: (l, e, 0, 0)),
                  _layer_slab(g, l), _layer_slab(b, l)],
        out_specs=xspec,
        scratch_shapes=[pltpu.VMEM((N_GROUPS, 8, tm), F32), pltpu.VMEM((8, tm), F32), pltpu.VMEM((8, tm), F32),
                        pltpu.VMEM((tm, D), F32), pltpu.VMEM((tm, D), BF16)],
        compiler_params=_cparams(("parallel", "arbitrary")),
        name="moe",
    )(x2d, rw_t, rbias, wg, wu, wd, g, b)


def _rope_constants():
    half_d = DIL_HEAD_DIM // 2
    inv_d = ROPE_THETA ** (-jnp.arange(half_d, dtype=F32) / half_d)
    half_a = MLA_ROPE_DIM // 2
    inv_a = ROPE_THETA ** (-jnp.arange(half_a, dtype=F32) / half_a)
    lane = np.arange(LANES)
    in_rope = (lane >= MLA_NOPE_DIM) & (lane < MLA_NOPE_DIM + MLA_ROPE_DIM)
    invf = jnp.where(jnp.asarray(lane < half_d), inv_d[lane % half_d],
                     jnp.where(jnp.asarray(in_rope), inv_a[(lane - MLA_NOPE_DIM) % half_a], 0.0))
    return invf.reshape(1, LANES)


def _layer_weights(w_in, b_forget, q_gain, kv_gain, w_uq, w_ukv):
    D = w_in.shape[0]
    cuts = np.cumsum([MLA_Q_RANK, MLA_KV_RANK, MLA_ROPE_DIM, 3 * FOX_HEADS * FOX_HEAD_DIM, FOX_HEADS])
    w_cq, w_ckv, w_kr, w_fox, w_ff, w_dil = jnp.split(w_in, cuts.tolist(), axis=1)
    half = MLA_ROPE_DIM // 2
    w_kr_rot = jnp.concatenate([-w_kr[:, half:], w_kr[:, :half]], axis=1)

    def place_rope(w):
        return jnp.pad(w, ((0, 0), (MLA_NOPE_DIM, LANES - MLA_NOPE_DIM - MLA_ROPE_DIM)))

    def place_gate(w):
        rep = jnp.repeat(w, 3, axis=1)
        return jnp.pad(rep, ((0, 0), (FOX_HEAD_DIM, LANES - FOX_HEAD_DIM - 3 * FOX_HEADS)))

    wa = jnp.concatenate([w_cq, w_ckv, place_rope(w_kr), place_rope(w_kr_rot),
                          place_gate(w_ff)], axis=1)

    nd = DIL_HEADS * DIL_HEAD_DIM

    def pair_layout(w):
        half = DIL_HEAD_DIM // 2
        w5 = w.reshape(D, DIL_HEADS // 2, 2, 2, half)
        return w5.transpose(0, 1, 3, 2, 4).reshape(D, nd)

    wdil = jnp.concatenate([pair_layout(w_dil[:, :nd]), pair_layout(w_dil[:, nd:2 * nd]), w_dil[:, 2 * nd:]],
                           axis=1)

    dq = MLA_NOPE_DIM + MLA_ROPE_DIM
    uq = w_uq.reshape(MLA_Q_RANK, MLA_HEADS, dq)
    uq_rope = uq[:, :, MLA_NOPE_DIM:]
    uq_rot = jnp.concatenate([-uq_rope[:, :, half:], uq_rope[:, :, :half]], axis=2)
    pad_tail = LANES - dq
    uq_plain = jnp.pad(uq, ((0, 0), (0, 0), (0, pad_tail))).reshape(MLA_Q_RANK, MLA_HEADS * LANES)
    uq_rotp = jnp.pad(uq_rot, ((0, 0), (0, 0), (MLA_NOPE_DIM, pad_tail))).reshape(MLA_Q_RANK, MLA_HEADS * LANES)
    wuq = jnp.concatenate([uq_plain, uq_rotp], axis=1)

    ukv = w_ukv.reshape(MLA_KV_RANK, MLA_HEADS, MLA_NOPE_DIM + MLA_V_DIM)
    wuk = jnp.pad(ukv[:, :, :MLA_NOPE_DIM], ((0, 0), (0, 0), (0, LANES - MLA_NOPE_DIM)))
    wuk = wuk.reshape(MLA_KV_RANK, MLA_HEADS * LANES)
    wuv = ukv[:, :, MLA_NOPE_DIM:].reshape(MLA_KV_RANK, MLA_HEADS * MLA_V_DIM)

    bf = place_gate(b_forget.astype(F32).reshape(1, -1))
    nf = FOX_HEADS * FOX_HEAD_DIM
    return dict(wa=wa.astype(BF16), wfox=w_fox[:, :2 * nf].astype(BF16),
                wfoxv_t=w_fox[:, 2 * nf:].T.astype(BF16), wdil=wdil.astype(BF16),
                qg=q_gain.reshape(1, -1).astype(F32), kvg=kv_gain.reshape(1, -1).astype(F32),
                wuq=wuq.astype(BF16), wuk=wuk.astype(BF16), wuv_t=wuv.T.astype(BF16), bf=bf)


def _tiles(S):
    tm = min(512, S)
    return dict(rope=min(512, S), proj=tm, flash=min(512, S), mix=tm, moe=min(1024, S))


def kernel(x, mem, positions, w_in, b_forget, mla_q_gain, mla_kv_gain, mla_w_uq, mla_w_ukv, w_mix_out, ln_mix_g, ln_mix_b, xattn_w_q, xattn_w_kv, xattn_w_o, ln_mem_g, ln_mem_b, router_w, router_bias, expert_w_gate, expert_w_up, expert_w_down, ln_ffn_g, ln_ffn_b):
    B, S, D = x.shape
    depth = w_in.shape[0]
    t = _tiles(S)
    tabs = _rope_tables(positions, _rope_constants(), t["rope"])
    rw_t = router_w.T.astype(F32)
    rbias = router_bias.reshape(-1, 1).astype(F32)

    def rows(v):
        return v.reshape(depth, 1, -1).astype(F32)

    wts = jax.vmap(_layer_weights)(w_in, b_forget, mla_q_gain, mla_kv_gain, mla_w_uq, mla_w_ukv)
    na, nb = MLA_HEADS * MLA_V_DIM, FOX_HEADS * FOX_HEAD_DIM
    w_mix = w_mix_out.astype(BF16)
    w_mix_a, w_mix_b, w_mix_c = w_mix[:, :na], w_mix[:, na:na + nb], w_mix[:, na + nb:]
    w_q, w_kv, w_o = xattn_w_q.astype(BF16), xattn_w_kv.astype(BF16), xattn_w_o.astype(BF16)
    wg, wu, wd = expert_w_gate.astype(BF16), expert_w_up.astype(BF16), expert_w_down.astype(BF16)
    g_mix, b_mix, g_mem, b_mem = rows(ln_mix_g), rows(ln_mix_b), rows(ln_mem_g), rows(ln_mem_b)
    g_ffn, b_ffn = rows(ln_ffn_g), rows(ln_ffn_b)

    for l in range(depth):
        qa, ka, va_t, qb, kb, vb_t, qc, kc, vc = _proj(x, wts, l, tabs, t["proj"])
        oa_t = _flash_attention(qa, ka, va_t, MLA_HEADS, t["flash"], "mla_flash")
        ob_t = _flash_attention(qb, kb, vb_t, FOX_HEADS, t["flash"], "fox_flash")
        oc = _dilated_attention(qc, kc, vc)
        k_mem, v_mem = _mem_kv(mem, w_kv, l)
        x = _mix_xattn(oa_t, ob_t, oc, x, w_mix_a, w_mix_b, w_mix_c, g_mix, b_mix,
                       k_mem, v_mem, w_q, w_o, g_mem, b_mem, l, t["mix"])
        x2d = _moe(x.reshape(B * S, D), rw_t, rbias, wg, wu, wd, g_ffn, b_ffn, l, t["moe"])
        x = x2d.reshape(B, S, D)
    return x
```
